```python
import jax, jax.numpy as jnp
from jax import lax
import numpy as np

D_MODEL = 2048
BATCH = 4
SEQ = 2048
DEPTH = 1
DEC_BATCH = 128
DEC_SEQ = 4
PAST_LEN = 2048
PAGE_SIZE = 128

A_HEAD_DIM = 128
A_HEADS_PER_GROUP = 4
A_GROUPS = ((128, 1), (512, 4), (2048, 16))
A_N_GROUPS = 3
A_HEADS = A_HEADS_PER_GROUP * A_N_GROUPS
A_WIDTH = A_HEADS * A_HEAD_DIM
A_OUT_WIDTH = A_HEADS_PER_GROUP * A_HEAD_DIM
ATTN_SCALE = A_HEAD_DIM ** -0.5
REL_BUCKETS = 32
REL_MAX_DIST = 2048
B_HEADS = 8
B_KEY_DIM = 128
B_VAL_DIM = 128
B_WIDTH = B_HEADS * B_VAL_DIM
B_CHUNK = 32
N_EXPERTS = 32
TOP_K = 4
D_FF = 2048
SWIGLU_ALPHA = 1.702
SWIGLU_LIMIT = 7.0
MOE_BLOCK = 128
NORM_EPS = 1e-6
NEG_INF = -1e30
IN_WIDTH = 3 * A_WIDTH + 2 * B_HEADS * B_KEY_DIM + 2 * B_WIDTH + 2 * D_MODEL

kernel_name = 'hybrid_dilated_attn_hgrn2_moe_step'


def _rmsnorm(x, w):
    xf = x.astype(jnp.float32)
    y = xf * lax.rsqrt(jnp.mean(xf * xf, axis=-1, keepdims=True) + NORM_EPS)
    return (y * w.astype(jnp.float32)).astype(x.dtype)


def _t5_bucket(dist):
    dist = np.asarray(dist, np.int64)
    max_exact = REL_BUCKETS // 2
    ratio = np.log(np.maximum(dist, 1) / max_exact) / np.log(REL_MAX_DIST / max_exact)
    large = np.minimum(max_exact + (ratio * (REL_BUCKETS - max_exact)).astype(np.int64), REL_BUCKETS - 1)
    return np.where(dist < max_exact, dist, large).astype(np.int32)


def _masked_softmax_lse(logits, mask):
    logits = jnp.where(mask, logits, NEG_INF)
    mx = jnp.max(logits, axis=-1, keepdims=True)
    p = jnp.exp(logits - mx)
    den = jnp.sum(p, axis=-1, keepdims=True)
    return p / den, (mx + jnp.log(den))[..., 0]


def _band_dilated_attention(q, k, v, tab, window, dilation):
    bsz, seq, nh, dh = q.shape
    n = window // dilation
    m = seq // dilation
    nb = -(-m // n)
    mp = nb * n

    def to_blocks(t):
        t = t.reshape(bsz, m, dilation, nh, dh).transpose(0, 2, 1, 3, 4)
        t = jnp.pad(t, ((0, 0), (0, 0), (0, mp - m), (0, 0), (0, 0)))
        return t.reshape(bsz, dilation, nb, n, nh, dh)

    def with_prev(t):
        prev = jnp.pad(t, ((0, 0), (0, 0), (1, 0), (0, 0), (0, 0), (0, 0)))[:, :, :-1]
        return jnp.concatenate([prev, t], axis=3)

    qb = to_blocks(q)
    kk = with_prev(to_blocks(k))
    vv = with_prev(to_blocks(v))
    qi = np.arange(n)[:, None]
    kc = np.arange(2 * n)[None, :]
    j = n + qi - kc
    valid = (j >= 0) & (j < n)
    mask = valid[None] & ((np.arange(nb)[:, None, None] > 0) | (kc[None] >= n))
    bias = jnp.transpose(tab[_t5_bucket(np.clip(j, 0, n - 1) * dilation)], (2, 0, 1)).astype(jnp.float32)
    logits = jnp.einsum('brnqhd,brnkhd->brnhqk', qb, kk, preferred_element_type=jnp.float32) * ATTN_SCALE + bias
    p, lse = _masked_softmax_lse(logits, mask[:, None])
    out = jnp.einsum('brnhqk,brnkhd->brnqhd', p, vv.astype(jnp.float32))
    lse = jnp.transpose(lse, (0, 1, 2, 4, 3))

    def from_blocks(t):
        t = t.reshape((bsz, dilation, mp) + t.shape[4:])[:, :, :m]
        return jnp.moveaxis(t, 1, 2).reshape((bsz, seq) + t.shape[3:])

    return from_blocks(out), from_blocks(lse)


def _gather_dilated_attention(q, k_new, v_new, kv_buf, tab, window, dilation):
    t_new = q.shape[1]
    buf_len = kv_buf.shape[1]
    n = window // dilation
    ext_k = jnp.concatenate([kv_buf[:, :, 0].astype(k_new.dtype), k_new], axis=1)
    ext_v = jnp.concatenate([kv_buf[:, :, 1].astype(v_new.dtype), v_new], axis=1)
    idx = buf_len + np.arange(t_new)[:, None] - dilation * np.arange(n)[None, :]
    idx_c = np.maximum(idx, 0)
    kg = ext_k[:, idx_c]
    vg = ext_v[:, idx_c]
    bias = tab[_t5_bucket(dilation * np.arange(n))].T.astype(jnp.float32)
    logits = jnp.einsum('bthd,btjhd->bthj', q, kg, preferred_element_type=jnp.float32) * ATTN_SCALE + bias
    p, lse = _masked_softmax_lse(logits, (idx >= 0)[:, None, :])
    out = jnp.einsum('bthj,btjhd->bthd', p, vg.astype(jnp.float32))
    return out, lse


def _hgrn2_chunkwise(q, k, v, logf, s0, chunk):
    bsz, t_len, nh, _ = q.shape
    dv = v.shape[-1]
    nc = t_len // chunk
    rs = lambda t: t.reshape(bsz, nc, chunk, nh, t.shape[-1])
    q, k, v, logf = rs(q), rs(k), rs(v), rs(logf)
    b = jnp.cumsum(logf, axis=2)
    b_last = b[:, :, -1:]
    q_dec = q * jnp.exp(b)
    k_inv = k * jnp.exp(-b)
    k_end = k * jnp.exp(b_last - b)
    causal = np.tril(np.ones((chunk, chunk), dtype=bool))
    att = jnp.where(causal, jnp.einsum('bncha,bnsha->bnhcs', q_dec, k_inv), 0.0)
    intra = jnp.einsum('bnhcs,bnshv->bnchv', att, v)
    ds = jnp.einsum('bncha,bnchv->bnhav', k_end, v)
    decay = jnp.exp(b_last[:, :, 0])

    def step(s, inp):
        d_n, ds_n = inp
        return d_n[..., None] * s + ds_n, s

    s_final, s_starts = lax.scan(step, s0, (jnp.moveaxis(decay, 1, 0), jnp.moveaxis(ds, 1, 0)))
    inter = jnp.einsum('bncha,bnhav->bnchv', q_dec, jnp.moveaxis(s_starts, 0, 1))
    return (intra + inter).reshape(bsz, t_len, nh, dv), s_final


def _project(xn, w_in, q_norm, k_norm, lb):
    bsz, t_len = xn.shape[:2]
    sizes = (A_WIDTH,) * 3 + (B_HEADS * B_KEY_DIM,) * 2 + (B_WIDTH,) * 2 + (D_MODEL,) * 2
    splits = [int(s) for s in np.cumsum(sizes)[:-1]]
    qa, ka, va, qb, fb, ib, ob, ga, gb = jnp.split(xn @ w_in, splits, axis=-1)
    heads = lambda t, dh: t.reshape(bsz, t_len, -1, dh)
    qa = _rmsnorm(heads(qa, A_HEAD_DIM), q_norm)
    ka = _rmsnorm(heads(ka, A_HEAD_DIM), k_norm)
    va = heads(va, A_HEAD_DIM)
    qb = jax.nn.silu(heads(qb, B_KEY_DIM).astype(jnp.float32))
    z = heads(fb, B_KEY_DIM).astype(jnp.float32)
    lbh = lb.reshape(B_HEADS, B_KEY_DIM)
    logf = jnp.log(lbh + (1.0 - lbh) * jax.nn.sigmoid(z))
    kb = (1.0 - lbh) * jax.nn.sigmoid(-z)
    ib = heads(ib, B_VAL_DIM).astype(jnp.float32)
    ob = heads(ob, B_VAL_DIM)
    return qa, ka, va, qb, kb, logf, ib, ob, ga, gb


def _moe(x, w_router, b_router, w_up, b_up, w_down, b_down):
    shp = x.shape
    xt = x.reshape(-1, shp[-1])
    n_tok = xt.shape[0]
    logits = (xt @ w_router).astype(jnp.float32) + b_router.astype(jnp.float32)
    top_val, top_idx = lax.top_k(logits, TOP_K)
    gate = jax.nn.softmax(top_val, axis=-1)
    n_asg = n_tok * TOP_K
    flat_e = top_idx.reshape(-1)
    order = jnp.argsort(flat_e)
    sorted_e = flat_e[order]
    sorted_tok = order // TOP_K
    sorted_gate = gate.reshape(-1)[order]
    counts = jnp.bincount(flat_e, length=N_EXPERTS)
    padded = (counts + MOE_BLOCK - 1) // MOE_BLOCK * MOE_BLOCK
    pad_end = jnp.cumsum(padded)
    pad_start = pad_end - padded
    start = jnp.cumsum(counts) - counts
    dest = pad_start[sorted_e] + jnp.arange(n_asg) - start[sorted_e]
    n_blocks = -(-n_asg // MOE_BLOCK) + N_EXPERTS
    cap = n_blocks * MOE_BLOCK
    buf_tok = jnp.full((cap,), n_tok, jnp.int32).at[dest].set(sorted_tok.astype(jnp.int32))
    buf_gate = jnp.zeros((cap,), jnp.float32).at[dest].set(sorted_gate)
    blk_e = jnp.minimum(jnp.searchsorted(pad_end, jnp.arange(n_blocks) * MOE_BLOCK, side='right'), N_EXPERTS - 1)
    x_pad = jnp.concatenate([xt, jnp.zeros((1, shp[-1]), xt.dtype)], axis=0)
    xb = x_pad[buf_tok].reshape(n_blocks, MOE_BLOCK, shp[-1])

    def expert_block(args):
        xi, e = args
        h = xi @ w_up[e] + b_up[e]
        glu = jnp.minimum(h[:, :D_FF], SWIGLU_LIMIT)
        lin = jnp.clip(h[:, D_FF:], -SWIGLU_LIMIT, SWIGLU_LIMIT)
        act = glu * jax.nn.sigmoid(SWIGLU_ALPHA * glu) * (lin + 1.0)
        return act @ w_down[e] + b_down[e]

    yb = lax.map(expert_block, (xb, blk_e)).reshape(cap, shp[-1])
    y = jax.ops.segment_sum(yb.astype(jnp.float32) * buf_gate[:, None], buf_tok, num_segments=n_tok + 1)[:n_tok]
    return y.astype(x.dtype).reshape(shp)


def _layer(x, kv_bufs, s0, rel_bias, mix, ffn):
    attn_norm, w_in, q_norm, k_norm, lb, g_norm, w_pa, w_pb, w_o = mix
    ffn_norm, w_router, b_router, w_up, b_up, w_down, b_down = ffn
    bsz, t_len = x.shape[:2]
    xn = _rmsnorm(x, attn_norm)
    qa, ka, va, qb, kb, logf, ib, ob, ga, gb = _project(xn, w_in, q_norm, k_norm, lb)
    outs, lses, kv_rows = [], [], []
    for g, (win, dil) in enumerate(A_GROUPS):
        hs = slice(g * A_HEADS_PER_GROUP, (g + 1) * A_HEADS_PER_GROUP)
        q_g, k_g, v_g, tab = qa[:, :, hs], ka[:, :, hs], va[:, :, hs], rel_bias[:, hs]
        rows = jnp.stack([k_g, v_g], axis=2)
        if kv_bufs is None:
            o, lse = _band_dilated_attention(q_g, k_g, v_g, tab, win, dil)
            kv_rows.append(rows[:, -min(win, t_len):])
        else:
            o, lse = _gather_dilated_attention(q_g, k_g, v_g, kv_bufs[g], tab, win, dil)
            kv_rows.append(rows)
        outs.append(o)
        lses.append(lse)
    wts = jax.nn.softmax(jnp.stack(lses), axis=0)
    o_a = jnp.sum(wts[..., None] * jnp.stack(outs), axis=0).reshape(bsz, t_len, A_OUT_WIDTH)
    if s0 is None:
        s0 = jnp.zeros((bsz, B_HEADS, B_KEY_DIM, B_VAL_DIM), jnp.float32)
        chunk = B_CHUNK
    else:
        chunk = t_len
    o_b, s_new = _hgrn2_chunkwise(qb, kb, ib, logf, s0.astype(jnp.float32), chunk)
    o_b = (_rmsnorm(o_b, g_norm) * jax.nn.silu(ob.astype(jnp.float32))).reshape(bsz, t_len, B_WIDTH)
    mixed = jax.nn.sigmoid(ga) * (o_a.astype(x.dtype) @ w_pa) + jax.nn.sigmoid(gb) * (o_b.astype(x.dtype) @ w_pb)
    h = x + mixed @ w_o
    y = h + _moe(_rmsnorm(h, ffn_norm), w_router, b_router, w_up, b_up, w_down, b_down)
    return y, kv_rows, s_new


def setup_inputs(seed: int = 0) -> dict:
    key = jax.random.key(seed)
    ks = jax.random.split(key, 24)
    f32 = jnp.float32
    nrm = lambda k, shape, scale: scale * jax.random.normal(k, shape, f32)
    bk = B_HEADS * B_KEY_DIM
    inp = {}
    inp['x_prompt'] = nrm(ks[0], (BATCH, SEQ, D_MODEL), 1.0)
    inp['x_sample'] = nrm(ks[1], (DEC_BATCH, DEC_SEQ, D_MODEL), 1.0)
    for i, (win, _) in enumerate(A_GROUPS):
        inp[f'cache_kv_w{win}'] = nrm(ks[2 + i], (DEPTH, DEC_BATCH, min(win, PAST_LEN), 2, A_HEADS_PER_GROUP, A_HEAD_DIM), 1.0)
    inp['state_hgrn'] = nrm(ks[5], (DEPTH, DEC_BATCH, B_HEADS, B_KEY_DIM, B_VAL_DIM), 0.3)
    inp['rel_bias'] = nrm(ks[6], (REL_BUCKETS, A_HEADS), 0.1)
    inp['attn_norm'] = 1.0 + nrm(ks[7], (DEPTH, D_MODEL), 0.01)
    inp['w_in'] = nrm(ks[8], (DEPTH, D_MODEL, IN_WIDTH), D_MODEL ** -0.5)
    inp['q_norm'] = 1.0 + nrm(ks[9], (DEPTH, A_HEAD_DIM), 0.01)
    inp['k_norm'] = 1.0 + nrm(ks[10], (DEPTH, A_HEAD_DIM), 0.01)
    inp['lb_raw'] = nrm(ks[11], (DEPTH + 1, bk), 0.1)
    inp['g_norm'] = 1.0 + nrm(ks[12], (DEPTH, B_VAL_DIM), 0.01)
    inp['w_pa'] = nrm(ks[13], (DEPTH, A_OUT_WIDTH, D_MODEL), A_OUT_WIDTH ** -0.5)
    inp['w_pb'] = nrm(ks[14], (DEPTH, B_WIDTH, D_MODEL), B_WIDTH ** -0.5)
    inp['w_o'] = nrm(ks[15], (DEPTH, D_MODEL, D_MODEL), D_MODEL ** -0.5)
    inp['ffn_norm'] = 1.0 + nrm(ks[16], (DEPTH, D_MODEL), 0.01)
    inp['w_router'] = nrm(ks[17], (DEPTH, D_MODEL, N_EXPERTS), D_MODEL ** -0.5)
    inp['b_router'] = nrm(ks[18], (DEPTH, N_EXPERTS), 0.01)
    inp['w_up'] = nrm(ks[19], (DEPTH, N_EXPERTS, D_MODEL, 2 * D_FF), D_MODEL ** -0.5)
    inp['b_up'] = nrm(ks[20], (DEPTH, N_EXPERTS, 2 * D_FF), 0.01)
    inp['w_down'] = nrm(ks[21], (DEPTH, N_EXPERTS, D_FF, D_MODEL), D_FF ** -0.5)
    inp['b_down'] = nrm(ks[22], (DEPTH, N_EXPERTS, D_MODEL), 0.01)
    return inp


def reference(x_prompt, x_sample, cache_kv_w128, cache_kv_w512, cache_kv_w2048, state_hgrn,
              rel_bias, attn_norm, w_in, q_norm, k_norm, lb_raw, g_norm, w_pa, w_pb, w_o,
              ffn_norm, w_router, b_router, w_up, b_up, w_down, b_down):
    lower_bounds = jnp.cumsum(jax.nn.softmax(lb_raw.astype(jnp.float32), axis=0), axis=0)
    caches = (cache_kv_w128, cache_kv_w512, cache_kv_w2048)
    y_prompt, y_sample = x_prompt, x_sample
    kv_p = [[] for _ in A_GROUPS]
    kv_s = [[] for _ in A_GROUPS]
    st_p, st_s = [], []
    for l in range(DEPTH):
        mix = (attn_norm[l], w_in[l], q_norm[l], k_norm[l], lower_bounds[l], g_norm[l], w_pa[l], w_pb[l], w_o[l])
        ffn = (ffn_norm[l], w_router[l], b_router[l], w_up[l], b_up[l], w_down[l], b_down[l])
        y_prompt, rows_p, s_p = _layer(y_prompt, None, None, rel_bias, mix, ffn)
        y_sample, rows_s, s_s = _layer(y_sample, tuple(c[l] for c in caches), state_hgrn[l], rel_bias, mix, ffn)
        for g in range(A_N_GROUPS):
            kv_p[g].append(rows_p[g])
            kv_s[g].append(rows_s[g])
        st_p.append(s_p)
        st_s.append(s_s)
    kvp = [jnp.stack(r) for r in kv_p]
    kvs = [jnp.stack(r) for r in kv_s]
    return (y_prompt, y_sample, kvp[0], kvp[1], kvp[2], jnp.stack(st_p), kvs[0], kvs[1], kvs[2], jnp.stack(st_s))
```

```python
import functools

import numpy as np
import jax
import jax.numpy as jnp
from jax import lax
from jax.experimental import pallas as pl
from jax.experimental.pallas import tpu as pltpu

F32 = jnp.float32
BF16 = jnp.bfloat16

D_MODEL = 2048
HEAD = 128
A_GROUPS = ((128, 1), (512, 4), (2048, 16))
A_HPG = 4
A_KEYS = 128
A_WIDTH = 3 * A_HPG * HEAD
A_OUT = A_HPG * HEAD
B_HEADS = 8
B_WIDTH = B_HEADS * HEAD
B_CHUNK = 32
REL_BUCKETS = 32
REL_MAX_DIST = 2048
N_EXPERTS = 32
TOP_K = 4
D_FF = 2048
SWIGLU_ALPHA = 1.702
SWIGLU_LIMIT = 7.0
NORM_EPS = 1e-6
NEG_INF = -1e30
ATTN_SCALE = HEAD ** -0.5
IN_WIDTH = 3 * A_WIDTH + 4 * B_WIDTH + 2 * D_MODEL
OFF_Q, OFF_K, OFF_V = 0, A_WIDTH, 2 * A_WIDTH
OFF_QB = 3 * A_WIDTH
OFF_FB, OFF_IB, OFF_OB = OFF_QB + B_WIDTH, OFF_QB + 2 * B_WIDTH, OFF_QB + 3 * B_WIDTH
OFF_GA = OFF_QB + 4 * B_WIDTH
OFF_GB = OFF_GA + D_MODEL

V7X_VMEM_LIMIT = 56 * 1024 * 1024

NT_DIMS = (((1,), (1,)), ((), ()))


def _sigmoid(x):
    return 1.0 / (1.0 + jnp.exp(-x))


def _cparams(sem):
    return pltpu.CompilerParams(dimension_semantics=sem, vmem_limit_bytes=V7X_VMEM_LIMIT)


PROJ_TM = 512
PROJ_TN = 1280


def _inproj_kernel(x_ref, nw_ref, w_ref, o_ref, wbf_ref):
    @pl.when(pl.program_id(1) == 0)
    def _():
        wbf_ref[...] = w_ref[...].astype(BF16)

    x = x_ref[...]
    ms = jnp.mean(x * x, axis=-1, keepdims=True)
    xn = (x * lax.rsqrt(ms + NORM_EPS) * nw_ref[...]).astype(BF16)
    o_ref[...] = jnp.dot(xn, wbf_ref[...], preferred_element_type=F32)


def _inproj(x, norm_w, w_in):
    m = x.shape[0]
    return pl.pallas_call(
        _inproj_kernel,
        out_shape=jax.ShapeDtypeStruct((m, IN_WIDTH), F32),
        grid=(IN_WIDTH // PROJ_TN, m // PROJ_TM),
        in_specs=[
            pl.BlockSpec((PROJ_TM, D_MODEL), lambda j, i: (i, 0)),
            pl.BlockSpec((1, D_MODEL), lambda j, i: (0, 0)),
            pl.BlockSpec((D_MODEL, PROJ_TN), lambda j, i: (0, j)),
        ],
        out_specs=pl.BlockSpec((PROJ_TM, PROJ_TN), lambda j, i: (i, j)),
        scratch_shapes=[pltpu.VMEM((D_MODEL, PROJ_TN), BF16)],
        compiler_params=_cparams(("arbitrary", "arbitrary")),
        name="inproj",
    )(x, norm_w.reshape(1, D_MODEL), w_in)


QKV_TM = 512


def _qkv_kernel(q_ref, k_ref, v_ref, qw_ref, kw_ref, qn_ref, kv0_ref, kv1_ref, kv2_ref):
    kv_refs = (kv0_ref, kv1_ref, kv2_ref)
    qw = qw_ref[...]
    kw = kw_ref[...]
    for h in range(3 * A_HPG):
        sl = slice(h * HEAD, (h + 1) * HEAD)
        q = q_ref[:, sl]
        qn_ref[:, sl] = q * lax.rsqrt(jnp.mean(q * q, axis=-1, keepdims=True) + NORM_EPS) * qw
        k = k_ref[:, sl]
        kn = k * lax.rsqrt(jnp.mean(k * k, axis=-1, keepdims=True) + NORM_EPS) * kw
        g, hh = divmod(h, A_HPG)
        kv_refs[g][:, hh * HEAD:(hh + 1) * HEAD] = kn
        kv_refs[g][:, A_OUT + hh * HEAD:A_OUT + (hh + 1) * HEAD] = v_ref[:, sl]


def _qkv(proj, q_norm, k_norm):
    m = proj.shape[0]
    kv_shape = jax.ShapeDtypeStruct((m, 2 * A_OUT), F32)
    return pl.pallas_call(
        _qkv_kernel,
        out_shape=(jax.ShapeDtypeStruct((m, A_WIDTH), F32), kv_shape, kv_shape, kv_shape),
        grid=(m // QKV_TM,),
        in_specs=[
            pl.BlockSpec((QKV_TM, A_WIDTH), lambda i: (i, OFF_Q // A_WIDTH)),
            pl.BlockSpec((QKV_TM, A_WIDTH), lambda i: (i, OFF_K // A_WIDTH)),
            pl.BlockSpec((QKV_TM, A_WIDTH), lambda i: (i, OFF_V // A_WIDTH)),
            pl.BlockSpec((1, HEAD), lambda i: (0, 0)),
            pl.BlockSpec((1, HEAD), lambda i: (0, 0)),
        ],
        out_specs=(
            pl.BlockSpec((QKV_TM, A_WIDTH), lambda i: (i, 0)),
            pl.BlockSpec((QKV_TM, 2 * A_OUT), lambda i: (i, 0)),
            pl.BlockSpec((QKV_TM, 2 * A_OUT), lambda i: (i, 0)),
            pl.BlockSpec((QKV_TM, 2 * A_OUT), lambda i: (i, 0)),
        ),
        compiler_params=_cparams(("arbitrary",)),
        name="qkv_norm",
    )(proj, proj, proj, q_norm.reshape(1, HEAD), k_norm.reshape(1, HEAD))


def _t5_bucket(dist):
    dist = np.asarray(dist, np.int64)
    max_exact = REL_BUCKETS // 2
    ratio = np.log(np.maximum(dist, 1) / max_exact) / np.log(REL_MAX_DIST / max_exact)
    large = np.minimum(max_exact + (ratio * (REL_BUCKETS - max_exact)).astype(np.int64), REL_BUCKETS - 1)
    return np.where(dist < max_exact, dist, large).astype(np.int32)


def _prompt_bias(rel_bias, g):
    _, dil = A_GROUPS[g]
    n = A_KEYS
    j = n + np.arange(n)[:, None] - np.arange(2 * n)[None, :]
    valid = (j >= 0) & (j < n)
    tab = rel_bias[:, g * A_HPG:(g + 1) * A_HPG].astype(F32)
    bias = jnp.transpose(tab[_t5_bucket(np.clip(j, 0, n - 1) * dil)], (2, 0, 1))
    return jnp.where(valid[None], bias, NEG_INF)


def _attn_prompt_kernel(q_ref, kp_ref, ko_ref, vp_ref, vo_ref, bias_ref, o_ref, l_ref):
    n = A_KEYS
    first_block = pl.program_id(2) == 0
    kc = lax.broadcasted_iota(jnp.int32, (n, 2 * n), 1)
    keep = jnp.logical_or(kc >= n, jnp.logical_not(first_block))
    for h in range(A_HPG):
        sl = slice(h * HEAD, (h + 1) * HEAD)
        q = q_ref[:, sl].astype(BF16)
        k2 = jnp.concatenate([kp_ref[:, sl], ko_ref[:, sl]], axis=0).astype(BF16)
        s = lax.dot_general(q, k2, NT_DIMS, preferred_element_type=F32) * ATTN_SCALE + bias_ref[h]
        s = jnp.where(keep, s, NEG_INF)
        mx = jnp.max(s, axis=-1, keepdims=True)
        p = jnp.exp(s - mx)
        den = jnp.sum(p, axis=-1, keepdims=True)
        v2 = jnp.concatenate([vp_ref[:, sl], vo_ref[:, sl]], axis=0).astype(BF16)
        o_ref[:, sl] = jnp.dot(p.astype(BF16), v2, preferred_element_type=F32) / den
        l_ref[:, sl] = jnp.broadcast_to(mx + jnp.log(den), (n, HEAD))


def _attn_prompt(qn, kv, bias, g, bsz, seq):
    win, dil = A_GROUPS[g]
    n = A_KEYS
    m = seq // dil
    nb = m // n
    qv = qn.reshape(bsz, m, dil * A_WIDTH)
    kvv = kv.reshape(bsz, m, dil * 2 * A_OUT)
    qpb = A_WIDTH // A_OUT
    prev = lambda i: jnp.maximum(i - 1, 0)
    blk = (None, n, A_OUT)
    out_sds = jax.ShapeDtypeStruct((bsz, m, dil * A_OUT), F32)
    o, lse = pl.pallas_call(
        _attn_prompt_kernel,
        out_shape=(out_sds, out_sds),
        grid=(bsz, dil, nb),
        in_specs=[
            pl.BlockSpec(blk, lambda b, r, i: (b, i, r * qpb + g)),
            pl.BlockSpec(blk, lambda b, r, i: (b, prev(i), 2 * r)),
            pl.BlockSpec(blk, lambda b, r, i: (b, i, 2 * r)),
            pl.BlockSpec(blk, lambda b, r, i: (b, prev(i), 2 * r + 1)),
            pl.BlockSpec(blk, lambda b, r, i: (b, i, 2 * r + 1)),
            pl.BlockSpec((A_HPG, n, 2 * n), lambda b, r, i: (0, 0, 0)),
        ],
        out_specs=(
            pl.BlockSpec(blk, lambda b, r, i: (b, i, r)),
            pl.BlockSpec(blk, lambda b, r, i: (b, i, r)),
        ),
        compiler_params=_cparams(("arbitrary", "arbitrary", "arbitrary")),
        name=f"attn_prompt_w{win}",
    )(qv, kvv, kvv, kvv, kvv, bias)
    return o.reshape(bsz * seq, A_OUT), lse.reshape(bsz * seq, A_OUT)


def _hgrn_gates(q, z, lb):
    qs = q * _sigmoid(q)
    logf = jnp.log(lb + (1.0 - lb) * _sigmoid(z))
    kb = (1.0 - lb) * _sigmoid(-z)
    return qs, kb, logf


def _hgrn_out(o, gate, gn):
    on = o * lax.rsqrt(jnp.mean(o * o, axis=-1, keepdims=True) + NORM_EPS) * gn
    return (on * (gate * _sigmoid(gate))).astype(BF16)


def _split3(x):
    hi = x.astype(BF16)
    r1 = x - hi.astype(F32)
    mid = r1.astype(BF16)
    lo = (r1 - mid.astype(F32)).astype(BF16)
    return hi, mid, lo


HG_TC = 256
HG_NC = HG_TC // B_CHUNK
HG_HPS = 4


def _hgrn_prompt_kernel(q_ref, f_ref, i_ref, g_ref, lb_ref, gn_ref, ob_ref, st_ref, s_scr):
    tb = pl.program_id(2)

    @pl.when(tb == 0)
    def _():
        s_scr[...] = jnp.zeros_like(s_scr)

    row = lax.broadcasted_iota(jnp.int32, (HG_TC, HG_TC), 0)
    col = lax.broadcasted_iota(jnp.int32, (HG_TC, HG_TC), 1)
    same_chunk = (row // B_CHUNK) == (col // B_CHUNK)
    causal = jnp.logical_and(same_chunk, col <= row)
    cum_mat = jnp.where(causal, 1.0, 0.0).astype(BF16)
    tcol = lax.broadcasted_iota(jnp.int32, (HEAD, HG_TC), 1) // B_CHUNK
    gn = gn_ref[...]
    for h in range(HG_HPS):
        sl = slice(h * HEAD, (h + 1) * HEAD)
        qs, kb, logf = _hgrn_gates(q_ref[:, sl], f_ref[:, sl], lb_ref[:, sl])
        hi, mid, lo = _split3(logf)
        b = (jnp.dot(cum_mat, hi, preferred_element_type=F32)
             + jnp.dot(cum_mat, mid, preferred_element_type=F32)
             + jnp.dot(cum_mat, lo, preferred_element_type=F32))
        bl_rows = [b[c * B_CHUNK + B_CHUNK - 1:(c + 1) * B_CHUNK, :] for c in range(HG_NC)]
        bl = jnp.concatenate([jnp.broadcast_to(r, (B_CHUNK, HEAD)) for r in bl_rows], axis=0)
        q_dec = (qs * jnp.exp(b)).astype(BF16)
        k_inv = (kb * jnp.exp(-b)).astype(BF16)
        k_end_t = jnp.transpose(kb * jnp.exp(bl - b))
        v = i_ref[:, sl].astype(BF16)
        att = lax.dot_general(q_dec, k_inv, NT_DIMS, preferred_element_type=F32)
        att = jnp.where(causal, att, 0.0).astype(BF16)
        o_intra = jnp.dot(att, v, preferred_element_type=F32)
        decay_t = jnp.exp(jnp.transpose(bl))
        s = s_scr[h]
        inter = []
        for c in range(HG_NC):
            inter.append(jnp.dot(q_dec[c * B_CHUNK:(c + 1) * B_CHUNK, :], s.astype(BF16),
                                 preferred_element_type=F32))
            ke_c = jnp.where(tcol == c, k_end_t, 0.0).astype(BF16)
            ds = jnp.dot(ke_c, v, preferred_element_type=F32)
            s = decay_t[:, c * B_CHUNK:c * B_CHUNK + 1] * s + ds
        s_scr[h] = s
        o = o_intra + jnp.concatenate(inter, axis=0)
        ob_ref[:, sl] = _hgrn_out(o, g_ref[:, sl], gn)

    @pl.when(tb == pl.num_programs(2) - 1)
    def _():
        st_ref[...] = s_scr[...]


def _hgrn_prompt(proj, lb, g_norm, bsz, seq):
    m = proj.shape[0]
    hw = HG_HPS * HEAD
    nt = seq // HG_TC
    rowblk = lambda b, hf, t: b * nt + t
    spec = lambda off: pl.BlockSpec((HG_TC, hw), lambda b, hf, t: (rowblk(b, hf, t), off // hw + hf))
    return pl.pallas_call(
        _hgrn_prompt_kernel,
        out_shape=(jax.ShapeDtypeStruct((m, B_WIDTH), BF16),
                   jax.ShapeDtypeStruct((bsz, B_HEADS, HEAD, HEAD), F32)),
        grid=(bsz, B_HEADS // HG_HPS, nt),
        in_specs=[
            spec(OFF_QB), spec(OFF_FB), spec(OFF_IB), spec(OFF_OB),
            pl.BlockSpec((1, hw), lambda b, hf, t: (0, hf)),
            pl.BlockSpec((1, HEAD), lambda b, hf, t: (0, 0)),
        ],
        out_specs=(
            pl.BlockSpec((HG_TC, hw), lambda b, hf, t: (rowblk(b, hf, t), hf)),
            pl.BlockSpec((None, HG_HPS, HEAD, HEAD), lambda b, hf, t: (b, hf, 0, 0)),
        ),
        scratch_shapes=[pltpu.VMEM((HG_HPS, HEAD, HEAD), F32)],
        compiler_params=_cparams(("arbitrary", "arbitrary", "arbitrary")),
        name="hgrn_prompt",
    )(proj, proj, proj, proj, lb.reshape(1, B_WIDTH), g_norm.reshape(1, HEAD))


HS_SEQ = 32


def _hgrn_sample_kernel(q_ref, f_ref, i_ref, g_ref, lb_ref, gn_ref, s0_ref, ob_ref, st_ref, *, t_new):
    rows = HS_SEQ * t_new
    qs, kb, logf = _hgrn_gates(q_ref[...], f_ref[...], lb_ref[...])
    t_of = lax.broadcasted_iota(jnp.int32, (rows, HEAD), 0) % t_new
    b = logf
    sh = 1
    while sh < t_new:
        b = b + jnp.where(t_of >= sh, pltpu.roll(b, sh, axis=0), 0.0)
        sh *= 2
    bl = b
    for back in range(1, t_new):
        bl = jnp.where(t_of == t_new - 1 - back, pltpu.roll(b, rows - back, axis=0), bl)
    q_dec = (qs * jnp.exp(b)).astype(BF16)
    k_inv = (kb * jnp.exp(-b)).astype(BF16)
    k_end_t = jnp.transpose(kb * jnp.exp(bl - b))
    decay_t = jnp.exp(jnp.transpose(bl))
    v = i_ref[...].astype(BF16)
    row = lax.broadcasted_iota(jnp.int32, (rows, rows), 0)
    col = lax.broadcasted_iota(jnp.int32, (rows, rows), 1)
    causal = jnp.logical_and(row // t_new == col // t_new, col <= row)
    att = lax.dot_general(q_dec, k_inv, NT_DIMS, preferred_element_type=F32)
    att = jnp.where(causal, att, 0.0).astype(BF16)
    o_intra = jnp.dot(att, v, preferred_element_type=F32)
    seq_of_col = lax.broadcasted_iota(jnp.int32, (HEAD, rows), 1) // t_new
    seq_of_row = lax.broadcasted_iota(jnp.int32, (rows, HEAD), 0) // t_new
    o_inter = jnp.zeros((rows, HEAD), F32)
    for s_i in range(HS_SEQ):
        s = s0_ref[s_i, 0]
        part = jnp.dot(q_dec, s.astype(BF16), preferred_element_type=F32)
        o_inter = jnp.where(seq_of_row == s_i, part, o_inter)
        ke = jnp.where(seq_of_col == s_i, k_end_t, 0.0).astype(BF16)
        ds = jnp.dot(ke, v, preferred_element_type=F32)
        st_ref[s_i, 0] = decay_t[:, s_i * t_new:s_i * t_new + 1] * s + ds
    ob_ref[...] = _hgrn_out(o_intra + o_inter, g_ref[...], gn_ref[...])


def _hgrn_sample(proj, lb, g_norm, s0, t_new):
    m = proj.shape[0]
    nseq = m // t_new
    rows = HS_SEQ * t_new
    spec = lambda off: pl.BlockSpec((rows, HEAD), lambda i, h: (i, off // HEAD + h))
    st_spec = pl.BlockSpec((HS_SEQ, 1, HEAD, HEAD), lambda i, h: (i, h, 0, 0))
    return pl.pallas_call(
        functools.partial(_hgrn_sample_kernel, t_new=t_new),
        out_shape=(jax.ShapeDtypeStruct((m, B_WIDTH), BF16),
                   jax.ShapeDtypeStruct((nseq, B_HEADS, HEAD, HEAD), F32)),
        grid=(nseq // HS_SEQ, B_HEADS),
        in_specs=[
            spec(OFF_QB), spec(OFF_FB), spec(OFF_IB), spec(OFF_OB),
            pl.BlockSpec((1, HEAD), lambda i, h: (0, h)),
            pl.BlockSpec((1, HEAD), lambda i, h: (0, 0)),
            st_spec,
        ],
        out_specs=(pl.BlockSpec((rows, HEAD), lambda i, h: (i, h)), st_spec),
        compiler_params=_cparams(("arbitrary", "arbitrary")),
        name="hgrn_sample",
    )(proj, proj, proj, proj, lb.reshape(1, B_WIDTH), g_norm.reshape(1, HEAD), s0)


SA_SEQ = 4
SA_ROWS = A_KEYS + 16


def _sample_bias(rel_bias, t_new):
    nq = SA_SEQ * t_new
    r = np.arange(SA_ROWS)[:, None]
    c = np.arange(HEAD)[None, :]
    t_c = c % t_new
    seq_c = c // t_new
    is_new = r >= A_KEYS
    rn = np.maximum(r - A_KEYS, 0)
    tabs = []
    for s in range(SA_SEQ):
        dist = np.where(is_new, t_c - rn % t_new, A_KEYS + t_c - r)
        valid = np.where(is_new, (rn // t_new == s) & (rn % t_new <= t_c) & (rn < nq), dist < A_KEYS)
        valid = valid & (seq_c == s) & (c < nq)
        tabs.append((np.clip(dist, 0, A_KEYS - 1), valid))
    t0 = []
    tab = rel_bias[:, 0:A_HPG].astype(F32)
    for dist, valid in tabs:
        bias = jnp.transpose(tab[_t5_bucket(dist)], (2, 0, 1))
        t0.append(jnp.where(valid[None], bias, NEG_INF))
    t0 = jnp.stack(t0)
    t12 = []
    for g in (1, 2):
        _, dil = A_GROUPS[g]
        dist = np.where(is_new, 0, dil * (A_KEYS - r)) + 0 * c
        valid = np.where(is_new, rn == c, (r >= 1) & (c < nq))
        tab = rel_bias[:, g * A_HPG:(g + 1) * A_HPG].astype(F32)
        bias = jnp.transpose(tab[_t5_bucket(np.clip(dist, 0, REL_MAX_DIST))], (2, 0, 1))
        t12.append(jnp.where(valid[None], bias, NEG_INF))
    return t0, jnp.stack(t12)


def _attn_sample_kernel(q_ref, n0_ref, n1_ref, n2_ref, c0_ref, c1_ref, c2_ref, t0_ref, t12_ref,
                        o_ref, *, t_new):
    nq = SA_SEQ * t_new
    new_refs = (n0_ref, n1_ref, n2_ref)
    pad_q = jnp.zeros((HEAD - nq, HEAD), F32)
    kvw = 2 * A_OUT

    def column_attend(st, c, v_ext):
        col = st[:, c:c + 1]
        mx = jnp.max(col, axis=0, keepdims=True)
        p = jnp.exp(col - mx)
        den = jnp.sum(p, axis=0, keepdims=True)
        out = jnp.sum(p * v_ext, axis=0, keepdims=True) / den
        return out, mx + jnp.log(den)

    for h in range(A_HPG):
        sl = slice(h * HEAD, (h + 1) * HEAD)
        vsl = slice(A_OUT + h * HEAD, A_OUT + (h + 1) * HEAD)
        qpad = [jnp.concatenate([q_ref[:, g * A_OUT + h * HEAD:g * A_OUT + (h + 1) * HEAD], pad_q],
                                axis=0).astype(BF16) for g in range(3)]
        knew = [new_refs[g][:, sl] for g in range(3)]
        vnew = [new_refs[g][:, vsl] for g in range(3)]
        for s in range(SA_SEQ):
            outs = [[None] * t_new for _ in range(3)]
            lses = [[None] * t_new for _ in range(3)]
            k_ext = jnp.concatenate([c0_ref[s, :, sl], knew[0]], axis=0).astype(BF16)
            v_ext = jnp.concatenate([c0_ref[s, :, vsl], vnew[0]], axis=0)
            st = lax.dot_general(k_ext, qpad[0], NT_DIMS, preferred_element_type=F32) * ATTN_SCALE
            st = st + t0_ref[s, h]
            for t in range(t_new):
                outs[0][t], lses[0][t] = column_attend(st, s * t_new + t, v_ext)
            for g, c_ref in ((1, c1_ref), (2, c2_ref)):
                for t in range(t_new):
                    base = t * kvw
                    k_ext = jnp.concatenate(
                        [c_ref[s, :, base + h * HEAD:base + (h + 1) * HEAD], knew[g]], axis=0).astype(BF16)
                    v_ext = jnp.concatenate(
                        [c_ref[s, :, base + A_OUT + h * HEAD:base + A_OUT + (h + 1) * HEAD], vnew[g]], axis=0)
                    st = lax.dot_general(k_ext, qpad[g], NT_DIMS, preferred_element_type=F32) * ATTN_SCALE
                    st = st + t12_ref[g - 1, h]
                    outs[g][t], lses[g][t] = column_attend(st, s * t_new + t, v_ext)
            for t in range(t_new):
                mx = jnp.maximum(jnp.maximum(lses[0][t], lses[1][t]), lses[2][t])
                e = [jnp.exp(lses[g][t] - mx) for g in range(3)]
                tot = e[0] + e[1] + e[2]
                mix = (e[0] * outs[0][t] + e[1] * outs[1][t] + e[2] * outs[2][t]) / tot
                o_ref[s * t_new + t:s * t_new + t + 1, sl] = mix


def _attn_sample(qn, kv_new, caches, rel_bias, t_new):
    m = qn.shape[0]
    nseq = m // t_new
    nq = SA_SEQ * t_new
    assert nq == SA_ROWS - A_KEYS, "new-token rows of one grid step must fill one packed sublane tile"
    kvw = 2 * A_OUT
    views = []
    for g, (win, dil) in enumerate(A_GROUPS):
        c = caches[g]
        assert c.shape[1] == win, "cache must hold exactly one window of positions"
        views.append(c.reshape(nseq, A_KEYS, dil * kvw))
    t0, t12 = _sample_bias(rel_bias, t_new)
    cache_spec = lambda width: pl.BlockSpec((SA_SEQ, A_KEYS, width), lambda i: (i, 0, 0))
    new_spec = pl.BlockSpec((nq, kvw), lambda i: (i, 0))
    return pl.pallas_call(
        functools.partial(_attn_sample_kernel, t_new=t_new),
        out_shape=jax.ShapeDtypeStruct((m, A_OUT), F32),
        grid=(nseq // SA_SEQ,),
        in_specs=[
            pl.BlockSpec((nq, A_WIDTH), lambda i: (i, 0)),
            new_spec, new_spec, new_spec,
            cache_spec(kvw), cache_spec(t_new * kvw), cache_spec(t_new * kvw),
            pl.BlockSpec(t0.shape, lambda i: (0, 0, 0, 0)),
            pl.BlockSpec(t12.shape, lambda i: (0, 0, 0, 0)),
        ],
        out_specs=pl.BlockSpec((nq, A_OUT), lambda i: (i, 0)),
        compiler_params=_cparams(("arbitrary",)),
        name="attn_sample",
    )(qn, kv_new[0], kv_new[1], kv_new[2], views[0], views[1], views[2], t0, t12)


MIX_TM = 512


def _mix_prompt_kernel(o0, o1, o2, l0, l1, l2, ob_ref, ga_ref, gb_ref, wpa_ref, wpb_ref, out_ref):
    la, lb_, lc = l0[...], l1[...], l2[...]
    mx = jnp.maximum(jnp.maximum(la, lb_), lc)
    ea, eb, ec = jnp.exp(la - mx), jnp.exp(lb_ - mx), jnp.exp(lc - mx)
    oa = (ea * o0[...] + eb * o1[...] + ec * o2[...]) / (ea + eb + ec)
    _mix_tail(oa, ob_ref, ga_ref, gb_ref, wpa_ref, wpb_ref, out_ref)


def _mix_sample_kernel(oa_ref, ob_ref, ga_ref, gb_ref, wpa_ref, wpb_ref, out_ref):
    _mix_tail(oa_ref[...], ob_ref, ga_ref, gb_ref, wpa_ref, wpb_ref, out_ref)


def _mix_tail(oa, ob_ref, ga_ref, gb_ref, wpa_ref, wpb_ref, out_ref):
    pa = jnp.dot(oa.astype(BF16), wpa_ref[...], preferred_element_type=F32)
    pb = jnp.dot(ob_ref[...], wpb_ref[...], preferred_element_type=F32)
    out_ref[...] = (_sigmoid(ga_ref[...]) * pa + _sigmoid(gb_ref[...]) * pb).astype(BF16)


MIX_TN = 512


def _mix(attn_parts, ob, proj, w_pa, w_pb):
    m = proj.shape[0]
    row512 = pl.BlockSpec((MIX_TM, A_OUT), lambda i, j: (i, 0))
    common_specs = [
        pl.BlockSpec((MIX_TM, B_WIDTH), lambda i, j: (i, 0)),
        pl.BlockSpec((MIX_TM, MIX_TN), lambda i, j: (i, OFF_GA // MIX_TN + j)),
        pl.BlockSpec((MIX_TM, MIX_TN), lambda i, j: (i, OFF_GB // MIX_TN + j)),
        pl.BlockSpec((A_OUT, MIX_TN), lambda i, j: (0, j)),
        pl.BlockSpec((B_WIDTH, MIX_TN), lambda i, j: (0, j)),
    ]
    if len(attn_parts) == 1:
        body, name = _mix_sample_kernel, "mix_sample"
    else:
        body, name = _mix_prompt_kernel, "mix_prompt"
    return pl.pallas_call(
        body,
        out_shape=jax.ShapeDtypeStruct((m, D_MODEL), BF16),
        grid=(m // MIX_TM, D_MODEL // MIX_TN),
        in_specs=[row512] * len(attn_parts) + common_specs,
        out_specs=pl.BlockSpec((MIX_TM, MIX_TN), lambda i, j: (i, j)),
        compiler_params=_cparams(("arbitrary", "arbitrary")),
        name=name,
    )(*attn_parts, ob, proj, proj, w_pa, w_pb)


RT_LANES = 128


def _resid_kernel(x_ref, mixed_ref, wo_ref, nw_ref, wr_hi_ref, wr_lo_ref, br_ref, h_ref, hn_ref, lg_ref):
    h = x_ref[...] + jnp.dot(mixed_ref[...], wo_ref[...], preferred_element_type=F32)
    h_ref[...] = h
    hn = h * lax.rsqrt(jnp.mean(h * h, axis=-1, keepdims=True) + NORM_EPS) * nw_ref[...]
    hi = hn.astype(BF16)
    hn_ref[...] = hi
    lo = (hn - hi.astype(F32)).astype(BF16)
    lg = (jnp.dot(hi, wr_hi_ref[...], preferred_element_type=F32)
          + jnp.dot(lo, wr_hi_ref[...], preferred_element_type=F32)
          + jnp.dot(hi, wr_lo_ref[...], preferred_element_type=F32))
    lg_ref[...] = lg + br_ref[...]


def _resid(x, mixed, w_o, ffn_norm, wr_hi, wr_lo, br):
    m = x.shape[0]
    full = lambda shape: pl.BlockSpec(shape, lambda i: (0, 0))
    return pl.pallas_call(
        _resid_kernel,
        out_shape=(jax.ShapeDtypeStruct((m, D_MODEL), F32),
                   jax.ShapeDtypeStruct((m, D_MODEL), BF16),
                   jax.ShapeDtypeStruct((m, RT_LANES), F32)),
        grid=(m // MIX_TM,),
        in_specs=[
            pl.BlockSpec((MIX_TM, D_MODEL), lambda i: (i, 0)),
            pl.BlockSpec((MIX_TM, D_MODEL), lambda i: (i, 0)),
            full((D_MODEL, D_MODEL)), full((1, D_MODEL)),
            full((D_MODEL, RT_LANES)), full((D_MODEL, RT_LANES)), full((1, RT_LANES)),
        ],
        out_specs=(
            pl.BlockSpec((MIX_TM, D_MODEL), lambda i: (i, 0)),
            pl.BlockSpec((MIX_TM, D_MODEL), lambda i: (i, 0)),
            pl.BlockSpec((MIX_TM, RT_LANES), lambda i: (i, 0)),
        ),
        compiler_params=_cparams(("arbitrary",)),
        name="resid_router",
    )(x, mixed, w_o, ffn_norm.reshape(1, D_MODEL), wr_hi, wr_lo, br)


MOE_RB = 1024
MOE_SUB = 256
MOE_TF = 256
MOE_NF = D_FF // MOE_TF


def _moe_kernel(blk_e_ref, blk_nsub_ref, n_used_ref, xs_ref, gate_ref, wg_ref, wl_ref, bg_ref, bl_ref,
                wd_ref, bd_ref, y_ref, wg_s, wl_s, wd_s):
    i = pl.program_id(0)
    f = pl.program_id(1)

    @pl.when(i < n_used_ref[0])
    def _():
        wg_s[...] = wg_ref[...].astype(BF16)
        wl_s[...] = wl_ref[...].astype(BF16)
        wd_s[...] = wd_ref[...].astype(BF16)
        bg = bg_ref[...]
        bl = bl_ref[...]

        def sub(sb, carry):
            rows = pl.ds(pl.multiple_of(sb * MOE_SUB, MOE_SUB), MOE_SUB)
            x = xs_ref[rows, :]
            hg = jnp.dot(x, wg_s[...], preferred_element_type=F32) + bg
            hl = jnp.dot(x, wl_s[...], preferred_element_type=F32) + bl
            glu = jnp.minimum(hg, SWIGLU_LIMIT)
            lin = jnp.clip(hl, -SWIGLU_LIMIT, SWIGLU_LIMIT)
            act = glu * _sigmoid(SWIGLU_ALPHA * glu) * (lin + 1.0)
            part = jnp.dot(act.astype(BF16), wd_s[...], preferred_element_type=F32)

            @pl.when(f == 0)
            def _():
                y_ref[rows, :] = part

            @pl.when(jnp.logical_and(f > 0, f < MOE_NF - 1))
            def _():
                y_ref[rows, :] += part

            @pl.when(f == MOE_NF - 1)
            def _():
                y_ref[rows, :] = (y_ref[rows, :] + part + bd_ref[...]) * gate_ref[rows, :]

            return carry

        lax.fori_loop(0, blk_nsub_ref[i], sub, 0)


def _moe_experts(blk_e, blk_nsub, n_used, xs, gate_col, w_up, b_up, w_down, b_down):
    n_blk = xs.shape[0] // MOE_RB

    def live(i, n_used_ref):
        return jnp.minimum(i, n_used_ref[0] - 1)

    def ff(i, f, n_used_ref):
        return jnp.where(i < n_used_ref[0], f, MOE_NF - 1)

    def expert(i, be, nu):
        return be[live(i, nu)]

    grid_spec = pltpu.PrefetchScalarGridSpec(
        num_scalar_prefetch=3,
        grid=(n_blk, MOE_NF),
        in_specs=[
            pl.BlockSpec((MOE_RB, D_MODEL), lambda i, f, be, ns, nu: (live(i, nu), 0)),
            pl.BlockSpec((MOE_RB, 1), lambda i, f, be, ns, nu: (live(i, nu), 0)),
            pl.BlockSpec((None, D_MODEL, MOE_TF), lambda i, f, be, ns, nu: (expert(i, be, nu), 0, ff(i, f, nu))),
            pl.BlockSpec((None, D_MODEL, MOE_TF),
                         lambda i, f, be, ns, nu: (expert(i, be, nu), 0, ff(i, f, nu) + MOE_NF)),
            pl.BlockSpec((None, 1, MOE_TF), lambda i, f, be, ns, nu: (expert(i, be, nu), 0, ff(i, f, nu))),
            pl.BlockSpec((None, 1, MOE_TF),
                         lambda i, f, be, ns, nu: (expert(i, be, nu), 0, ff(i, f, nu) + MOE_NF)),
            pl.BlockSpec((None, MOE_TF, D_MODEL), lambda i, f, be, ns, nu: (expert(i, be, nu), ff(i, f, nu), 0)),
            pl.BlockSpec((None, 1, D_MODEL), lambda i, f, be, ns, nu: (expert(i, be, nu), 0, 0)),
        ],
        out_specs=pl.BlockSpec((MOE_RB, D_MODEL), lambda i, f, be, ns, nu: (live(i, nu), 0)),
        scratch_shapes=[
            pltpu.VMEM((D_MODEL, MOE_TF), BF16),
            pltpu.VMEM((D_MODEL, MOE_TF), BF16),
            pltpu.VMEM((MOE_TF, D_MODEL), BF16),
        ],
    )
    b_up3 = b_up.reshape(N_EXPERTS, 1, 2 * D_FF)
    return pl.pallas_call(
        _moe_kernel,
        out_shape=jax.ShapeDtypeStruct(xs.shape, F32),
        grid_spec=grid_spec,
        compiler_params=_cparams(("arbitrary", "arbitrary")),
        name="moe_experts",
    )(blk_e, blk_nsub, n_used, xs, gate_col, w_up, w_up, b_up3, b_up3, w_down,
      b_down.reshape(N_EXPERTS, 1, D_MODEL))


def _route(logits):
    n_tok = logits.shape[0]
    top_val, top_idx = lax.top_k(logits, TOP_K)
    gate = jax.nn.softmax(top_val, axis=-1)
    flat_e = top_idx.reshape(-1)
    onehot = (flat_e[:, None] == jnp.arange(N_EXPERTS)[None, :]).astype(jnp.int32)
    incl = jnp.cumsum(onehot, axis=0)
    rank = jnp.sum((incl - onehot) * onehot, axis=1)
    counts = incl[-1]
    padded = (counts + MOE_SUB - 1) // MOE_SUB * MOE_SUB
    padded_rb = (counts + MOE_RB - 1) // MOE_RB * MOE_RB
    blk_end = jnp.cumsum(padded_rb) // MOE_RB
    row_start = jnp.cumsum(padded_rb) - padded_rb
    dest = row_start[flat_e] + rank
    n_blk = -(-(n_tok * TOP_K) // MOE_RB) + N_EXPERTS
    cap = n_blk * MOE_RB
    blk_ids = jnp.arange(n_blk)
    blk_e = jnp.minimum(jnp.searchsorted(blk_end, blk_ids, side='right'), N_EXPERTS - 1).astype(jnp.int32)
    rows_left = padded[blk_e] - (blk_ids * MOE_RB - row_start[blk_e])
    blk_nsub = jnp.clip(rows_left // MOE_SUB, 0, MOE_RB // MOE_SUB).astype(jnp.int32)
    n_used = blk_end[-1:].astype(jnp.int32)
    tok = (jnp.arange(n_tok * TOP_K) // TOP_K).astype(jnp.int32)
    buf_tok = jnp.full((cap,), n_tok, jnp.int32).at[dest].set(tok)
    buf_gate = jnp.zeros((cap,), F32).at[dest].set(gate.reshape(-1))
    return blk_e, blk_nsub, n_used, buf_tok, buf_gate, dest.reshape(n_tok, TOP_K)


CMB_TM = 256


def _combine_kernel(h_ref, yg_ref, y_ref):
    acc = h_ref[...]
    for k in range(TOP_K):
        acc = acc + yg_ref[:, k * D_MODEL:(k + 1) * D_MODEL]
    y_ref[...] = acc


def _combine(h, yg, row0):
    m = h.shape[0]
    blk0 = row0 // CMB_TM
    return pl.pallas_call(
        _combine_kernel,
        out_shape=jax.ShapeDtypeStruct((m, D_MODEL), F32),
        grid=(m // CMB_TM,),
        in_specs=[
            pl.BlockSpec((CMB_TM, D_MODEL), lambda i: (i, 0)),
            pl.BlockSpec((CMB_TM, TOP_K * D_MODEL), lambda i: (i + blk0, 0)),
        ],
        out_specs=pl.BlockSpec((CMB_TM, D_MODEL), lambda i: (i, 0)),
        compiler_params=_cparams(("arbitrary",)),
        name="combine",
    )(h, yg)


def kernel(x_prompt, x_sample, cache_kv_w128, cache_kv_w512, cache_kv_w2048, state_hgrn, rel_bias, attn_norm,
           w_in, q_norm, k_norm, lb_raw, g_norm, w_pa, w_pb, w_o, ffn_norm, w_router, b_router, w_up, b_up,
           w_down, b_down):
    assert attn_norm.shape[0] == 1, "single-layer stack"
    bsz, seq, _ = x_prompt.shape
    nseq, t_new, _ = x_sample.shape
    n_p, n_s = bsz * seq, nseq * t_new
    lb = jax.nn.softmax(lb_raw.astype(F32), axis=0)[0]
    wpa, wpb, wo = w_pa[0].astype(BF16), w_pb[0].astype(BF16), w_o[0].astype(BF16)
    wr = jnp.pad(w_router[0].astype(F32), ((0, 0), (0, RT_LANES - N_EXPERTS)))
    wr_hi = wr.astype(BF16)
    wr_lo = (wr - wr_hi.astype(F32)).astype(BF16)
    br = jnp.pad(b_router[0].astype(F32), (0, RT_LANES - N_EXPERTS)).reshape(1, RT_LANES)
    caches = (cache_kv_w128[0], cache_kv_w512[0], cache_kv_w2048[0])

    xp = x_prompt.reshape(n_p, D_MODEL)
    xs = x_sample.reshape(n_s, D_MODEL)

    proj_p = _inproj(xp, attn_norm[0], w_in[0])
    qn_p, *kv_p = _qkv(proj_p, q_norm[0], k_norm[0])
    parts, lses = [], []
    for g in range(3):
        o, lse = _attn_prompt(qn_p, kv_p[g], _prompt_bias(rel_bias, g), g, bsz, seq)
        parts.append(o)
        lses.append(lse)
    ob_p, st_p = _hgrn_prompt(proj_p, lb, g_norm[0], bsz, seq)
    mixed_p = _mix(parts + lses, ob_p, proj_p, wpa, wpb)
    h_p, hn_p, lg_p = _resid(xp, mixed_p, wo, ffn_norm[0], wr_hi, wr_lo, br)

    proj_s = _inproj(xs, attn_norm[0], w_in[0])
    qn_s, *kv_s = _qkv(proj_s, q_norm[0], k_norm[0])
    oa_s = _attn_sample(qn_s, kv_s, caches, rel_bias, t_new)
    ob_s, st_s = _hgrn_sample(proj_s, lb, g_norm[0], state_hgrn[0], t_new)
    mixed_s = _mix([oa_s], ob_s, proj_s, wpa, wpb)
    h_s, hn_s, lg_s = _resid(xs, mixed_s, wo, ffn_norm[0], wr_hi, wr_lo, br)

    logits = jnp.concatenate([lg_p, lg_s], axis=0)[:, :N_EXPERTS]
    blk_e, blk_nsub, n_used, buf_tok, buf_gate, dest = _route(logits)
    hn_all = jnp.concatenate([hn_p, hn_s, jnp.zeros((1, D_MODEL), BF16)], axis=0)
    xs_sorted = hn_all[buf_tok]
    yb = _moe_experts(blk_e, blk_nsub, n_used, xs_sorted, buf_gate.reshape(-1, 1),
                      w_up[0], b_up[0], w_down[0], b_down[0])
    yg = yb[dest].reshape(n_p + n_s, TOP_K * D_MODEL)
    y_p = _combine(h_p, yg, 0)
    y_s = _combine(h_s, yg, n_p)

    def prompt_rows(kv, win):
        rows = kv.reshape(bsz, seq, 2, A_HPG, HEAD)[:, seq - min(win, seq):]
        return rows[None]

    kvp = [prompt_rows(kv_p[g], A_GROUPS[g][0]) for g in range(3)]
    kvs = [kv_s[g].reshape(1, nseq, t_new, 2, A_HPG, HEAD) for g in range(3)]
    return (y_p.reshape(bsz, seq, D_MODEL), y_s.reshape(nseq, t_new, D_MODEL),
            kvp[0], kvp[1], kvp[2], st_p[None],
            kvs[0], kvs[1], kvs[2], st_s[None])
```

```python
import functools

import numpy as np
import jax
import jax.numpy as jnp
from jax import lax
from jax.experimental import pallas as pl
from jax.experimental.pallas import tpu as pltpu

F32 = jnp.float32
BF16 = jnp.bfloat16

D_MODEL = 2048
HEAD = 128
A_GROUPS = ((128, 1), (512, 4), (2048, 16))
A_HPG = 4
A_KEYS = 128
A_WIDTH = 3 * A_HPG * HEAD
A_OUT = A_HPG * HEAD
B_HEADS = 8
B_WIDTH = B_HEADS * HEAD
B_CHUNK = 32
REL_BUCKETS = 32
REL_MAX_DIST = 2048
N_EXPERTS = 32
TOP_K = 4
D_FF = 2048
SWIGLU_ALPHA = 1.702
SWIGLU_LIMIT = 7.0
NORM_EPS = 1e-6
NEG_INF = -1e30
ATTN_SCALE = HEAD ** -0.5
IN_WIDTH = 3 * A_WIDTH + 4 * B_WIDTH + 2 * D_MODEL
OFF_Q, OFF_K, OFF_V = 0, A_WIDTH, 2 * A_WIDTH
OFF_QB = 3 * A_WIDTH
OFF_FB, OFF_IB, OFF_OB = OFF_QB + B_WIDTH, OFF_QB + 2 * B_WIDTH, OFF_QB + 3 * B_WIDTH
OFF_GA = OFF_QB + 4 * B_WIDTH
OFF_GB = OFF_GA + D_MODEL

V7X_VMEM_LIMIT = 56 * 1024 * 1024
SUBLANES = 8

NT_DIMS = (((1,), (1,)), ((), ()))


def _sigmoid(x):
    return 1.0 / (1.0 + jnp.exp(-x))


def _cparams(sem):
    return pltpu.CompilerParams(dimension_semantics=sem, vmem_limit_bytes=V7X_VMEM_LIMIT)


PROJ_TM = 512
PROJ_TN = 1280


def _inproj_kernel(x_ref, nw_ref, w_ref, o_ref, wbf_ref):
    @pl.when(pl.program_id(1) == 0)
    def _():
        wbf_ref[...] = w_ref[...].astype(BF16)

    x = x_ref[...]
    ms = jnp.mean(x * x, axis=-1, keepdims=True)
    xn = (x * lax.rsqrt(ms + NORM_EPS) * nw_ref[...]).astype(BF16)
    o_ref[...] = jnp.dot(xn, wbf_ref[...], preferred_element_type=F32)


def _inproj(x, norm_w, w_in):
    m = x.shape[0]
    return pl.pallas_call(
        _inproj_kernel,
        out_shape=jax.ShapeDtypeStruct((m, IN_WIDTH), F32),
        grid=(IN_WIDTH // PROJ_TN, m // PROJ_TM),
        in_specs=[
            pl.BlockSpec((PROJ_TM, D_MODEL), lambda j, i: (i, 0)),
            pl.BlockSpec((1, D_MODEL), lambda j, i: (0, 0)),
            pl.BlockSpec((D_MODEL, PROJ_TN), lambda j, i: (0, j)),
        ],
        out_specs=pl.BlockSpec((PROJ_TM, PROJ_TN), lambda j, i: (i, j)),
        scratch_shapes=[pltpu.VMEM((D_MODEL, PROJ_TN), BF16)],
        compiler_params=_cparams(("arbitrary", "arbitrary")),
        name="inproj",
    )(x, norm_w.reshape(1, D_MODEL), w_in)


QKV_TM = 512


def _qkv_kernel(q_ref, k_ref, v_ref, qw_ref, kw_ref, qn_ref, kv0_ref, kv1_ref, kv2_ref):
    kv_refs = (kv0_ref, kv1_ref, kv2_ref)
    qw = qw_ref[...]
    kw = kw_ref[...]
    for h in range(3 * A_HPG):
        sl = slice(h * HEAD, (h + 1) * HEAD)
        q = q_ref[:, sl]
        qn_ref[:, sl] = q * lax.rsqrt(jnp.mean(q * q, axis=-1, keepdims=True) + NORM_EPS) * qw
        k = k_ref[:, sl]
        kn = k * lax.rsqrt(jnp.mean(k * k, axis=-1, keepdims=True) + NORM_EPS) * kw
        g, hh = divmod(h, A_HPG)
        kv_refs[g][:, hh * HEAD:(hh + 1) * HEAD] = kn
        kv_refs[g][:, A_OUT + hh * HEAD:A_OUT + (hh + 1) * HEAD] = v_ref[:, sl]


def _qkv(proj, q_norm, k_norm):
    m = proj.shape[0]
    kv_shape = jax.ShapeDtypeStruct((m, 2 * A_OUT), F32)
    return pl.pallas_call(
        _qkv_kernel,
        out_shape=(jax.ShapeDtypeStruct((m, A_WIDTH), F32), kv_shape, kv_shape, kv_shape),
        grid=(m // QKV_TM,),
        in_specs=[
            pl.BlockSpec((QKV_TM, A_WIDTH), lambda i: (i, OFF_Q // A_WIDTH)),
            pl.BlockSpec((QKV_TM, A_WIDTH), lambda i: (i, OFF_K // A_WIDTH)),
            pl.BlockSpec((QKV_TM, A_WIDTH), lambda i: (i, OFF_V // A_WIDTH)),
            pl.BlockSpec((1, HEAD), lambda i: (0, 0)),
            pl.BlockSpec((1, HEAD), lambda i: (0, 0)),
        ],
        out_specs=(
            pl.BlockSpec((QKV_TM, A_WIDTH), lambda i: (i, 0)),
            pl.BlockSpec((QKV_TM, 2 * A_OUT), lambda i: (i, 0)),
            pl.BlockSpec((QKV_TM, 2 * A_OUT), lambda i: (i, 0)),
            pl.BlockSpec((QKV_TM, 2 * A_OUT), lambda i: (i, 0)),
        ),
        compiler_params=_cparams(("arbitrary",)),
        name="qkv_norm",
    )(proj, proj, proj, q_norm.reshape(1, HEAD), k_norm.reshape(1, HEAD))


def _t5_bucket(dist):
    dist = np.asarray(dist, np.int64)
    max_exact = REL_BUCKETS // 2
    ratio = np.log(np.maximum(dist, 1) / max_exact) / np.log(REL_MAX_DIST / max_exact)
    large = np.minimum(max_exact + (ratio * (REL_BUCKETS - max_exact)).astype(np.int64), REL_BUCKETS - 1)
    return np.where(dist < max_exact, dist, large).astype(np.int32)


def _bias_lookup(tab, bucket, valid):
    bucket = jnp.asarray(np.where(valid, bucket, -1).astype(np.int32))[None]
    out = jnp.full((tab.shape[1],) + bucket.shape[1:], NEG_INF, F32)
    for b in range(REL_BUCKETS):
        out = jnp.where(bucket == b, tab[b].reshape((-1,) + (1,) * (bucket.ndim - 1)), out)
    return out


def _prompt_bias(rel_bias, g):
    _, dil = A_GROUPS[g]
    n = A_KEYS
    j = n + np.arange(n)[:, None] - np.arange(2 * n)[None, :]
    valid = (j >= 0) & (j < n)
    tab = rel_bias[:, g * A_HPG:(g + 1) * A_HPG].astype(F32)
    return _bias_lookup(tab, _t5_bucket(np.clip(j, 0, n - 1) * dil), valid)


def _attn_prompt_kernel(q_ref, kp_ref, ko_ref, vp_ref, vo_ref, bias_ref, o_ref, l_ref):
    n = A_KEYS
    first_block = pl.program_id(2) == 0
    kc = lax.broadcasted_iota(jnp.int32, (n, 2 * n), 1)
    keep = jnp.logical_or(kc >= n, jnp.logical_not(first_block))
    for h in range(A_HPG):
        sl = slice(h * HEAD, (h + 1) * HEAD)
        q = q_ref[:, sl].astype(BF16)
        k2 = jnp.concatenate([kp_ref[:, sl], ko_ref[:, sl]], axis=0).astype(BF16)
        s = lax.dot_general(q, k2, NT_DIMS, preferred_element_type=F32) * ATTN_SCALE + bias_ref[h]
        s = jnp.where(keep, s, NEG_INF)
        mx = jnp.max(s, axis=-1, keepdims=True)
        p = jnp.exp(s - mx)
        den = jnp.sum(p, axis=-1, keepdims=True)
        v2 = jnp.concatenate([vp_ref[:, sl], vo_ref[:, sl]], axis=0).astype(BF16)
        o_ref[:, sl] = jnp.dot(p.astype(BF16), v2, preferred_element_type=F32) / den
        l_ref[:, sl] = jnp.broadcast_to(mx + jnp.log(den), (n, HEAD))


def _attn_prompt(qn, kv, bias, g, bsz, seq):
    win, dil = A_GROUPS[g]
    n = A_KEYS
    m = seq // dil
    nb = m // n
    qv = qn.reshape(bsz, m, dil * A_WIDTH)
    kvv = kv.reshape(bsz, m, dil * 2 * A_OUT)
    qpb = A_WIDTH // A_OUT
    prev = lambda i: jnp.maximum(i - 1, 0)
    blk = (None, n, A_OUT)
    out_sds = jax.ShapeDtypeStruct((bsz, m, dil * A_OUT), F32)
    o, lse = pl.pallas_call(
        _attn_prompt_kernel,
        out_shape=(out_sds, out_sds),
        grid=(bsz, dil, nb),
        in_specs=[
            pl.BlockSpec(blk, lambda b, r, i: (b, i, r * qpb + g)),
            pl.BlockSpec(blk, lambda b, r, i: (b, prev(i), 2 * r)),
            pl.BlockSpec(blk, lambda b, r, i: (b, i, 2 * r)),
            pl.BlockSpec(blk, lambda b, r, i: (b, prev(i), 2 * r + 1)),
            pl.BlockSpec(blk, lambda b, r, i: (b, i, 2 * r + 1)),
            pl.BlockSpec((A_HPG, n, 2 * n), lambda b, r, i: (0, 0, 0)),
        ],
        out_specs=(
            pl.BlockSpec(blk, lambda b, r, i: (b, i, r)),
            pl.BlockSpec(blk, lambda b, r, i: (b, i, r)),
        ),
        compiler_params=_cparams(("arbitrary", "arbitrary", "arbitrary")),
        name=f"attn_prompt_w{win}",
    )(qv, kvv, kvv, kvv, kvv, bias)
    return o.reshape(bsz * seq, A_OUT), lse.reshape(bsz * seq, A_OUT)


def _hgrn_gates(q, z, lb):
    qs = q * _sigmoid(q)
    logf = jnp.log(lb + (1.0 - lb) * _sigmoid(z))
    kb = (1.0 - lb) * _sigmoid(-z)
    return qs, kb, logf


def _hgrn_out(o, gate, gn):
    on = o * lax.rsqrt(jnp.mean(o * o, axis=-1, keepdims=True) + NORM_EPS) * gn
    return (on * (gate * _sigmoid(gate))).astype(BF16)


def _split3(x):
    hi = x.astype(BF16)
    r1 = x - hi.astype(F32)
    mid = r1.astype(BF16)
    lo = (r1 - mid.astype(F32)).astype(BF16)
    return hi, mid, lo


HG_TC = 256
HG_NC = HG_TC // B_CHUNK
HG_HPS = 4


def _hgrn_prompt_kernel(q_ref, f_ref, i_ref, g_ref, lb_ref, gn_ref, ob_ref, st_ref, s_scr):
    tb = pl.program_id(2)

    @pl.when(tb == 0)
    def _():
        s_scr[...] = jnp.zeros_like(s_scr)

    row = lax.broadcasted_iota(jnp.int32, (HG_TC, HG_TC), 0)
    col = lax.broadcasted_iota(jnp.int32, (HG_TC, HG_TC), 1)
    same_chunk = (row // B_CHUNK) == (col // B_CHUNK)
    causal = jnp.logical_and(same_chunk, col <= row)
    cum_mat = jnp.where(causal, 1.0, 0.0).astype(BF16)
    tcol = lax.broadcasted_iota(jnp.int32, (HEAD, HG_TC), 1) // B_CHUNK
    gn = gn_ref[...]
    for h in range(HG_HPS):
        sl = slice(h * HEAD, (h + 1) * HEAD)
        qs, kb, logf = _hgrn_gates(q_ref[:, sl], f_ref[:, sl], lb_ref[:, sl])
        hi, mid, lo = _split3(logf)
        b = (jnp.dot(cum_mat, hi, preferred_element_type=F32)
             + jnp.dot(cum_mat, mid, preferred_element_type=F32)
             + jnp.dot(cum_mat, lo, preferred_element_type=F32))
        bl_rows = [b[c * B_CHUNK + B_CHUNK - 1:(c + 1) * B_CHUNK, :] for c in range(HG_NC)]
        bl = jnp.concatenate([jnp.broadcast_to(r, (B_CHUNK, HEAD)) for r in bl_rows], axis=0)
        q_dec = (qs * jnp.exp(b)).astype(BF16)
        k_inv = (kb * jnp.exp(-b)).astype(BF16)
        k_end_t = jnp.transpose(kb * jnp.exp(bl - b))
        v = i_ref[:, sl].astype(BF16)
        att = lax.dot_general(q_dec, k_inv, NT_DIMS, preferred_element_type=F32)
        att = jnp.where(causal, att, 0.0).astype(BF16)
        o_intra = jnp.dot(att, v, preferred_element_type=F32)
        decay_t = jnp.exp(jnp.transpose(bl))
        s = s_scr[h]
        inter = []
        for c in range(HG_NC):
            inter.append(jnp.dot(q_dec[c * B_CHUNK:(c + 1) * B_CHUNK, :], s.astype(BF16),
                                 preferred_element_type=F32))
            ke_c = jnp.where(tcol == c, k_end_t, 0.0).astype(BF16)
            ds = jnp.dot(ke_c, v, preferred_element_type=F32)
            s = decay_t[:, c * B_CHUNK:c * B_CHUNK + 1] * s + ds
        s_scr[h] = s
        o = o_intra + jnp.concatenate(inter, axis=0)
        ob_ref[:, sl] = _hgrn_out(o, g_ref[:, sl], gn)

    @pl.when(tb == pl.num_programs(2) - 1)
    def _():
        st_ref[...] = s_scr[...]


def _hgrn_prompt(proj, lb, g_norm, bsz, seq):
    m = proj.shape[0]
    hw = HG_HPS * HEAD
    nt = seq // HG_TC
    rowblk = lambda b, hf, t: b * nt + t
    spec = lambda off: pl.BlockSpec((HG_TC, hw), lambda b, hf, t: (rowblk(b, hf, t), off // hw + hf))
    return pl.pallas_call(
        _hgrn_prompt_kernel,
        out_shape=(jax.ShapeDtypeStruct((m, B_WIDTH), BF16),
                   jax.ShapeDtypeStruct((bsz, B_HEADS, HEAD, HEAD), F32)),
        grid=(bsz, B_HEADS // HG_HPS, nt),
        in_specs=[
            spec(OFF_QB), spec(OFF_FB), spec(OFF_IB), spec(OFF_OB),
            pl.BlockSpec((1, hw), lambda b, hf, t: (0, hf)),
            pl.BlockSpec((1, HEAD), lambda b, hf, t: (0, 0)),
        ],
        out_specs=(
            pl.BlockSpec((HG_TC, hw), lambda b, hf, t: (rowblk(b, hf, t), hf)),
            pl.BlockSpec((None, HG_HPS, HEAD, HEAD), lambda b, hf, t: (b, hf, 0, 0)),
        ),
        scratch_shapes=[pltpu.VMEM((HG_HPS, HEAD, HEAD), F32)],
        compiler_params=_cparams(("arbitrary", "arbitrary", "arbitrary")),
        name="hgrn_prompt",
    )(proj, proj, proj, proj, lb.reshape(1, B_WIDTH), g_norm.reshape(1, HEAD))


HS_SEQ = 32


def _hgrn_sample_kernel(q_ref, f_ref, i_ref, g_ref, lb_ref, gn_ref, s0_ref, ob_ref, st_ref, *, t_new):
    rows = HS_SEQ * t_new
    qs, kb, logf = _hgrn_gates(q_ref[...], f_ref[...], lb_ref[...])
    t_of = lax.broadcasted_iota(jnp.int32, (rows, HEAD), 0) % t_new
    b = logf
    sh = 1
    while sh < t_new:
        b = b + jnp.where(t_of >= sh, pltpu.roll(b, sh, axis=0), 0.0)
        sh *= 2
    bl = b
    for back in range(1, t_new):
        bl = jnp.where(t_of == t_new - 1 - back, pltpu.roll(b, rows - back, axis=0), bl)
    q_dec = (qs * jnp.exp(b)).astype(BF16)
    k_inv = (kb * jnp.exp(-b)).astype(BF16)
    k_end_t = jnp.transpose(kb * jnp.exp(bl - b))
    decay_t = jnp.exp(jnp.transpose(bl))
    v = i_ref[...].astype(BF16)
    row = lax.broadcasted_iota(jnp.int32, (rows, rows), 0)
    col = lax.broadcasted_iota(jnp.int32, (rows, rows), 1)
    causal = jnp.logical_and(row // t_new == col // t_new, col <= row)
    att = lax.dot_general(q_dec, k_inv, NT_DIMS, preferred_element_type=F32)
    att = jnp.where(causal, att, 0.0).astype(BF16)
    o_intra = jnp.dot(att, v, preferred_element_type=F32)
    seq_of_col = lax.broadcasted_iota(jnp.int32, (HEAD, rows), 1) // t_new
    seq_of_row = lax.broadcasted_iota(jnp.int32, (rows, HEAD), 0) // t_new
    o_inter = jnp.zeros((rows, HEAD), F32)
    for s_i in range(HS_SEQ):
        s = s0_ref[s_i, 0]
        part = jnp.dot(q_dec, s.astype(BF16), preferred_element_type=F32)
        o_inter = jnp.where(seq_of_row == s_i, part, o_inter)
        ke = jnp.where(seq_of_col == s_i, k_end_t, 0.0).astype(BF16)
        ds = jnp.dot(ke, v, preferred_element_type=F32)
        st_ref[s_i, 0] = decay_t[:, s_i * t_new:s_i * t_new + 1] * s + ds
    ob_ref[...] = _hgrn_out(o_intra + o_inter, g_ref[...], gn_ref[...])


def _hgrn_sample(proj, lb, g_norm, s0, t_new):
    m = proj.shape[0]
    nseq = m // t_new
    rows = HS_SEQ * t_new
    spec = lambda off: pl.BlockSpec((rows, HEAD), lambda i, h: (i, off // HEAD + h))
    st_spec = pl.BlockSpec((HS_SEQ, 1, HEAD, HEAD), lambda i, h: (i, h, 0, 0))
    return pl.pallas_call(
        functools.partial(_hgrn_sample_kernel, t_new=t_new),
        out_shape=(jax.ShapeDtypeStruct((m, B_WIDTH), BF16),
                   jax.ShapeDtypeStruct((nseq, B_HEADS, HEAD, HEAD), F32)),
        grid=(nseq // HS_SEQ, B_HEADS),
        in_specs=[
            spec(OFF_QB), spec(OFF_FB), spec(OFF_IB), spec(OFF_OB),
            pl.BlockSpec((1, HEAD), lambda i, h: (0, h)),
            pl.BlockSpec((1, HEAD), lambda i, h: (0, 0)),
            st_spec,
        ],
        out_specs=(pl.BlockSpec((rows, HEAD), lambda i, h: (i, h)), st_spec),
        compiler_params=_cparams(("arbitrary", "arbitrary")),
        name="hgrn_sample",
    )(proj, proj, proj, proj, lb.reshape(1, B_WIDTH), g_norm.reshape(1, HEAD), s0)


SA_SEQ = 2
KV_ROWS = 2 * A_HPG


def _sample_bias(rel_bias, t_new):
    def tile(tab, dist, valid):
        bias = _bias_lookup(tab, _t5_bucket(np.clip(dist, 0, REL_MAX_DIST)), valid)
        bias = jnp.moveaxis(bias, 0, -1)
        bias = jnp.concatenate([bias, jnp.zeros(bias.shape[:-1] + (KV_ROWS - A_HPG,), F32)], axis=-1)
        return jnp.broadcast_to(bias[..., None], bias.shape + (HEAD,))

    t = np.arange(t_new)[:, None]
    r = np.arange(A_KEYS + t_new)[None, :]
    is_new = r >= A_KEYS
    dist = np.where(is_new, t - (r - A_KEYS), A_KEYS + t - r)
    valid = (dist >= 0) & (dist < A_KEYS)
    b0 = tile(rel_bias[:, 0:A_HPG].astype(F32), dist, valid)
    b12 = []
    for g in (1, 2):
        _, dil = A_GROUPS[g]
        m = np.arange(A_KEYS + 1)
        dist = dil * (A_KEYS - m)
        b12.append(tile(rel_bias[:, g * A_HPG:(g + 1) * A_HPG].astype(F32), dist, m >= 1))
    return b0, jnp.stack(b12)


def _attn_sample_kernel(q_ref, n0_ref, n1_ref, n2_ref, c0_ref, c1_ref, c2_ref, b0_ref, b12_ref,
                        o_ref, qt_scr, ns_scr, l_scr, *, t_new):
    nq = SA_SEQ * t_new
    new_refs = (n0_ref, n1_ref, n2_ref)
    cache_refs = (c0_ref, c1_ref, c2_ref)
    qt_scr[...] = jnp.zeros_like(qt_scr)
    for g in range(3):
        for j in range(nq):
            for h in range(A_HPG):
                qt_scr[g, j, h:h + 1, :] = (
                    q_ref[j:j + 1, g * A_OUT + h * HEAD:g * A_OUT + (h + 1) * HEAD] * ATTN_SCALE)
                ns_scr[g, j, h:h + 1, :] = new_refs[g][j:j + 1, h * HEAD:(h + 1) * HEAD]
                ns_scr[g, j, A_HPG + h:A_HPG + h + 1, :] = (
                    new_refs[g][j:j + 1, A_OUT + h * HEAD:A_OUT + (h + 1) * HEAD])

    row_id = lax.broadcasted_iota(jnp.int32, (nq, HEAD), 0)

    def per_query(j, out):
        s = j // t_new
        t = j - s * t_new
        row0 = s * A_KEYS
        stats = []
        for g in range(3):
            c_ref = cache_refs[g]
            qt = qt_scr[g, j]
            if g == 0:
                sub0 = 0
                bias_at = lambda m: b0_ref[t, m]
                new_tiles = [(A_KEYS + tp, ns_scr[0, s * t_new + tp]) for tp in range(t_new)]
            else:
                sub0 = pl.multiple_of(t * KV_ROWS, KV_ROWS)
                bias_at = lambda m, g=g: b12_ref[g - 1, m]
                new_tiles = [(A_KEYS, ns_scr[g, j])]

            def score(tile_kv, m):
                return jnp.sum(tile_kv * qt, axis=-1, keepdims=True) + bias_at(m)

            def pass1(m, mx):
                l = score(c_ref[row0 + m, pl.ds(sub0, KV_ROWS), :], m)
                l_scr[m] = l
                return jnp.maximum(mx, l)

            mx = lax.fori_loop(0, A_KEYS, pass1, jnp.full((KV_ROWS, HEAD), NEG_INF, F32), unroll=8)
            new_l = [score(tile_kv, m) for m, tile_kv in new_tiles]
            for l in new_l:
                mx = jnp.maximum(mx, l)

            def pass2(m, carry):
                den, acc = carry
                p = jnp.exp(l_scr[m] - mx)
                tile_kv = c_ref[row0 + m, pl.ds(sub0, KV_ROWS), :]
                return den + p, acc + pltpu.roll(p, A_HPG, axis=0) * tile_kv

            zero = jnp.zeros((KV_ROWS, HEAD), F32)
            den, acc = lax.fori_loop(0, A_KEYS, pass2, (zero, zero), unroll=8)
            for l, (_, tile_kv) in zip(new_l, new_tiles):
                p = jnp.exp(l - mx)
                den = den + p
                acc = acc + pltpu.roll(p, A_HPG, axis=0) * tile_kv
            stats.append((mx + jnp.log(den), den, acc))
        top = jnp.maximum(jnp.maximum(stats[0][0], stats[1][0]), stats[2][0])
        e = [jnp.exp(st[0] - top) for st in stats]
        tot = e[0] + e[1] + e[2]
        mix = jnp.zeros((KV_ROWS, HEAD), F32)
        for g in range(3):
            mix = mix + pltpu.roll(e[g] / (tot * stats[g][1]), A_HPG, axis=0) * stats[g][2]
        pieces = [jnp.broadcast_to(mix[A_HPG + h:A_HPG + h + 1, :], (nq, HEAD)) for h in range(A_HPG)]
        return tuple(jnp.where(row_id == j, pieces[h], out[h]) for h in range(A_HPG))

    init = tuple(jnp.zeros((nq, HEAD), F32) for _ in range(A_HPG))
    out = lax.fori_loop(0, nq, per_query, init)
    for h in range(A_HPG):
        o_ref[:, h * HEAD:(h + 1) * HEAD] = out[h]


def _attn_sample(qn, kv_new, caches, rel_bias, t_new):
    m = qn.shape[0]
    nseq = m // t_new
    nq = SA_SEQ * t_new
    assert nq % SUBLANES == 0, "the new-token rows of one grid step must fill whole sublane tiles"
    kvw = 2 * A_OUT
    views, specs = [], []
    for g, (win, dil) in enumerate(A_GROUPS):
        c = caches[g]
        assert c.shape[1] == win, "cache must hold exactly one window of positions"
        assert g == 0 or dil >= t_new, "each new token must sit in its own residue class"
        views.append(c.reshape(nseq * A_KEYS, dil * KV_ROWS, HEAD))
        specs.append(pl.BlockSpec((SA_SEQ * A_KEYS, min(dil, t_new) * KV_ROWS, HEAD), lambda i: (i, 0, 0)))
    b0, b12 = _sample_bias(rel_bias, t_new)
    new_spec = pl.BlockSpec((nq, kvw), lambda i: (i, 0))
    return pl.pallas_call(
        functools.partial(_attn_sample_kernel, t_new=t_new),
        out_shape=jax.ShapeDtypeStruct((m, A_OUT), F32),
        grid=(nseq // SA_SEQ,),
        in_specs=[
            pl.BlockSpec((nq, A_WIDTH), lambda i: (i, 0)),
            new_spec, new_spec, new_spec,
            specs[0], specs[1], specs[2],
            pl.BlockSpec(b0.shape, lambda i: (0, 0, 0, 0)),
            pl.BlockSpec(b12.shape, lambda i: (0, 0, 0, 0)),
        ],
        out_specs=pl.BlockSpec((nq, A_OUT), lambda i: (i, 0)),
        scratch_shapes=[
            pltpu.VMEM((3, nq, KV_ROWS, HEAD), F32),
            pltpu.VMEM((3, nq, KV_ROWS, HEAD), F32),
            pltpu.VMEM((A_KEYS, KV_ROWS, HEAD), F32),
        ],
        compiler_params=_cparams(("arbitrary",)),
        name="attn_sample",
    )(qn, kv_new[0], kv_new[1], kv_new[2], views[0], views[1], views[2], b0, b12)


MIX_TM = 512
MIX_TN = 512


def _mix_prompt_kernel(o0, o1, o2, l0, l1, l2, ob_ref, ga_ref, gb_ref, wpa_ref, wpb_ref, out_ref):
    la, lb_, lc = l0[...], l1[...], l2[...]
    mx = jnp.maximum(jnp.maximum(la, lb_), lc)
    ea, eb, ec = jnp.exp(la - mx), jnp.exp(lb_ - mx), jnp.exp(lc - mx)
    oa = (ea * o0[...] + eb * o1[...] + ec * o2[...]) / (ea + eb + ec)
    _mix_tail(oa, ob_ref, ga_ref, gb_ref, wpa_ref, wpb_ref, out_ref)


def _mix_sample_kernel(oa_ref, ob_ref, ga_ref, gb_ref, wpa_ref, wpb_ref, out_ref):
    _mix_tail(oa_ref[...], ob_ref, ga_ref, gb_ref, wpa_ref, wpb_ref, out_ref)


def _mix_tail(oa, ob_ref, ga_ref, gb_ref, wpa_ref, wpb_ref, out_ref):
    pa = jnp.dot(oa.astype(BF16), wpa_ref[...], preferred_element_type=F32)
    pb = jnp.dot(ob_ref[...], wpb_ref[...], preferred_element_type=F32)
    out_ref[...] = (_sigmoid(ga_ref[...]) * pa + _sigmoid(gb_ref[...]) * pb).astype(BF16)


def _mix(attn_parts, ob, proj, w_pa, w_pb):
    m = proj.shape[0]
    row512 = pl.BlockSpec((MIX_TM, A_OUT), lambda i, j: (i, 0))
    common_specs = [
        pl.BlockSpec((MIX_TM, B_WIDTH), lambda i, j: (i, 0)),
        pl.BlockSpec((MIX_TM, MIX_TN), lambda i, j: (i, OFF_GA // MIX_TN + j)),
        pl.BlockSpec((MIX_TM, MIX_TN), lambda i, j: (i, OFF_GB // MIX_TN + j)),
        pl.BlockSpec((A_OUT, MIX_TN), lambda i, j: (0, j)),
        pl.BlockSpec((B_WIDTH, MIX_TN), lambda i, j: (0, j)),
    ]
    if len(attn_parts) == 1:
        body, name = _mix_sample_kernel, "mix_sample"
    else:
        body, name = _mix_prompt_kernel, "mix_prompt"
    return pl.pallas_call(
        body,
        out_shape=jax.ShapeDtypeStruct((m, D_MODEL), BF16),
        grid=(m // MIX_TM, D_MODEL // MIX_TN),
        in_specs=[row512] * len(attn_parts) + common_specs,
        out_specs=pl.BlockSpec((MIX_TM, MIX_TN), lambda i, j: (i, j)),
        compiler_params=_cparams(("arbitrary", "arbitrary")),
        name=name,
    )(*attn_parts, ob, proj, proj, w_pa, w_pb)


RT_LANES = 128


def _resid_kernel(x_ref, mixed_ref, wo_ref, nw_ref, wr_hi_ref, wr_lo_ref, br_ref, h_ref, hn_ref, lg_ref):
    h = x_ref[...] + jnp.dot(mixed_ref[...], wo_ref[...], preferred_element_type=F32)
    h_ref[...] = h
    hn = h * lax.rsqrt(jnp.mean(h * h, axis=-1, keepdims=True) + NORM_EPS) * nw_ref[...]
    hi = hn.astype(BF16)
    hn_ref[...] = hi
    lo = (hn - hi.astype(F32)).astype(BF16)
    lg = (jnp.dot(hi, wr_hi_ref[...], preferred_element_type=F32)
          + jnp.dot(lo, wr_hi_ref[...], preferred_element_type=F32)
          + jnp.dot(hi, wr_lo_ref[...], preferred_element_type=F32))
    lg_ref[...] = lg + br_ref[...]


def _resid(x, mixed, w_o, ffn_norm, wr_hi, wr_lo, br):
    m = x.shape[0]
    full = lambda shape: pl.BlockSpec(shape, lambda i: (0, 0))
    return pl.pallas_call(
        _resid_kernel,
        out_shape=(jax.ShapeDtypeStruct((m, D_MODEL), F32),
                   jax.ShapeDtypeStruct((m, D_MODEL), BF16),
                   jax.ShapeDtypeStruct((m, RT_LANES), F32)),
        grid=(m // MIX_TM,),
        in_specs=[
            pl.BlockSpec((MIX_TM, D_MODEL), lambda i: (i, 0)),
            pl.BlockSpec((MIX_TM, D_MODEL), lambda i: (i, 0)),
            full((D_MODEL, D_MODEL)), full((1, D_MODEL)),
            full((D_MODEL, RT_LANES)), full((D_MODEL, RT_LANES)), full((1, RT_LANES)),
        ],
        out_specs=(
            pl.BlockSpec((MIX_TM, D_MODEL), lambda i: (i, 0)),
            pl.BlockSpec((MIX_TM, D_MODEL), lambda i: (i, 0)),
            pl.BlockSpec((MIX_TM, RT_LANES), lambda i: (i, 0)),
        ),
        compiler_params=_cparams(("arbitrary",)),
        name="resid_router",
    )(x, mixed, w_o, ffn_norm.reshape(1, D_MODEL), wr_hi, wr_lo, br)


MOE_RB = 1280
MOE_SUB = 256
MOE_TF = 256
MOE_NF = D_FF // MOE_TF


MOE_TM = 512
MOE_GRAN = 16
MOE_SLOTS = 2560
MOE_CHUNK = 256
assert MOE_SLOTS >= MOE_TM * TOP_K + N_EXPERTS * (MOE_GRAN - 1) and MOE_SLOTS % MOE_CHUNK == 0


def _route(logits):
    n_tok = logits.shape[0]
    nt = n_tok // MOE_TM
    top_val, top_idx = lax.top_k(logits, TOP_K)
    gate = jax.nn.softmax(top_val, axis=-1)
    e3 = top_idx.reshape(nt, MOE_TM * TOP_K)
    onehot = (e3[..., None] == jnp.arange(N_EXPERTS)).astype(jnp.int32)
    incl = jnp.cumsum(onehot, axis=1)
    rank = jnp.sum((incl - onehot) * onehot, axis=2)
    seg = (incl[:, -1] + MOE_GRAN - 1) // MOE_GRAN * MOE_GRAN
    loc_off = jnp.cumsum(seg, axis=1) - seg
    rows_e = jnp.sum(seg, axis=0)
    padded_sub = (rows_e + MOE_SUB - 1) // MOE_SUB * MOE_SUB
    padded_rb = (rows_e + MOE_RB - 1) // MOE_RB * MOE_RB
    blk_end = jnp.cumsum(padded_rb) // MOE_RB
    row_start = jnp.cumsum(padded_rb) - padded_rb
    seg_start = row_start[None, :] + jnp.cumsum(seg, axis=0) - seg
    slot = jnp.take_along_axis(loc_off, e3, axis=1) + rank
    dest = jnp.take_along_axis(seg_start, e3, axis=1) + rank
    max_rows = n_tok * TOP_K + nt * N_EXPERTS * (MOE_GRAN - 1)
    n_blk = max_rows // MOE_RB + N_EXPERTS
    blk_ids = jnp.arange(n_blk)
    blk_e = jnp.minimum(jnp.searchsorted(blk_end, blk_ids, side='right'), N_EXPERTS - 1).astype(jnp.int32)
    rows_left = padded_sub[blk_e] - (blk_ids * MOE_RB - row_start[blk_e])
    blk_nsub = jnp.clip(rows_left // MOE_SUB, 0, MOE_RB // MOE_SUB).astype(jnp.int32)
    n_used = blk_end[-1:].astype(jnp.int32)
    tail_row = jnp.where(rows_e > 0, row_start + padded_sub - MOE_SUB, -1).astype(jnp.int32)
    gate_rows = jnp.zeros((n_blk * MOE_RB,), F32).at[dest.reshape(-1)].set(gate.reshape(-1))
    layout = dict(seg_start=seg_start.reshape(-1).astype(jnp.int32),
                  seg_gran=(seg // MOE_GRAN).reshape(-1).astype(jnp.int32),
                  loc_off=loc_off.reshape(-1).astype(jnp.int32),
                  tail_row=tail_row)
    slot = slot.reshape(nt, MOE_TM, TOP_K).astype(jnp.int32)
    return blk_e, blk_nsub, n_used, layout, slot, gate_rows, n_blk


def _segment_copies(layout_refs, tile, src_of, dst_of, sem, start):
    seg_start_ref, seg_gran_ref, loc_off_ref = layout_refs

    for e in range(N_EXPERTS):
        idx = tile * N_EXPERTS + e
        loc = loc_off_ref[idx]
        glob = seg_start_ref[idx]

        def body(c, carry, loc=loc, glob=glob):
            copy = pltpu.make_async_copy(
                src_of(pl.multiple_of(loc + c * MOE_GRAN, MOE_GRAN), pl.multiple_of(glob + c * MOE_GRAN, MOE_GRAN)),
                dst_of(pl.multiple_of(loc + c * MOE_GRAN, MOE_GRAN), pl.multiple_of(glob + c * MOE_GRAN, MOE_GRAN)),
                sem)
            if start:
                copy.start()
            else:
                copy.wait()
            return carry

        lax.fori_loop(0, seg_gran_ref[idx], body, 0)


def _slot_onehot(slot_vecs, base, shape, axis):
    ids = base + lax.broadcasted_iota(jnp.int32, shape, axis)
    hit = jnp.zeros(shape, F32)
    for v in slot_vecs:
        hit = jnp.where(ids == v, 1.0, hit)
    return hit.astype(BF16)


def _dispatch_kernel(seg_start_ref, seg_gran_ref, loc_off_ref, tail_ref, slot_ref, x_ref, xs_ref,
                     sorted_scr, zero_scr, sem):
    i = pl.program_id(0)

    @pl.when(i == 0)
    def _():
        zero_scr[...] = jnp.zeros_like(zero_scr)

        def tail_copy(e):
            row = pl.multiple_of(jnp.maximum(tail_ref[e], 0), MOE_SUB)
            return pltpu.make_async_copy(zero_scr, xs_ref.at[pl.ds(row, MOE_SUB)], sem)

        for e in range(N_EXPERTS):
            @pl.when(tail_ref[e] >= 0)
            def _():
                tail_copy(e).start()
        for e in range(N_EXPERTS):
            @pl.when(tail_ref[e] >= 0)
            def _():
                tail_copy(e).wait()

    x = x_ref[...]
    slot_vecs = [slot_ref[k:k + 1, :] for k in range(TOP_K)]
    for c in range(MOE_SLOTS // MOE_CHUNK):
        perm = _slot_onehot(slot_vecs, c * MOE_CHUNK, (MOE_CHUNK, MOE_TM), 0)
        sorted_scr[c * MOE_CHUNK:(c + 1) * MOE_CHUNK, :] = jnp.dot(
            perm, x, preferred_element_type=F32).astype(BF16)

    layout_refs = (seg_start_ref, seg_gran_ref, loc_off_ref)
    src_of = lambda loc, glob: sorted_scr.at[pl.ds(loc, MOE_GRAN)]
    dst_of = lambda loc, glob: xs_ref.at[pl.ds(glob, MOE_GRAN)]
    _segment_copies(layout_refs, i, src_of, dst_of, sem, start=True)
    _segment_copies(layout_refs, i, src_of, dst_of, sem, start=False)


def _dispatch(layout, slot, hn, n_rows):
    n_tok = hn.shape[0]
    slot_t = jnp.swapaxes(slot, 1, 2)
    grid_spec = pltpu.PrefetchScalarGridSpec(
        num_scalar_prefetch=4,
        grid=(n_tok // MOE_TM,),
        in_specs=[
            pl.BlockSpec((None, TOP_K, MOE_TM), lambda i, *_: (i, 0, 0)),
            pl.BlockSpec((MOE_TM, D_MODEL), lambda i, *_: (i, 0)),
        ],
        out_specs=pl.BlockSpec(memory_space=pl.ANY),
        scratch_shapes=[
            pltpu.VMEM((MOE_SLOTS, D_MODEL), BF16),
            pltpu.VMEM((MOE_SUB, D_MODEL), BF16),
            pltpu.SemaphoreType.DMA,
        ],
    )
    return pl.pallas_call(
        _dispatch_kernel,
        out_shape=jax.ShapeDtypeStruct((n_rows, D_MODEL), BF16),
        grid_spec=grid_spec,
        compiler_params=_cparams(("arbitrary",)),
        name="moe_dispatch",
    )(layout["seg_start"], layout["seg_gran"], layout["loc_off"], layout["tail_row"], slot_t, hn)


def _moe_kernel(blk_e_ref, blk_nsub_ref, n_used_ref, xs_ref, gate_ref, wg_ref, wl_ref, bg_ref, bl_ref,
                wd_ref, bd_ref, y_ref, wg_s, wl_s, wd_s):
    i = pl.program_id(0)
    f = pl.program_id(1)

    @pl.when(i < n_used_ref[0])
    def _():
        wg_s[...] = wg_ref[...].astype(BF16)
        wl_s[...] = wl_ref[...].astype(BF16)
        wd_s[...] = wd_ref[...].astype(BF16)
        bg = bg_ref[...]
        bl = bl_ref[...]
        nsub = blk_nsub_ref[i]

        def ffn(start, size):
            rows = pl.ds(pl.multiple_of(start, MOE_SUB), size)
            x = xs_ref[rows, :]
            hg = jnp.dot(x, wg_s[...], preferred_element_type=F32) + bg
            hl = jnp.dot(x, wl_s[...], preferred_element_type=F32) + bl
            glu = jnp.minimum(hg, SWIGLU_LIMIT)
            lin = jnp.clip(hl, -SWIGLU_LIMIT, SWIGLU_LIMIT)
            act = glu * _sigmoid(SWIGLU_ALPHA * glu) * (lin + 1.0)
            part = jnp.dot(act.astype(BF16), wd_s[...], preferred_element_type=F32)

            @pl.when(f == 0)
            def _():
                y_ref[rows, :] = part + bd_ref[...]

            @pl.when(jnp.logical_and(f > 0, f < MOE_NF - 1))
            def _():
                y_ref[rows, :] += part

            @pl.when(f == MOE_NF - 1)
            def _():
                y_ref[rows, :] = (y_ref[rows, :] + part) * gate_ref[rows, :]

        def pair(sb, c):
            ffn(sb * (2 * MOE_SUB), 2 * MOE_SUB)
            return c

        lax.fori_loop(0, nsub // 2, pair, 0)

        @pl.when(nsub % 2 == 1)
        def _():
            ffn((nsub - 1) * MOE_SUB, MOE_SUB)


def _moe_experts(blk_e, blk_nsub, n_used, xs, gate_rows, w_up, b_up, w_down, b_down):
    n_blk = xs.shape[0] // MOE_RB

    def live(i, n_used_ref):
        return jnp.minimum(i, n_used_ref[0] - 1)

    def ff(i, f, n_used_ref):
        return jnp.where(i < n_used_ref[0], f, MOE_NF - 1)

    def expert(i, be, nu):
        return be[live(i, nu)]

    grid_spec = pltpu.PrefetchScalarGridSpec(
        num_scalar_prefetch=3,
        grid=(n_blk, MOE_NF),
        in_specs=[
            pl.BlockSpec((MOE_RB, D_MODEL), lambda i, f, be, ns, nu: (live(i, nu), 0)),
            pl.BlockSpec((MOE_RB, 1), lambda i, f, be, ns, nu: (live(i, nu), 0)),
            pl.BlockSpec((None, D_MODEL, MOE_TF), lambda i, f, be, ns, nu: (expert(i, be, nu), 0, ff(i, f, nu))),
            pl.BlockSpec((None, D_MODEL, MOE_TF),
                         lambda i, f, be, ns, nu: (expert(i, be, nu), 0, ff(i, f, nu) + MOE_NF)),
            pl.BlockSpec((None, 1, MOE_TF), lambda i, f, be, ns, nu: (expert(i, be, nu), 0, ff(i, f, nu))),
            pl.BlockSpec((None, 1, MOE_TF),
                         lambda i, f, be, ns, nu: (expert(i, be, nu), 0, ff(i, f, nu) + MOE_NF)),
            pl.BlockSpec((None, MOE_TF, D_MODEL), lambda i, f, be, ns, nu: (expert(i, be, nu), ff(i, f, nu), 0)),
            pl.BlockSpec((None, 1, D_MODEL), lambda i, f, be, ns, nu: (expert(i, be, nu), 0, 0)),
        ],
        out_specs=pl.BlockSpec((MOE_RB, D_MODEL), lambda i, f, be, ns, nu: (live(i, nu), 0)),
        scratch_shapes=[
            pltpu.VMEM((D_MODEL, MOE_TF), BF16),
            pltpu.VMEM((D_MODEL, MOE_TF), BF16),
            pltpu.VMEM((MOE_TF, D_MODEL), BF16),
        ],
    )
    b_up3 = b_up.reshape(N_EXPERTS, 1, 2 * D_FF)
    return pl.pallas_call(
        _moe_kernel,
        out_shape=jax.ShapeDtypeStruct(xs.shape, F32),
        grid_spec=grid_spec,
        compiler_params=_cparams(("arbitrary", "arbitrary")),
        name="moe_experts",
    )(blk_e, blk_nsub, n_used, xs, gate_rows.reshape(-1, 1), w_up, w_up, b_up3, b_up3, w_down,
      b_down.reshape(N_EXPERTS, 1, D_MODEL))


def _combine_kernel(seg_start_ref, seg_gran_ref, loc_off_ref, slot_ref, h_ref, yb_ref, y_ref, rows_scr, sem, *,
                    tile0):
    i = pl.program_id(0)

    @pl.when(i == 0)
    def _():
        rows_scr[...] = jnp.zeros_like(rows_scr)

    layout_refs = (seg_start_ref, seg_gran_ref, loc_off_ref)
    src_of = lambda loc, glob: yb_ref.at[pl.ds(glob, MOE_GRAN)]
    dst_of = lambda loc, glob: rows_scr.at[pl.ds(loc, MOE_GRAN)]
    _segment_copies(layout_refs, tile0 + i, src_of, dst_of, sem, start=True)
    _segment_copies(layout_refs, tile0 + i, src_of, dst_of, sem, start=False)

    slot_vecs = [slot_ref[:, k:k + 1] for k in range(TOP_K)]
    acc = h_ref[...]
    for c in range(MOE_SLOTS // MOE_CHUNK):
        pick = _slot_onehot(slot_vecs, c * MOE_CHUNK, (MOE_TM, MOE_CHUNK), 1)
        rows = rows_scr[c * MOE_CHUNK:(c + 1) * MOE_CHUNK, :]
        hi = rows.astype(BF16)
        lo = (rows - hi.astype(F32)).astype(BF16)
        acc = acc + (jnp.dot(pick, hi, preferred_element_type=F32)
                     + jnp.dot(pick, lo, preferred_element_type=F32))
    y_ref[...] = acc


def _combine(layout, slot, h, yb, tile0):
    m = h.shape[0]
    grid_spec = pltpu.PrefetchScalarGridSpec(
        num_scalar_prefetch=3,
        grid=(m // MOE_TM,),
        in_specs=[
            pl.BlockSpec((None, MOE_TM, TOP_K), lambda i, *_: (i + tile0, 0, 0)),
            pl.BlockSpec((MOE_TM, D_MODEL), lambda i, *_: (i, 0)),
            pl.BlockSpec(memory_space=pl.ANY),
        ],
        out_specs=pl.BlockSpec((MOE_TM, D_MODEL), lambda i, *_: (i, 0)),
        scratch_shapes=[pltpu.VMEM((MOE_SLOTS, D_MODEL), F32), pltpu.SemaphoreType.DMA],
    )
    return pl.pallas_call(
        functools.partial(_combine_kernel, tile0=tile0),
        out_shape=jax.ShapeDtypeStruct((m, D_MODEL), F32),
        grid_spec=grid_spec,
        compiler_params=_cparams(("arbitrary",)),
        name="moe_combine",
    )(layout["seg_start"], layout["seg_gran"], layout["loc_off"], slot, h, yb)


def kernel(x_prompt, x_sample, cache_kv_w128, cache_kv_w512, cache_kv_w2048, state_hgrn, rel_bias, attn_norm,
           w_in, q_norm, k_norm, lb_raw, g_norm, w_pa, w_pb, w_o, ffn_norm, w_router, b_router, w_up, b_up,
           w_down, b_down):
    assert attn_norm.shape[0] == 1, "single-layer stack"
    bsz, seq, _ = x_prompt.shape
    nseq, t_new, _ = x_sample.shape
    n_p, n_s = bsz * seq, nseq * t_new
    lb = jax.nn.softmax(lb_raw.astype(F32), axis=0)[0]
    wpa, wpb, wo = w_pa[0].astype(BF16), w_pb[0].astype(BF16), w_o[0].astype(BF16)
    wr = jnp.pad(w_router[0].astype(F32), ((0, 0), (0, RT_LANES - N_EXPERTS)))
    wr_hi = wr.astype(BF16)
    wr_lo = (wr - wr_hi.astype(F32)).astype(BF16)
    br = jnp.pad(b_router[0].astype(F32), (0, RT_LANES - N_EXPERTS)).reshape(1, RT_LANES)
    caches = (cache_kv_w128[0], cache_kv_w512[0], cache_kv_w2048[0])

    xp = x_prompt.reshape(n_p, D_MODEL)
    xs = x_sample.reshape(n_s, D_MODEL)

    proj_p = _inproj(xp, attn_norm[0], w_in[0])
    qn_p, *kv_p = _qkv(proj_p, q_norm[0], k_norm[0])
    parts, lses = [], []
    for g in range(3):
        o, lse = _attn_prompt(qn_p, kv_p[g], _prompt_bias(rel_bias, g), g, bsz, seq)
        parts.append(o)
        lses.append(lse)
    ob_p, st_p = _hgrn_prompt(proj_p, lb, g_norm[0], bsz, seq)
    mixed_p = _mix(parts + lses, ob_p, proj_p, wpa, wpb)
    h_p, hn_p, lg_p = _resid(xp, mixed_p, wo, ffn_norm[0], wr_hi, wr_lo, br)

    proj_s = _inproj(xs, attn_norm[0], w_in[0])
    qn_s, *kv_s = _qkv(proj_s, q_norm[0], k_norm[0])
    oa_s = _attn_sample(qn_s, kv_s, caches, rel_bias, t_new)
    ob_s, st_s = _hgrn_sample(proj_s, lb, g_norm[0], state_hgrn[0], t_new)
    mixed_s = _mix([oa_s], ob_s, proj_s, wpa, wpb)
    h_s, hn_s, lg_s = _resid(xs, mixed_s, wo, ffn_norm[0], wr_hi, wr_lo, br)

    logits = jnp.concatenate([lg_p, lg_s], axis=0)[:, :N_EXPERTS]
    assert n_p % MOE_TM == 0 and n_s % MOE_TM == 0, "each pass must fill whole MoE token tiles"
    blk_e, blk_nsub, n_used, layout, slot, gate_rows, n_blk = _route(logits)
    hn_all = jnp.concatenate([hn_p, hn_s], axis=0)
    xs_sorted = _dispatch(layout, slot, hn_all, n_blk * MOE_RB)
    yb = _moe_experts(blk_e, blk_nsub, n_used, xs_sorted, gate_rows, w_up[0], b_up[0], w_down[0], b_down[0])
    y_p = _combine(layout, slot, h_p, yb, 0)
    y_s = _combine(layout, slot, h_s, yb, n_p // MOE_TM)

    def prompt_rows(kv, win):
        rows = kv.reshape(bsz, seq, 2, A_HPG, HEAD)[:, seq - min(win, seq):]
        return rows[None]

    kvp = [prompt_rows(kv_p[g], A_GROUPS[g][0]) for g in range(3)]
    kvs = [kv_s[g].reshape(1, nseq, t_new, 2, A_HPG, HEAD) for g in range(3)]
    return (y_p.reshape(bsz, seq, D_MODEL), y_s.reshape(nseq, t_new, D_MODEL),
            kvp[0], kvp[1], kvp[2], st_p[None],
            kvs[0], kvs[1], kvs[2], st_s[None])
```

```python
import functools

import numpy as np
import jax
import jax.numpy as jnp
from jax import lax
from jax.experimental import pallas as pl
from jax.experimental.pallas import tpu as pltpu

F32 = jnp.float32
BF16 = jnp.bfloat16

D_MODEL = 2048
HEAD = 128
A_GROUPS = ((128, 1), (512, 4), (2048, 16))
A_HPG = 4
A_KEYS = 128
A_WIDTH = 3 * A_HPG * HEAD
A_OUT = A_HPG * HEAD
B_HEADS = 8
B_WIDTH = B_HEADS * HEAD
B_CHUNK = 32
REL_BUCKETS = 32
REL_MAX_DIST = 2048
N_EXPERTS = 32
TOP_K = 4
D_FF = 2048
SWIGLU_ALPHA = 1.702
SWIGLU_LIMIT = 7.0
NORM_EPS = 1e-6
NEG_INF = -1e30
ATTN_SCALE = HEAD ** -0.5
IN_WIDTH = 3 * A_WIDTH + 4 * B_WIDTH + 2 * D_MODEL
OFF_Q, OFF_K, OFF_V = 0, A_WIDTH, 2 * A_WIDTH
OFF_QB = 3 * A_WIDTH
OFF_FB, OFF_IB, OFF_OB = OFF_QB + B_WIDTH, OFF_QB + 2 * B_WIDTH, OFF_QB + 3 * B_WIDTH
OFF_GA = OFF_QB + 4 * B_WIDTH
OFF_GB = OFF_GA + D_MODEL

V7X_VMEM_LIMIT = 56 * 1024 * 1024
SUBLANES = 8

NT_DIMS = (((1,), (1,)), ((), ()))


def _sigmoid(x):
    return 1.0 / (1.0 + jnp.exp(-x))


def _cparams(sem):
    return pltpu.CompilerParams(dimension_semantics=sem, vmem_limit_bytes=V7X_VMEM_LIMIT)


PROJ_TM = 512
PROJ_TN = 1280


def _inproj_kernel(x_ref, nw_ref, w_ref, o_ref, wbf_ref):
    @pl.when(pl.program_id(1) == 0)
    def _():
        wbf_ref[...] = w_ref[...].astype(BF16)

    x = x_ref[...]
    ms = jnp.mean(x * x, axis=-1, keepdims=True)
    xn = (x * lax.rsqrt(ms + NORM_EPS) * nw_ref[...]).astype(BF16)
    o_ref[...] = jnp.dot(xn, wbf_ref[...], preferred_element_type=F32)


def _inproj(x, norm_w, w_in):
    m = x.shape[0]
    return pl.pallas_call(
        _inproj_kernel,
        out_shape=jax.ShapeDtypeStruct((m, IN_WIDTH), F32),
        grid=(IN_WIDTH // PROJ_TN, m // PROJ_TM),
        in_specs=[
            pl.BlockSpec((PROJ_TM, D_MODEL), lambda j, i: (i, 0)),
            pl.BlockSpec((1, D_MODEL), lambda j, i: (0, 0)),
            pl.BlockSpec((D_MODEL, PROJ_TN), lambda j, i: (0, j)),
        ],
        out_specs=pl.BlockSpec((PROJ_TM, PROJ_TN), lambda j, i: (i, j)),
        scratch_shapes=[pltpu.VMEM((D_MODEL, PROJ_TN), BF16)],
        compiler_params=_cparams(("arbitrary", "arbitrary")),
        name="inproj",
    )(x, norm_w.reshape(1, D_MODEL), w_in)


QKV_TM = 512


def _qkv_kernel(q_ref, k_ref, v_ref, qw_ref, kw_ref, qn_ref, kv0_ref, kv1_ref, kv2_ref):
    kv_refs = (kv0_ref, kv1_ref, kv2_ref)
    qw = qw_ref[...]
    kw = kw_ref[...]
    for h in range(3 * A_HPG):
        sl = slice(h * HEAD, (h + 1) * HEAD)
        q = q_ref[:, sl]
        qn_ref[:, sl] = q * lax.rsqrt(jnp.mean(q * q, axis=-1, keepdims=True) + NORM_EPS) * qw
        k = k_ref[:, sl]
        kn = k * lax.rsqrt(jnp.mean(k * k, axis=-1, keepdims=True) + NORM_EPS) * kw
        g, hh = divmod(h, A_HPG)
        kv_refs[g][:, hh * HEAD:(hh + 1) * HEAD] = kn
        kv_refs[g][:, A_OUT + hh * HEAD:A_OUT + (hh + 1) * HEAD] = v_ref[:, sl]


def _qkv(proj, q_norm, k_norm):
    m = proj.shape[0]
    kv_shape = jax.ShapeDtypeStruct((m, 2 * A_OUT), F32)
    return pl.pallas_call(
        _qkv_kernel,
        out_shape=(jax.ShapeDtypeStruct((m, A_WIDTH), F32), kv_shape, kv_shape, kv_shape),
        grid=(m // QKV_TM,),
        in_specs=[
            pl.BlockSpec((QKV_TM, A_WIDTH), lambda i: (i, OFF_Q // A_WIDTH)),
            pl.BlockSpec((QKV_TM, A_WIDTH), lambda i: (i, OFF_K // A_WIDTH)),
            pl.BlockSpec((QKV_TM, A_WIDTH), lambda i: (i, OFF_V // A_WIDTH)),
            pl.BlockSpec((1, HEAD), lambda i: (0, 0)),
            pl.BlockSpec((1, HEAD), lambda i: (0, 0)),
        ],
        out_specs=(
            pl.BlockSpec((QKV_TM, A_WIDTH), lambda i: (i, 0)),
            pl.BlockSpec((QKV_TM, 2 * A_OUT), lambda i: (i, 0)),
            pl.BlockSpec((QKV_TM, 2 * A_OUT), lambda i: (i, 0)),
            pl.BlockSpec((QKV_TM, 2 * A_OUT), lambda i: (i, 0)),
        ),
        compiler_params=_cparams(("arbitrary",)),
        name="qkv_norm",
    )(proj, proj, proj, q_norm.reshape(1, HEAD), k_norm.reshape(1, HEAD))


def _t5_bucket(dist):
    dist = np.asarray(dist, np.int64)
    max_exact = REL_BUCKETS // 2
    ratio = np.log(np.maximum(dist, 1) / max_exact) / np.log(REL_MAX_DIST / max_exact)
    large = np.minimum(max_exact + (ratio * (REL_BUCKETS - max_exact)).astype(np.int64), REL_BUCKETS - 1)
    return np.where(dist < max_exact, dist, large).astype(np.int32)


def _bias_lookup(tab, bucket, valid):
    bucket = jnp.asarray(np.where(valid, bucket, -1).astype(np.int32))[None]
    out = jnp.full((tab.shape[1],) + bucket.shape[1:], NEG_INF, F32)
    for b in range(REL_BUCKETS):
        out = jnp.where(bucket == b, tab[b].reshape((-1,) + (1,) * (bucket.ndim - 1)), out)
    return out


def _prompt_bias(rel_bias, g):
    _, dil = A_GROUPS[g]
    n = A_KEYS
    j = n + np.arange(n)[:, None] - np.arange(2 * n)[None, :]
    valid = (j >= 0) & (j < n)
    tab = rel_bias[:, g * A_HPG:(g + 1) * A_HPG].astype(F32)
    return _bias_lookup(tab, _t5_bucket(np.clip(j, 0, n - 1) * dil), valid)


def _attn_prompt_kernel(q_ref, kp_ref, ko_ref, vp_ref, vo_ref, bias_ref, o_ref, l_ref):
    n = A_KEYS
    first_block = pl.program_id(2) == 0
    kc = lax.broadcasted_iota(jnp.int32, (n, 2 * n), 1)
    keep = jnp.logical_or(kc >= n, jnp.logical_not(first_block))
    for h in range(A_HPG):
        sl = slice(h * HEAD, (h + 1) * HEAD)
        q = q_ref[:, sl].astype(BF16)
        k2 = jnp.concatenate([kp_ref[:, sl], ko_ref[:, sl]], axis=0).astype(BF16)
        s = lax.dot_general(q, k2, NT_DIMS, preferred_element_type=F32) * ATTN_SCALE + bias_ref[h]
        s = jnp.where(keep, s, NEG_INF)
        mx = jnp.max(s, axis=-1, keepdims=True)
        p = jnp.exp(s - mx)
        den = jnp.sum(p, axis=-1, keepdims=True)
        v2 = jnp.concatenate([vp_ref[:, sl], vo_ref[:, sl]], axis=0).astype(BF16)
        o_ref[:, sl] = jnp.dot(p.astype(BF16), v2, preferred_element_type=F32) / den
        l_ref[:, sl] = jnp.broadcast_to(mx + jnp.log(den), (n, HEAD))


def _attn_prompt(qn, kv, bias, g, bsz, seq):
    win, dil = A_GROUPS[g]
    n = A_KEYS
    m = seq // dil
    nb = m // n
    qv = qn.reshape(bsz, m, dil * A_WIDTH)
    kvv = kv.reshape(bsz, m, dil * 2 * A_OUT)
    qpb = A_WIDTH // A_OUT
    prev = lambda i: jnp.maximum(i - 1, 0)
    blk = (None, n, A_OUT)
    out_sds = jax.ShapeDtypeStruct((bsz, m, dil * A_OUT), F32)
    o, lse = pl.pallas_call(
        _attn_prompt_kernel,
        out_shape=(out_sds, out_sds),
        grid=(bsz, dil, nb),
        in_specs=[
            pl.BlockSpec(blk, lambda b, r, i: (b, i, r * qpb + g)),
            pl.BlockSpec(blk, lambda b, r, i: (b, prev(i), 2 * r)),
            pl.BlockSpec(blk, lambda b, r, i: (b, i, 2 * r)),
            pl.BlockSpec(blk, lambda b, r, i: (b, prev(i), 2 * r + 1)),
            pl.BlockSpec(blk, lambda b, r, i: (b, i, 2 * r + 1)),
            pl.BlockSpec((A_HPG, n, 2 * n), lambda b, r, i: (0, 0, 0)),
        ],
        out_specs=(
            pl.BlockSpec(blk, lambda b, r, i: (b, i, r)),
            pl.BlockSpec(blk, lambda b, r, i: (b, i, r)),
        ),
        compiler_params=_cparams(("arbitrary", "arbitrary", "arbitrary")),
        name=f"attn_prompt_w{win}",
    )(qv, kvv, kvv, kvv, kvv, bias)
    return o.reshape(bsz * seq, A_OUT), lse.reshape(bsz * seq, A_OUT)


def _hgrn_gates(q, z, lb):
    qs = q * _sigmoid(q)
    logf = jnp.log(lb + (1.0 - lb) * _sigmoid(z))
    kb = (1.0 - lb) * _sigmoid(-z)
    return qs, kb, logf


def _hgrn_out(o, gate, gn):
    on = o * lax.rsqrt(jnp.mean(o * o, axis=-1, keepdims=True) + NORM_EPS) * gn
    return (on * (gate * _sigmoid(gate))).astype(BF16)


def _split3(x):
    hi = x.astype(BF16)
    r1 = x - hi.astype(F32)
    mid = r1.astype(BF16)
    lo = (r1 - mid.astype(F32)).astype(BF16)
    return hi, mid, lo


HG_TC = 256
HG_NC = HG_TC // B_CHUNK
HG_HPS = 4


def _hgrn_prompt_kernel(q_ref, f_ref, i_ref, g_ref, lb_ref, gn_ref, ob_ref, st_ref, s_scr):
    tb = pl.program_id(2)

    @pl.when(tb == 0)
    def _():
        s_scr[...] = jnp.zeros_like(s_scr)

    row = lax.broadcasted_iota(jnp.int32, (HG_TC, HG_TC), 0)
    col = lax.broadcasted_iota(jnp.int32, (HG_TC, HG_TC), 1)
    same_chunk = (row // B_CHUNK) == (col // B_CHUNK)
    causal = jnp.logical_and(same_chunk, col <= row)
    cum_mat = jnp.where(causal, 1.0, 0.0).astype(BF16)
    tcol = lax.broadcasted_iota(jnp.int32, (HEAD, HG_TC), 1) // B_CHUNK
    gn = gn_ref[...]
    for h in range(HG_HPS):
        sl = slice(h * HEAD, (h + 1) * HEAD)
        qs, kb, logf = _hgrn_gates(q_ref[:, sl], f_ref[:, sl], lb_ref[:, sl])
        hi, mid, lo = _split3(logf)
        b = (jnp.dot(cum_mat, hi, preferred_element_type=F32)
             + jnp.dot(cum_mat, mid, preferred_element_type=F32)
             + jnp.dot(cum_mat, lo, preferred_element_type=F32))
        bl_rows = [b[c * B_CHUNK + B_CHUNK - 1:(c + 1) * B_CHUNK, :] for c in range(HG_NC)]
        bl = jnp.concatenate([jnp.broadcast_to(r, (B_CHUNK, HEAD)) for r in bl_rows], axis=0)
        q_dec = (qs * jnp.exp(b)).astype(BF16)
        k_inv = (kb * jnp.exp(-b)).astype(BF16)
        k_end_t = jnp.transpose(kb * jnp.exp(bl - b))
        v = i_ref[:, sl].astype(BF16)
        att = lax.dot_general(q_dec, k_inv, NT_DIMS, preferred_element_type=F32)
        att = jnp.where(causal, att, 0.0).astype(BF16)
        o_intra = jnp.dot(att, v, preferred_element_type=F32)
        decay_t = jnp.exp(jnp.transpose(bl))
        s = s_scr[h]
        inter = []
        for c in range(HG_NC):
            inter.append(jnp.dot(q_dec[c * B_CHUNK:(c + 1) * B_CHUNK, :], s.astype(BF16),
                                 preferred_element_type=F32))
            ke_c = jnp.where(tcol == c, k_end_t, 0.0).astype(BF16)
            ds = jnp.dot(ke_c, v, preferred_element_type=F32)
            s = decay_t[:, c * B_CHUNK:c * B_CHUNK + 1] * s + ds
        s_scr[h] = s
        o = o_intra + jnp.concatenate(inter, axis=0)
        ob_ref[:, sl] = _hgrn_out(o, g_ref[:, sl], gn)

    @pl.when(tb == pl.num_programs(2) - 1)
    def _():
        st_ref[...] = s_scr[...]


def _hgrn_prompt(proj, lb, g_norm, bsz, seq):
    m = proj.shape[0]
    hw = HG_HPS * HEAD
    nt = seq // HG_TC
    rowblk = lambda b, hf, t: b * nt + t
    spec = lambda off: pl.BlockSpec((HG_TC, hw), lambda b, hf, t: (rowblk(b, hf, t), off // hw + hf))
    return pl.pallas_call(
        _hgrn_prompt_kernel,
        out_shape=(jax.ShapeDtypeStruct((m, B_WIDTH), BF16),
                   jax.ShapeDtypeStruct((bsz, B_HEADS, HEAD, HEAD), F32)),
        grid=(bsz, B_HEADS // HG_HPS, nt),
        in_specs=[
            spec(OFF_QB), spec(OFF_FB), spec(OFF_IB), spec(OFF_OB),
            pl.BlockSpec((1, hw), lambda b, hf, t: (0, hf)),
            pl.BlockSpec((1, HEAD), lambda b, hf, t: (0, 0)),
        ],
        out_specs=(
            pl.BlockSpec((HG_TC, hw), lambda b, hf, t: (rowblk(b, hf, t), hf)),
            pl.BlockSpec((None, HG_HPS, HEAD, HEAD), lambda b, hf, t: (b, hf, 0, 0)),
        ),
        scratch_shapes=[pltpu.VMEM((HG_HPS, HEAD, HEAD), F32)],
        compiler_params=_cparams(("arbitrary", "arbitrary", "arbitrary")),
        name="hgrn_prompt",
    )(proj, proj, proj, proj, lb.reshape(1, B_WIDTH), g_norm.reshape(1, HEAD))


HS_SEQ = 32


def _hgrn_sample_kernel(q_ref, f_ref, i_ref, g_ref, lb_ref, gn_ref, s0_ref, ob_ref, st_ref, *, t_new):
    rows = HS_SEQ * t_new
    qs, kb, logf = _hgrn_gates(q_ref[...], f_ref[...], lb_ref[...])
    t_of = lax.broadcasted_iota(jnp.int32, (rows, HEAD), 0) % t_new
    b = logf
    sh = 1
    while sh < t_new:
        b = b + jnp.where(t_of >= sh, pltpu.roll(b, sh, axis=0), 0.0)
        sh *= 2
    bl = b
    for back in range(1, t_new):
        bl = jnp.where(t_of == t_new - 1 - back, pltpu.roll(b, rows - back, axis=0), bl)
    q_dec = (qs * jnp.exp(b)).astype(BF16)
    k_inv = (kb * jnp.exp(-b)).astype(BF16)
    k_end_t = jnp.transpose(kb * jnp.exp(bl - b))
    decay_t = jnp.exp(jnp.transpose(bl))
    v = i_ref[...].astype(BF16)
    row = lax.broadcasted_iota(jnp.int32, (rows, rows), 0)
    col = lax.broadcasted_iota(jnp.int32, (rows, rows), 1)
    causal = jnp.logical_and(row // t_new == col // t_new, col <= row)
    att = lax.dot_general(q_dec, k_inv, NT_DIMS, preferred_element_type=F32)
    att = jnp.where(causal, att, 0.0).astype(BF16)
    o_intra = jnp.dot(att, v, preferred_element_type=F32)
    seq_of_col = lax.broadcasted_iota(jnp.int32, (HEAD, rows), 1) // t_new
    seq_of_row = lax.broadcasted_iota(jnp.int32, (rows, HEAD), 0) // t_new
    o_inter = jnp.zeros((rows, HEAD), F32)
    for s_i in range(HS_SEQ):
        s = s0_ref[s_i, 0]
        part = jnp.dot(q_dec, s.astype(BF16), preferred_element_type=F32)
        o_inter = jnp.where(seq_of_row == s_i, part, o_inter)
        ke = jnp.where(seq_of_col == s_i, k_end_t, 0.0).astype(BF16)
        ds = jnp.dot(ke, v, preferred_element_type=F32)
        st_ref[s_i, 0] = decay_t[:, s_i * t_new:s_i * t_new + 1] * s + ds
    ob_ref[...] = _hgrn_out(o_intra + o_inter, g_ref[...], gn_ref[...])


def _hgrn_sample(proj, lb, g_norm, s0, t_new):
    m = proj.shape[0]
    nseq = m // t_new
    rows = HS_SEQ * t_new
    spec = lambda off: pl.BlockSpec((rows, HEAD), lambda i, h: (i, off // HEAD + h))
    st_spec = pl.BlockSpec((HS_SEQ, 1, HEAD, HEAD), lambda i, h: (i, h, 0, 0))
    return pl.pallas_call(
        functools.partial(_hgrn_sample_kernel, t_new=t_new),
        out_shape=(jax.ShapeDtypeStruct((m, B_WIDTH), BF16),
                   jax.ShapeDtypeStruct((nseq, B_HEADS, HEAD, HEAD), F32)),
        grid=(nseq // HS_SEQ, B_HEADS),
        in_specs=[
            spec(OFF_QB), spec(OFF_FB), spec(OFF_IB), spec(OFF_OB),
            pl.BlockSpec((1, HEAD), lambda i, h: (0, h)),
            pl.BlockSpec((1, HEAD), lambda i, h: (0, 0)),
            st_spec,
        ],
        out_specs=(pl.BlockSpec((rows, HEAD), lambda i, h: (i, h)), st_spec),
        compiler_params=_cparams(("arbitrary", "arbitrary")),
        name="hgrn_sample",
    )(proj, proj, proj, proj, lb.reshape(1, B_WIDTH), g_norm.reshape(1, HEAD), s0)


SA_SEQ = 2
KV_ROWS = 2 * A_HPG


def _sample_bias(rel_bias, t_new):
    def tile(tab, dist, valid):
        bias = _bias_lookup(tab, _t5_bucket(np.clip(dist, 0, REL_MAX_DIST)), valid)
        bias = jnp.moveaxis(bias, 0, -1)
        bias = jnp.concatenate([bias, jnp.zeros(bias.shape[:-1] + (KV_ROWS - A_HPG,), F32)], axis=-1)
        return jnp.broadcast_to(bias[..., None], bias.shape + (HEAD,))

    t = np.arange(t_new)[:, None]
    r = np.arange(A_KEYS + t_new)[None, :]
    is_new = r >= A_KEYS
    dist = np.where(is_new, t - (r - A_KEYS), A_KEYS + t - r)
    valid = (dist >= 0) & (dist < A_KEYS)
    b0 = tile(rel_bias[:, 0:A_HPG].astype(F32), dist, valid)
    b12 = []
    for g in (1, 2):
        _, dil = A_GROUPS[g]
        m = np.arange(A_KEYS + 1)
        dist = dil * (A_KEYS - m)
        b12.append(tile(rel_bias[:, g * A_HPG:(g + 1) * A_HPG].astype(F32), dist, m >= 1))
    return b0, jnp.stack(b12)


def _attn_sample_kernel(q_ref, n0_ref, n1_ref, n2_ref, c0_ref, c1_ref, c2_ref, b0_ref, b12_ref,
                        o_ref, qt_scr, ns_scr, l_scr, *, t_new):
    nq = SA_SEQ * t_new
    new_refs = (n0_ref, n1_ref, n2_ref)
    cache_refs = (c0_ref, c1_ref, c2_ref)
    qt_scr[...] = jnp.zeros_like(qt_scr)
    for g in range(3):
        for j in range(nq):
            for h in range(A_HPG):
                qt_scr[g, j, h:h + 1, :] = (
                    q_ref[j:j + 1, g * A_OUT + h * HEAD:g * A_OUT + (h + 1) * HEAD] * ATTN_SCALE)
                ns_scr[g, j, h:h + 1, :] = new_refs[g][j:j + 1, h * HEAD:(h + 1) * HEAD]
                ns_scr[g, j, A_HPG + h:A_HPG + h + 1, :] = (
                    new_refs[g][j:j + 1, A_OUT + h * HEAD:A_OUT + (h + 1) * HEAD])

    row_id = lax.broadcasted_iota(jnp.int32, (nq, HEAD), 0)

    def per_query(j, out):
        s = j // t_new
        t = j - s * t_new
        row0 = s * A_KEYS
        stats = []
        for g in range(3):
            c_ref = cache_refs[g]
            qt = qt_scr[g, j]
            if g == 0:
                sub0 = 0
                bias_at = lambda m: b0_ref[t, m]
                new_tiles = [(A_KEYS + tp, ns_scr[0, s * t_new + tp]) for tp in range(t_new)]
            else:
                sub0 = pl.multiple_of(t * KV_ROWS, KV_ROWS)
                bias_at = lambda m, g=g: b12_ref[g - 1, m]
                new_tiles = [(A_KEYS, ns_scr[g, j])]

            def score(tile_kv, m):
                l = jnp.sum(tile_kv * qt, axis=-1, keepdims=True) + bias_at(m)
                return l + pltpu.roll(l, A_HPG, axis=0)

            lanes = [jnp.full((KV_ROWS, HEAD), NEG_INF, F32) for _ in range(4)]
            for m in range(A_KEYS):
                l = score(c_ref[row0 + m, pl.ds(sub0, KV_ROWS), :], m)
                l_scr[m] = l
                lanes[m % 4] = jnp.maximum(lanes[m % 4], l)
            mx = jnp.maximum(jnp.maximum(lanes[0], lanes[1]), jnp.maximum(lanes[2], lanes[3]))
            new_l = [score(tile_kv, m) for m, tile_kv in new_tiles]
            for l in new_l:
                mx = jnp.maximum(mx, l)

            def pass2(m, carry):
                den, acc = carry
                p = jnp.exp(l_scr[m] - mx)
                return den + p, acc + p * c_ref[row0 + m, pl.ds(sub0, KV_ROWS), :]

            zero = jnp.zeros((KV_ROWS, HEAD), F32)
            den, acc = lax.fori_loop(0, A_KEYS, pass2, (zero, zero), unroll=8)
            for l, (_, tile_kv) in zip(new_l, new_tiles):
                p = jnp.exp(l - mx)
                den = den + p
                acc = acc + p * tile_kv
            stats.append((mx + jnp.log(den), den, acc))
        top = jnp.maximum(jnp.maximum(stats[0][0], stats[1][0]), stats[2][0])
        e = [jnp.exp(st[0] - top) for st in stats]
        tot = e[0] + e[1] + e[2]
        mix = jnp.zeros((KV_ROWS, HEAD), F32)
        for g in range(3):
            mix = mix + e[g] / (tot * stats[g][1]) * stats[g][2]
        pieces = [jnp.broadcast_to(mix[A_HPG + h:A_HPG + h + 1, :], (nq, HEAD)) for h in range(A_HPG)]
        return tuple(jnp.where(row_id == j, pieces[h], out[h]) for h in range(A_HPG))

    init = tuple(jnp.zeros((nq, HEAD), F32) for _ in range(A_HPG))
    out = lax.fori_loop(0, nq, per_query, init)
    for h in range(A_HPG):
        o_ref[:, h * HEAD:(h + 1) * HEAD] = out[h]


def _attn_sample(qn, kv_new, caches, rel_bias, t_new):
    m = qn.shape[0]
    nseq = m // t_new
    nq = SA_SEQ * t_new
    assert nq % SUBLANES == 0, "the new-token rows of one grid step must fill whole sublane tiles"
    kvw = 2 * A_OUT
    views, specs = [], []
    for g, (win, dil) in enumerate(A_GROUPS):
        c = caches[g]
        assert c.shape[1] == win, "cache must hold exactly one window of positions"
        assert g == 0 or dil >= t_new, "each new token must sit in its own residue class"
        views.append(c.reshape(nseq * A_KEYS, dil * KV_ROWS, HEAD))
        specs.append(pl.BlockSpec((SA_SEQ * A_KEYS, min(dil, t_new) * KV_ROWS, HEAD), lambda i: (i, 0, 0)))
    b0, b12 = _sample_bias(rel_bias, t_new)
    new_spec = pl.BlockSpec((nq, kvw), lambda i: (i, 0))
    return pl.pallas_call(
        functools.partial(_attn_sample_kernel, t_new=t_new),
        out_shape=jax.ShapeDtypeStruct((m, A_OUT), F32),
        grid=(nseq // SA_SEQ,),
        in_specs=[
            pl.BlockSpec((nq, A_WIDTH), lambda i: (i, 0)),
            new_spec, new_spec, new_spec,
            specs[0], specs[1], specs[2],
            pl.BlockSpec(b0.shape, lambda i: (0, 0, 0, 0)),
            pl.BlockSpec(b12.shape, lambda i: (0, 0, 0, 0)),
        ],
        out_specs=pl.BlockSpec((nq, A_OUT), lambda i: (i, 0)),
        scratch_shapes=[
            pltpu.VMEM((3, nq, KV_ROWS, HEAD), F32),
            pltpu.VMEM((3, nq, KV_ROWS, HEAD), F32),
            pltpu.VMEM((A_KEYS, KV_ROWS, HEAD), F32),
        ],
        compiler_params=_cparams(("arbitrary",)),
        name="attn_sample",
    )(qn, kv_new[0], kv_new[1], kv_new[2], views[0], views[1], views[2], b0, b12)


MIX_TM = 512
MIX_TN = 512


def _mix_prompt_kernel(o0, o1, o2, l0, l1, l2, ob_ref, ga_ref, gb_ref, wpa_ref, wpb_ref, out_ref):
    la, lb_, lc = l0[...], l1[...], l2[...]
    mx = jnp.maximum(jnp.maximum(la, lb_), lc)
    ea, eb, ec = jnp.exp(la - mx), jnp.exp(lb_ - mx), jnp.exp(lc - mx)
    oa = (ea * o0[...] + eb * o1[...] + ec * o2[...]) / (ea + eb + ec)
    _mix_tail(oa, ob_ref, ga_ref, gb_ref, wpa_ref, wpb_ref, out_ref)


def _mix_sample_kernel(oa_ref, ob_ref, ga_ref, gb_ref, wpa_ref, wpb_ref, out_ref):
    _mix_tail(oa_ref[...], ob_ref, ga_ref, gb_ref, wpa_ref, wpb_ref, out_ref)


def _mix_tail(oa, ob_ref, ga_ref, gb_ref, wpa_ref, wpb_ref, out_ref):
    pa = jnp.dot(oa.astype(BF16), wpa_ref[...], preferred_element_type=F32)
    pb = jnp.dot(ob_ref[...], wpb_ref[...], preferred_element_type=F32)
    out_ref[...] = (_sigmoid(ga_ref[...]) * pa + _sigmoid(gb_ref[...]) * pb).astype(BF16)


def _mix(attn_parts, ob, proj, w_pa, w_pb):
    m = proj.shape[0]
    row512 = pl.BlockSpec((MIX_TM, A_OUT), lambda i, j: (i, 0))
    common_specs = [
        pl.BlockSpec((MIX_TM, B_WIDTH), lambda i, j: (i, 0)),
        pl.BlockSpec((MIX_TM, MIX_TN), lambda i, j: (i, OFF_GA // MIX_TN + j)),
        pl.BlockSpec((MIX_TM, MIX_TN), lambda i, j: (i, OFF_GB // MIX_TN + j)),
        pl.BlockSpec((A_OUT, MIX_TN), lambda i, j: (0, j)),
        pl.BlockSpec((B_WIDTH, MIX_TN), lambda i, j: (0, j)),
    ]
    if len(attn_parts) == 1:
        body, name = _mix_sample_kernel, "mix_sample"
    else:
        body, name = _mix_prompt_kernel, "mix_prompt"
    return pl.pallas_call(
        body,
        out_shape=jax.ShapeDtypeStruct((m, D_MODEL), BF16),
        grid=(m // MIX_TM, D_MODEL // MIX_TN),
        in_specs=[row512] * len(attn_parts) + common_specs,
        out_specs=pl.BlockSpec((MIX_TM, MIX_TN), lambda i, j: (i, j)),
        compiler_params=_cparams(("arbitrary", "arbitrary")),
        name=name,
    )(*attn_parts, ob, proj, proj, w_pa, w_pb)


RT_LANES = 128


def _resid_kernel(x_ref, mixed_ref, wo_ref, nw_ref, wr_hi_ref, wr_lo_ref, br_ref, h_ref, hn_ref, lg_ref):
    h = x_ref[...] + jnp.dot(mixed_ref[...], wo_ref[...], preferred_element_type=F32)
    h_ref[...] = h
    hn = h * lax.rsqrt(jnp.mean(h * h, axis=-1, keepdims=True) + NORM_EPS) * nw_ref[...]
    hi = hn.astype(BF16)
    hn_ref[...] = hi
    lo = (hn - hi.astype(F32)).astype(BF16)
    lg = (jnp.dot(hi, wr_hi_ref[...], preferred_element_type=F32)
          + jnp.dot(lo, wr_hi_ref[...], preferred_element_type=F32)
          + jnp.dot(hi, wr_lo_ref[...], preferred_element_type=F32))
    lg_ref[...] = lg + br_ref[...]


def _resid(x, mixed, w_o, ffn_norm, wr_hi, wr_lo, br):
    m = x.shape[0]
    full = lambda shape: pl.BlockSpec(shape, lambda i: (0, 0))
    return pl.pallas_call(
        _resid_kernel,
        out_shape=(jax.ShapeDtypeStruct((m, D_MODEL), F32),
                   jax.ShapeDtypeStruct((m, D_MODEL), BF16),
                   jax.ShapeDtypeStruct((m, RT_LANES), F32)),
        grid=(m // MIX_TM,),
        in_specs=[
            pl.BlockSpec((MIX_TM, D_MODEL), lambda i: (i, 0)),
            pl.BlockSpec((MIX_TM, D_MODEL), lambda i: (i, 0)),
            full((D_MODEL, D_MODEL)), full((1, D_MODEL)),
            full((D_MODEL, RT_LANES)), full((D_MODEL, RT_LANES)), full((1, RT_LANES)),
        ],
        out_specs=(
            pl.BlockSpec((MIX_TM, D_MODEL), lambda i: (i, 0)),
            pl.BlockSpec((MIX_TM, D_MODEL), lambda i: (i, 0)),
            pl.BlockSpec((MIX_TM, RT_LANES), lambda i: (i, 0)),
        ),
        compiler_params=_cparams(("arbitrary",)),
        name="resid_router",
    )(x, mixed, w_o, ffn_norm.reshape(1, D_MODEL), wr_hi, wr_lo, br)


MOE_RB = 1280
MOE_SUB = 256
MOE_TF = 256
MOE_NF = D_FF // MOE_TF


MOE_TM = 512
MOE_GRAN = 16
MOE_SLOTS = 2560
MOE_CHUNK = 256
assert MOE_SLOTS >= MOE_TM * TOP_K + N_EXPERTS * (MOE_GRAN - 1) and MOE_SLOTS % MOE_CHUNK == 0


def _route(logits):
    n_tok = logits.shape[0]
    nt = n_tok // MOE_TM
    top_val, top_idx = lax.top_k(logits, TOP_K)
    gate = jax.nn.softmax(top_val, axis=-1)
    na = MOE_TM * TOP_K
    e3 = top_idx.reshape(nt, na)
    hot = e3[..., None] == jnp.arange(N_EXPERTS)
    onehot = hot.astype(jnp.int32)
    lower = (lax.broadcasted_iota(jnp.int32, (na, na), 0) >= lax.broadcasted_iota(jnp.int32, (na, na), 1))
    incl = jnp.einsum('ij,tjk->tik', lower.astype(BF16), hot.astype(BF16),
                      preferred_element_type=F32).astype(jnp.int32)
    rank = jnp.sum((incl - onehot) * onehot, axis=2)
    seg = (incl[:, -1] + MOE_GRAN - 1) // MOE_GRAN * MOE_GRAN
    loc_off = jnp.cumsum(seg, axis=1) - seg
    rows_e = jnp.sum(seg, axis=0)
    padded_sub = (rows_e + MOE_SUB - 1) // MOE_SUB * MOE_SUB
    padded_rb = (rows_e + MOE_RB - 1) // MOE_RB * MOE_RB
    blk_end = jnp.cumsum(padded_rb) // MOE_RB
    row_start = jnp.cumsum(padded_rb) - padded_rb
    seg_start = row_start[None, :] + jnp.cumsum(seg, axis=0) - seg
    slot = jnp.sum(onehot * loc_off[:, None, :], axis=2) + rank
    dest = jnp.sum(onehot * seg_start[:, None, :], axis=2) + rank
    max_rows = n_tok * TOP_K + nt * N_EXPERTS * (MOE_GRAN - 1)
    n_blk = max_rows // MOE_RB + N_EXPERTS
    blk_ids = jnp.arange(n_blk)
    blk_e = jnp.minimum(jnp.searchsorted(blk_end, blk_ids, side='right'), N_EXPERTS - 1).astype(jnp.int32)
    rows_left = padded_sub[blk_e] - (blk_ids * MOE_RB - row_start[blk_e])
    blk_nsub = jnp.clip(rows_left // MOE_SUB, 0, MOE_RB // MOE_SUB).astype(jnp.int32)
    n_used = blk_end[-1:].astype(jnp.int32)
    tail_row = jnp.where(rows_e > 0, row_start + padded_sub - MOE_SUB, -1).astype(jnp.int32)
    gate_rows = jnp.zeros((n_blk * MOE_RB,), F32).at[dest.reshape(-1)].set(gate.reshape(-1))
    layout = dict(seg_start=seg_start.reshape(-1).astype(jnp.int32),
                  seg_gran=(seg // MOE_GRAN).reshape(-1).astype(jnp.int32),
                  loc_off=loc_off.reshape(-1).astype(jnp.int32),
                  tail_row=tail_row)
    slot = slot.reshape(nt, MOE_TM, TOP_K).astype(jnp.int32)
    return blk_e, blk_nsub, n_used, layout, slot, gate_rows, n_blk


def _segment_copies(layout_refs, tile, src_of, dst_of, sem, start):
    seg_start_ref, seg_gran_ref, loc_off_ref = layout_refs

    for e in range(N_EXPERTS):
        idx = tile * N_EXPERTS + e
        loc = loc_off_ref[idx]
        glob = seg_start_ref[idx]

        def body(c, carry, loc=loc, glob=glob):
            copy = pltpu.make_async_copy(
                src_of(pl.multiple_of(loc + c * MOE_GRAN, MOE_GRAN), pl.multiple_of(glob + c * MOE_GRAN, MOE_GRAN)),
                dst_of(pl.multiple_of(loc + c * MOE_GRAN, MOE_GRAN), pl.multiple_of(glob + c * MOE_GRAN, MOE_GRAN)),
                sem)
            if start:
                copy.start()
            else:
                copy.wait()
            return carry

        lax.fori_loop(0, seg_gran_ref[idx], body, 0)


def _slot_onehot(slot_vecs, base, shape, axis):
    ids = base + lax.broadcasted_iota(jnp.int32, shape, axis)
    hit = jnp.zeros(shape, F32)
    for v in slot_vecs:
        hit = jnp.where(ids == v, 1.0, hit)
    return hit.astype(BF16)


def _dispatch_kernel(seg_start_ref, seg_gran_ref, loc_off_ref, tail_ref, slot_ref, x_ref, xs_ref,
                     sorted_scr, zero_scr, sem):
    i = pl.program_id(0)

    @pl.when(i == 0)
    def _():
        zero_scr[...] = jnp.zeros_like(zero_scr)

        def tail_copy(e):
            row = pl.multiple_of(jnp.maximum(tail_ref[e], 0), MOE_SUB)
            return pltpu.make_async_copy(zero_scr, xs_ref.at[pl.ds(row, MOE_SUB)], sem)

        for e in range(N_EXPERTS):
            @pl.when(tail_ref[e] >= 0)
            def _():
                tail_copy(e).start()
        for e in range(N_EXPERTS):
            @pl.when(tail_ref[e] >= 0)
            def _():
                tail_copy(e).wait()

    x = x_ref[...]
    slot_vecs = [slot_ref[k:k + 1, :] for k in range(TOP_K)]
    for c in range(MOE_SLOTS // MOE_CHUNK):
        perm = _slot_onehot(slot_vecs, c * MOE_CHUNK, (MOE_CHUNK, MOE_TM), 0)
        sorted_scr[c * MOE_CHUNK:(c + 1) * MOE_CHUNK, :] = jnp.dot(
            perm, x, preferred_element_type=F32).astype(BF16)

    layout_refs = (seg_start_ref, seg_gran_ref, loc_off_ref)
    src_of = lambda loc, glob: sorted_scr.at[pl.ds(loc, MOE_GRAN)]
    dst_of = lambda loc, glob: xs_ref.at[pl.ds(glob, MOE_GRAN)]
    _segment_copies(layout_refs, i, src_of, dst_of, sem, start=True)
    _segment_copies(layout_refs, i, src_of, dst_of, sem, start=False)


def _dispatch(layout, slot, hn, n_rows):
    n_tok = hn.shape[0]
    slot_t = jnp.swapaxes(slot, 1, 2)
    grid_spec = pltpu.PrefetchScalarGridSpec(
        num_scalar_prefetch=4,
        grid=(n_tok // MOE_TM,),
        in_specs=[
            pl.BlockSpec((None, TOP_K, MOE_TM), lambda i, *_: (i, 0, 0)),
            pl.BlockSpec((MOE_TM, D_MODEL), lambda i, *_: (i, 0)),
        ],
        out_specs=pl.BlockSpec(memory_space=pl.ANY),
        scratch_shapes=[
            pltpu.VMEM((MOE_SLOTS, D_MODEL), BF16),
            pltpu.VMEM((MOE_SUB, D_MODEL), BF16),
            pltpu.SemaphoreType.DMA,
        ],
    )
    return pl.pallas_call(
        _dispatch_kernel,
        out_shape=jax.ShapeDtypeStruct((n_rows, D_MODEL), BF16),
        grid_spec=grid_spec,
        compiler_params=_cparams(("arbitrary",)),
        name="moe_dispatch",
    )(layout["seg_start"], layout["seg_gran"], layout["loc_off"], layout["tail_row"], slot_t, hn)


def _moe_kernel(blk_e_ref, blk_nsub_ref, n_used_ref, xs_ref, gate_ref, wg_ref, wl_ref, bg_ref, bl_ref,
                wd_ref, bd_ref, yb_ref, wg_s, wl_s, wd_s, act_s, y_s, pend_ref, sem):
    i = pl.program_id(0)
    f = pl.program_id(1)
    last_step = jnp.logical_and(i == pl.num_programs(0) - 1, f == MOE_NF - 1)

    def y_copy(blk, sb):
        rows = pl.ds(pl.multiple_of(sb * MOE_SUB, MOE_SUB), MOE_SUB)
        dst = pl.ds(pl.multiple_of(blk * MOE_RB + sb * MOE_SUB, MOE_SUB), MOE_SUB)
        return pltpu.make_async_copy(y_s.at[rows], yb_ref.at[dst], sem)

    def drain():
        blk = pend_ref[1]

        def body(sb, c):
            y_copy(blk, sb).wait()
            return c

        lax.fori_loop(0, pend_ref[0], body, 0)
        pend_ref[0] = 0

    @pl.when(jnp.logical_and(i == 0, f == 0))
    def _():
        pend_ref[0] = 0
        pend_ref[1] = 0

    def over_granules(nsub, fn):
        def pair(sb, c):
            fn(sb * (2 * MOE_SUB), 2 * MOE_SUB)
            return c

        lax.fori_loop(0, nsub // 2, pair, 0)

        @pl.when(nsub % 2 == 1)
        def _():
            fn((nsub - 1) * MOE_SUB, MOE_SUB)

    @pl.when(i < n_used_ref[0])
    def _():
        wg_s[...] = wg_ref[...].astype(BF16)
        wl_s[...] = wl_ref[...].astype(BF16)
        wd_s[pl.ds(pl.multiple_of(f * MOE_TF, MOE_TF), MOE_TF), :] = wd_ref[...].astype(BF16)
        bg = bg_ref[...]
        bl = bl_ref[...]
        nsub = blk_nsub_ref[i]

        def up(start, size):
            rows = pl.ds(pl.multiple_of(start, MOE_SUB), size)
            x = xs_ref[rows, :]
            hg = jnp.dot(x, wg_s[...], preferred_element_type=F32) + bg
            hl = jnp.dot(x, wl_s[...], preferred_element_type=F32) + bl
            glu = jnp.minimum(hg, SWIGLU_LIMIT)
            lin = jnp.clip(hl, -SWIGLU_LIMIT, SWIGLU_LIMIT)
            act = (glu * _sigmoid(SWIGLU_ALPHA * glu) * (lin + 1.0)).astype(BF16)
            for k in range(MOE_NF):
                @pl.when(f == k)
                def _():
                    act_s[rows, k * MOE_TF:(k + 1) * MOE_TF] = act

        over_granules(nsub, up)

        @pl.when(f == MOE_NF - 1)
        def _():
            drain()

            def down(start, size):
                rows = pl.ds(pl.multiple_of(start, MOE_SUB), size)
                y = jnp.dot(act_s[rows, :], wd_s[...], preferred_element_type=F32) + bd_ref[...]
                y_s[rows, :] = y * gate_ref[rows, :]

            over_granules(nsub, down)

            def send(sb, c):
                y_copy(i, sb).start()
                return c

            lax.fori_loop(0, nsub, send, 0)
            pend_ref[0] = nsub
            pend_ref[1] = i

    @pl.when(last_step)
    def _():
        drain()


def _moe_experts(blk_e, blk_nsub, n_used, xs, gate_rows, w_up, b_up, w_down, b_down):
    n_blk = xs.shape[0] // MOE_RB

    def live(i, n_used_ref):
        return jnp.minimum(i, n_used_ref[0] - 1)

    def ff(i, f, n_used_ref):
        return jnp.where(i < n_used_ref[0], f, MOE_NF - 1)

    def expert(i, be, nu):
        return be[live(i, nu)]

    grid_spec = pltpu.PrefetchScalarGridSpec(
        num_scalar_prefetch=3,
        grid=(n_blk, MOE_NF),
        in_specs=[
            pl.BlockSpec((MOE_RB, D_MODEL), lambda i, f, be, ns, nu: (live(i, nu), 0)),
            pl.BlockSpec((MOE_RB, 1), lambda i, f, be, ns, nu: (live(i, nu), 0)),
            pl.BlockSpec((None, D_MODEL, MOE_TF), lambda i, f, be, ns, nu: (expert(i, be, nu), 0, ff(i, f, nu))),
            pl.BlockSpec((None, D_MODEL, MOE_TF),
                         lambda i, f, be, ns, nu: (expert(i, be, nu), 0, ff(i, f, nu) + MOE_NF)),
            pl.BlockSpec((None, 1, MOE_TF), lambda i, f, be, ns, nu: (expert(i, be, nu), 0, ff(i, f, nu))),
            pl.BlockSpec((None, 1, MOE_TF),
                         lambda i, f, be, ns, nu: (expert(i, be, nu), 0, ff(i, f, nu) + MOE_NF)),
            pl.BlockSpec((None, MOE_TF, D_MODEL), lambda i, f, be, ns, nu: (expert(i, be, nu), ff(i, f, nu), 0)),
            pl.BlockSpec((None, 1, D_MODEL), lambda i, f, be, ns, nu: (expert(i, be, nu), 0, 0)),
        ],
        out_specs=pl.BlockSpec(memory_space=pl.ANY),
        scratch_shapes=[
            pltpu.VMEM((D_MODEL, MOE_TF), BF16),
            pltpu.VMEM((D_MODEL, MOE_TF), BF16),
            pltpu.VMEM((D_FF, D_MODEL), BF16),
            pltpu.VMEM((MOE_RB, D_FF), BF16),
            pltpu.VMEM((MOE_RB, D_MODEL), F32),
            pltpu.SMEM((2,), jnp.int32),
            pltpu.SemaphoreType.DMA,
        ],
    )
    b_up3 = b_up.reshape(N_EXPERTS, 1, 2 * D_FF)
    return pl.pallas_call(
        _moe_kernel,
        out_shape=jax.ShapeDtypeStruct(xs.shape, F32),
        grid_spec=grid_spec,
        compiler_params=_cparams(("arbitrary", "arbitrary")),
        name="moe_experts",
    )(blk_e, blk_nsub, n_used, xs, gate_rows.reshape(-1, 1), w_up, w_up, b_up3, b_up3, w_down,
      b_down.reshape(N_EXPERTS, 1, D_MODEL))


def _combine_kernel(seg_start_ref, seg_gran_ref, loc_off_ref, slot_ref, h_ref, yb_ref, y_ref, rows_scr, sem, *,
                    tile0):
    i = pl.program_id(0)

    @pl.when(i == 0)
    def _():
        rows_scr[...] = jnp.zeros_like(rows_scr)

    layout_refs = (seg_start_ref, seg_gran_ref, loc_off_ref)
    src_of = lambda loc, glob: yb_ref.at[pl.ds(glob, MOE_GRAN)]
    dst_of = lambda loc, glob: rows_scr.at[pl.ds(loc, MOE_GRAN)]
    _segment_copies(layout_refs, tile0 + i, src_of, dst_of, sem, start=True)
    _segment_copies(layout_refs, tile0 + i, src_of, dst_of, sem, start=False)

    slot_vecs = [slot_ref[:, k:k + 1] for k in range(TOP_K)]
    acc = h_ref[...]
    for c in range(MOE_SLOTS // MOE_CHUNK):
        pick = _slot_onehot(slot_vecs, c * MOE_CHUNK, (MOE_TM, MOE_CHUNK), 1)
        rows = rows_scr[c * MOE_CHUNK:(c + 1) * MOE_CHUNK, :]
        hi = rows.astype(BF16)
        lo = (rows - hi.astype(F32)).astype(BF16)
        acc = acc + (jnp.dot(pick, hi, preferred_element_type=F32)
                     + jnp.dot(pick, lo, preferred_element_type=F32))
    y_ref[...] = acc


def _combine(layout, slot, h, yb, tile0):
    m = h.shape[0]
    grid_spec = pltpu.PrefetchScalarGridSpec(
        num_scalar_prefetch=3,
        grid=(m // MOE_TM,),
        in_specs=[
            pl.BlockSpec((None, MOE_TM, TOP_K), lambda i, *_: (i + tile0, 0, 0)),
            pl.BlockSpec((MOE_TM, D_MODEL), lambda i, *_: (i, 0)),
            pl.BlockSpec(memory_space=pl.ANY),
        ],
        out_specs=pl.BlockSpec((MOE_TM, D_MODEL), lambda i, *_: (i, 0)),
        scratch_shapes=[pltpu.VMEM((MOE_SLOTS, D_MODEL), F32), pltpu.SemaphoreType.DMA],
    )
    return pl.pallas_call(
        functools.partial(_combine_kernel, tile0=tile0),
        out_shape=jax.ShapeDtypeStruct((m, D_MODEL), F32),
        grid_spec=grid_spec,
        compiler_params=_cparams(("arbitrary",)),
        name="moe_combine",
    )(layout["seg_start"], layout["seg_gran"], layout["loc_off"], slot, h, yb)


def kernel(x_prompt, x_sample, cache_kv_w128, cache_kv_w512, cache_kv_w2048, state_hgrn, rel_bias, attn_norm,
           w_in, q_norm, k_norm, lb_raw, g_norm, w_pa, w_pb, w_o, ffn_norm, w_router, b_router, w_up, b_up,
           w_down, b_down):
    assert attn_norm.shape[0] == 1, "single-layer stack"
    bsz, seq, _ = x_prompt.shape
    nseq, t_new, _ = x_sample.shape
    n_p, n_s = bsz * seq, nseq * t_new
    lb = jax.nn.softmax(lb_raw.astype(F32), axis=0)[0]
    wpa, wpb, wo = w_pa[0].astype(BF16), w_pb[0].astype(BF16), w_o[0].astype(BF16)
    wr = jnp.pad(w_router[0].astype(F32), ((0, 0), (0, RT_LANES - N_EXPERTS)))
    wr_hi = wr.astype(BF16)
    wr_lo = (wr - wr_hi.astype(F32)).astype(BF16)
    br = jnp.pad(b_router[0].astype(F32), (0, RT_LANES - N_EXPERTS)).reshape(1, RT_LANES)
    caches = (cache_kv_w128[0], cache_kv_w512[0], cache_kv_w2048[0])

    xp = x_prompt.reshape(n_p, D_MODEL)
    xs = x_sample.reshape(n_s, D_MODEL)

    proj_p = _inproj(xp, attn_norm[0], w_in[0])
    qn_p, *kv_p = _qkv(proj_p, q_norm[0], k_norm[0])
    parts, lses = [], []
    for g in range(3):
        o, lse = _attn_prompt(qn_p, kv_p[g], _prompt_bias(rel_bias, g), g, bsz, seq)
        parts.append(o)
        lses.append(lse)
    ob_p, st_p = _hgrn_prompt(proj_p, lb, g_norm[0], bsz, seq)
    mixed_p = _mix(parts + lses, ob_p, proj_p, wpa, wpb)
    h_p, hn_p, lg_p = _resid(xp, mixed_p, wo, ffn_norm[0], wr_hi, wr_lo, br)

    proj_s = _inproj(xs, attn_norm[0], w_in[0])
    qn_s, *kv_s = _qkv(proj_s, q_norm[0], k_norm[0])
    oa_s = _attn_sample(qn_s, kv_s, caches, rel_bias, t_new)
    ob_s, st_s = _hgrn_sample(proj_s, lb, g_norm[0], state_hgrn[0], t_new)
    mixed_s = _mix([oa_s], ob_s, proj_s, wpa, wpb)
    h_s, hn_s, lg_s = _resid(xs, mixed_s, wo, ffn_norm[0], wr_hi, wr_lo, br)

    logits = jnp.concatenate([lg_p, lg_s], axis=0)[:, :N_EXPERTS]
    assert n_p % MOE_TM == 0 and n_s % MOE_TM == 0, "each pass must fill whole MoE token tiles"
    blk_e, blk_nsub, n_used, layout, slot, gate_rows, n_blk = _route(logits)
    hn_all = jnp.concatenate([hn_p, hn_s], axis=0)
    xs_sorted = _dispatch(layout, slot, hn_all, n_blk * MOE_RB)
    yb = _moe_experts(blk_e, blk_nsub, n_used, xs_sorted, gate_rows, w_up[0], b_up[0], w_down[0], b_down[0])
    y_p = _combine(layout, slot, h_p, yb, 0)
    y_s = _combine(layout, slot, h_s, yb, n_p // MOE_TM)

    def prompt_rows(kv, win):
        rows = kv.reshape(bsz, seq, 2, A_HPG, HEAD)[:, seq - min(win, seq):]
        return rows[None]

    kvp = [prompt_rows(kv_p[g], A_GROUPS[g][0]) for g in range(3)]
    kvs = [kv_s[g].reshape(1, nseq, t_new, 2, A_HPG, HEAD) for g in range(3)]
    return (y_p.reshape(bsz, seq, D_MODEL), y_s.reshape(nseq, t_new, D_MODEL),
            kvp[0], kvp[1], kvp[2], st_p[None],
            kvs[0], kvs[1], kvs[2], st_s[None])
```

```python
import functools

import numpy as np
import jax
import jax.numpy as jnp
from jax import lax
from jax.experimental import pallas as pl
from jax.experimental.pallas import tpu as pltpu

F32 = jnp.float32
BF16 = jnp.bfloat16

D_MODEL = 2048
HEAD = 128
A_GROUPS = ((128, 1), (512, 4), (2048, 16))
A_HPG = 4
A_KEYS = 128
A_WIDTH = 3 * A_HPG * HEAD
A_OUT = A_HPG * HEAD
B_HEADS = 8
B_WIDTH = B_HEADS * HEAD
B_CHUNK = 32
REL_BUCKETS = 32
REL_MAX_DIST = 2048
N_EXPERTS = 32
TOP_K = 4
D_FF = 2048
SWIGLU_ALPHA = 1.702
SWIGLU_LIMIT = 7.0
NORM_EPS = 1e-6
NEG_INF = -1e30
ATTN_SCALE = HEAD ** -0.5
IN_WIDTH = 3 * A_WIDTH + 4 * B_WIDTH + 2 * D_MODEL
OFF_Q, OFF_K, OFF_V = 0, A_WIDTH, 2 * A_WIDTH
OFF_QB = 3 * A_WIDTH
OFF_FB, OFF_IB, OFF_OB = OFF_QB + B_WIDTH, OFF_QB + 2 * B_WIDTH, OFF_QB + 3 * B_WIDTH
OFF_GA = OFF_QB + 4 * B_WIDTH
OFF_GB = OFF_GA + D_MODEL

V7X_VMEM_LIMIT = 56 * 1024 * 1024
SUBLANES = 8

NT_DIMS = (((1,), (1,)), ((), ()))


def _sigmoid(x):
    return 1.0 / (1.0 + jnp.exp(-x))


def _cparams(sem):
    return pltpu.CompilerParams(dimension_semantics=sem, vmem_limit_bytes=V7X_VMEM_LIMIT)


PROJ_TM = 512
PROJ_TN = 1280


def _inproj_kernel(x_ref, nw_ref, w_ref, o_ref, wbf_ref):
    @pl.when(pl.program_id(1) == 0)
    def _():
        wbf_ref[...] = w_ref[...].astype(BF16)

    x = x_ref[...]
    ms = jnp.mean(x * x, axis=-1, keepdims=True)
    xn = (x * lax.rsqrt(ms + NORM_EPS) * nw_ref[...]).astype(BF16)
    o_ref[...] = jnp.dot(xn, wbf_ref[...], preferred_element_type=F32)


def _inproj(x, norm_w, w_in):
    m = x.shape[0]
    return pl.pallas_call(
        _inproj_kernel,
        out_shape=jax.ShapeDtypeStruct((m, IN_WIDTH), F32),
        grid=(IN_WIDTH // PROJ_TN, m // PROJ_TM),
        in_specs=[
            pl.BlockSpec((PROJ_TM, D_MODEL), lambda j, i: (i, 0)),
            pl.BlockSpec((1, D_MODEL), lambda j, i: (0, 0)),
            pl.BlockSpec((D_MODEL, PROJ_TN), lambda j, i: (0, j)),
        ],
        out_specs=pl.BlockSpec((PROJ_TM, PROJ_TN), lambda j, i: (i, j)),
        scratch_shapes=[pltpu.VMEM((D_MODEL, PROJ_TN), BF16)],
        compiler_params=_cparams(("arbitrary", "arbitrary")),
        name="inproj",
    )(x, norm_w.reshape(1, D_MODEL), w_in)


QKV_TM = 512


def _qkv_kernel(q_ref, k_ref, v_ref, qw_ref, kw_ref, qn_ref, kv0_ref, kv1_ref, kv2_ref):
    kv_refs = (kv0_ref, kv1_ref, kv2_ref)
    qw = qw_ref[...]
    kw = kw_ref[...]
    for h in range(3 * A_HPG):
        sl = slice(h * HEAD, (h + 1) * HEAD)
        q = q_ref[:, sl]
        qn_ref[:, sl] = q * lax.rsqrt(jnp.mean(q * q, axis=-1, keepdims=True) + NORM_EPS) * qw
        k = k_ref[:, sl]
        kn = k * lax.rsqrt(jnp.mean(k * k, axis=-1, keepdims=True) + NORM_EPS) * kw
        g, hh = divmod(h, A_HPG)
        kv_refs[g][:, hh * HEAD:(hh + 1) * HEAD] = kn
        kv_refs[g][:, A_OUT + hh * HEAD:A_OUT + (hh + 1) * HEAD] = v_ref[:, sl]


def _qkv(proj, q_norm, k_norm):
    m = proj.shape[0]
    kv_shape = jax.ShapeDtypeStruct((m, 2 * A_OUT), F32)
    return pl.pallas_call(
        _qkv_kernel,
        out_shape=(jax.ShapeDtypeStruct((m, A_WIDTH), F32), kv_shape, kv_shape, kv_shape),
        grid=(m // QKV_TM,),
        in_specs=[
            pl.BlockSpec((QKV_TM, A_WIDTH), lambda i: (i, OFF_Q // A_WIDTH)),
            pl.BlockSpec((QKV_TM, A_WIDTH), lambda i: (i, OFF_K // A_WIDTH)),
            pl.BlockSpec((QKV_TM, A_WIDTH), lambda i: (i, OFF_V // A_WIDTH)),
            pl.BlockSpec((1, HEAD), lambda i: (0, 0)),
            pl.BlockSpec((1, HEAD), lambda i: (0, 0)),
        ],
        out_specs=(
            pl.BlockSpec((QKV_TM, A_WIDTH), lambda i: (i, 0)),
            pl.BlockSpec((QKV_TM, 2 * A_OUT), lambda i: (i, 0)),
            pl.BlockSpec((QKV_TM, 2 * A_OUT), lambda i: (i, 0)),
            pl.BlockSpec((QKV_TM, 2 * A_OUT), lambda i: (i, 0)),
        ),
        compiler_params=_cparams(("arbitrary",)),
        name="qkv_norm",
    )(proj, proj, proj, q_norm.reshape(1, HEAD), k_norm.reshape(1, HEAD))


def _t5_bucket(dist):
    dist = np.asarray(dist, np.int64)
    max_exact = REL_BUCKETS // 2
    ratio = np.log(np.maximum(dist, 1) / max_exact) / np.log(REL_MAX_DIST / max_exact)
    large = np.minimum(max_exact + (ratio * (REL_BUCKETS - max_exact)).astype(np.int64), REL_BUCKETS - 1)
    return np.where(dist < max_exact, dist, large).astype(np.int32)


def _bias_lookup(tab, bucket, valid):
    bucket = jnp.asarray(np.where(valid, bucket, -1).astype(np.int32))[None]
    out = jnp.full((tab.shape[1],) + bucket.shape[1:], NEG_INF, F32)
    for b in range(REL_BUCKETS):
        out = jnp.where(bucket == b, tab[b].reshape((-1,) + (1,) * (bucket.ndim - 1)), out)
    return out


def _prompt_bias(rel_bias, g):
    _, dil = A_GROUPS[g]
    n = A_KEYS
    j = n + np.arange(n)[:, None] - np.arange(2 * n)[None, :]
    valid = (j >= 0) & (j < n)
    tab = rel_bias[:, g * A_HPG:(g + 1) * A_HPG].astype(F32)
    return _bias_lookup(tab, _t5_bucket(np.clip(j, 0, n - 1) * dil), valid)


ATT_MIX_ROWS = 256


def _attn_prompt_kernel(q0_ref, q1_ref, q2_ref, k0_ref, v0_ref, k1_ref, v1_ref, k2_ref, v2_ref, bias_ref,
                        o_ref, og_scr, lg_scr, *, seq):
    n = A_KEYS
    h = pl.program_id(1)
    q_refs = (q0_ref, q1_ref, q2_ref)
    k_refs = (k0_ref, k1_ref, k2_ref)
    v_refs = (v0_ref, v1_ref, v2_ref)
    for g, (win, dil) in enumerate(A_GROUPS):
        bias = bias_ref[g, h]
        for r in range(dil):
            for i in range(seq // win):
                def rows(blk, r=r, dil=dil):
                    if dil == 1:
                        return pl.ds(blk * n, n)
                    return pl.ds(r + blk * n * dil, n, stride=dil)

                q = q_refs[g][rows(i), :].astype(BF16)
                if i == 0:
                    kk = k_refs[g][rows(i), :].astype(BF16)
                    vv = v_refs[g][rows(i), :].astype(BF16)
                    b_tile = bias[:, n:]
                else:
                    kk = jnp.concatenate([k_refs[g][rows(i - 1), :], k_refs[g][rows(i), :]], axis=0).astype(BF16)
                    vv = jnp.concatenate([v_refs[g][rows(i - 1), :], v_refs[g][rows(i), :]], axis=0).astype(BF16)
                    b_tile = bias
                s = lax.dot_general(q, kk, NT_DIMS, preferred_element_type=F32) * ATTN_SCALE + b_tile
                mx = jnp.max(s, axis=-1, keepdims=True)
                p = jnp.exp(s - mx)
                den = jnp.sum(p, axis=-1, keepdims=True)
                og_scr[g, rows(i), :] = jnp.dot(p.astype(BF16), vv, preferred_element_type=F32) / den
                lg_scr[g, rows(i), :] = jnp.broadcast_to(mx + jnp.log(den), (n, HEAD))
    for c in range(seq // ATT_MIX_ROWS):
        sl = slice(c * ATT_MIX_ROWS, (c + 1) * ATT_MIX_ROWS)
        la, lb_, lc = lg_scr[0, sl, :], lg_scr[1, sl, :], lg_scr[2, sl, :]
        top = jnp.maximum(jnp.maximum(la, lb_), lc)
        ea, eb, ec = jnp.exp(la - top), jnp.exp(lb_ - top), jnp.exp(lc - top)
        o_ref[sl, :] = (ea * og_scr[0, sl, :] + eb * og_scr[1, sl, :] + ec * og_scr[2, sl, :]) / (ea + eb + ec)


def _attn_prompt(qn, kvs, bias, bsz, seq):
    m = qn.shape[0]
    col = lambda fn: pl.BlockSpec((seq, HEAD), lambda b, h: (b, fn(h)))
    in_specs = [col(lambda h, g=g: g * A_HPG + h) for g in range(3)]
    for _ in range(3):
        in_specs += [col(lambda h: h), col(lambda h: A_HPG + h)]
    in_specs.append(pl.BlockSpec(bias.shape, lambda b, h: (0, 0, 0, 0)))
    return pl.pallas_call(
        functools.partial(_attn_prompt_kernel, seq=seq),
        out_shape=jax.ShapeDtypeStruct((m, A_OUT), F32),
        grid=(bsz, A_HPG),
        in_specs=in_specs,
        out_specs=col(lambda h: h),
        scratch_shapes=[pltpu.VMEM((3, seq, HEAD), F32), pltpu.VMEM((3, seq, HEAD), F32)],
        compiler_params=_cparams(("arbitrary", "arbitrary")),
        name="attn_prompt",
    )(qn, qn, qn, kvs[0], kvs[0], kvs[1], kvs[1], kvs[2], kvs[2], bias)


def _hgrn_gates(q, z, lb):
    qs = q * _sigmoid(q)
    logf = jnp.log(lb + (1.0 - lb) * _sigmoid(z))
    kb = (1.0 - lb) * _sigmoid(-z)
    return qs, kb, logf


def _hgrn_out(o, gate, gn):
    on = o * lax.rsqrt(jnp.mean(o * o, axis=-1, keepdims=True) + NORM_EPS) * gn
    return (on * (gate * _sigmoid(gate))).astype(BF16)


def _split3(x):
    hi = x.astype(BF16)
    r1 = x - hi.astype(F32)
    mid = r1.astype(BF16)
    lo = (r1 - mid.astype(F32)).astype(BF16)
    return hi, mid, lo


HG_TC = 256
HG_NC = HG_TC // B_CHUNK
HG_HPS = 4


def _hgrn_prompt_kernel(q_ref, f_ref, i_ref, g_ref, lb_ref, gn_ref, ob_ref, st_ref, s_scr):
    tb = pl.program_id(2)

    @pl.when(tb == 0)
    def _():
        s_scr[...] = jnp.zeros_like(s_scr)

    row = lax.broadcasted_iota(jnp.int32, (HG_TC, HG_TC), 0)
    col = lax.broadcasted_iota(jnp.int32, (HG_TC, HG_TC), 1)
    same_chunk = (row // B_CHUNK) == (col // B_CHUNK)
    causal = jnp.logical_and(same_chunk, col <= row)
    cum_mat = jnp.where(causal, 1.0, 0.0).astype(BF16)
    tcol = lax.broadcasted_iota(jnp.int32, (HEAD, HG_TC), 1) // B_CHUNK
    gn = gn_ref[...]
    for h in range(HG_HPS):
        sl = slice(h * HEAD, (h + 1) * HEAD)
        qs, kb, logf = _hgrn_gates(q_ref[:, sl], f_ref[:, sl], lb_ref[:, sl])
        hi, mid, lo = _split3(logf)
        b = (jnp.dot(cum_mat, hi, preferred_element_type=F32)
             + jnp.dot(cum_mat, mid, preferred_element_type=F32)
             + jnp.dot(cum_mat, lo, preferred_element_type=F32))
        bl_rows = [b[c * B_CHUNK + B_CHUNK - 1:(c + 1) * B_CHUNK, :] for c in range(HG_NC)]
        bl = jnp.concatenate([jnp.broadcast_to(r, (B_CHUNK, HEAD)) for r in bl_rows], axis=0)
        q_dec = (qs * jnp.exp(b)).astype(BF16)
        k_inv = (kb * jnp.exp(-b)).astype(BF16)
        k_end_t = jnp.transpose(kb * jnp.exp(bl - b))
        v = i_ref[:, sl].astype(BF16)
        att = lax.dot_general(q_dec, k_inv, NT_DIMS, preferred_element_type=F32)
        att = jnp.where(causal, att, 0.0).astype(BF16)
        o_intra = jnp.dot(att, v, preferred_element_type=F32)
        decay_t = jnp.exp(jnp.transpose(bl))
        s = s_scr[h]
        inter = []
        for c in range(HG_NC):
            inter.append(jnp.dot(q_dec[c * B_CHUNK:(c + 1) * B_CHUNK, :], s.astype(BF16),
                                 preferred_element_type=F32))
            ke_c = jnp.where(tcol == c, k_end_t, 0.0).astype(BF16)
            ds = jnp.dot(ke_c, v, preferred_element_type=F32)
            s = decay_t[:, c * B_CHUNK:c * B_CHUNK + 1] * s + ds
        s_scr[h] = s
        o = o_intra + jnp.concatenate(inter, axis=0)
        ob_ref[:, sl] = _hgrn_out(o, g_ref[:, sl], gn)

    @pl.when(tb == pl.num_programs(2) - 1)
    def _():
        st_ref[...] = s_scr[...]


def _hgrn_prompt(proj, lb, g_norm, bsz, seq):
    m = proj.shape[0]
    hw = HG_HPS * HEAD
    nt = seq // HG_TC
    rowblk = lambda b, hf, t: b * nt + t
    spec = lambda off: pl.BlockSpec((HG_TC, hw), lambda b, hf, t: (rowblk(b, hf, t), off // hw + hf))
    return pl.pallas_call(
        _hgrn_prompt_kernel,
        out_shape=(jax.ShapeDtypeStruct((m, B_WIDTH), BF16),
                   jax.ShapeDtypeStruct((bsz, B_HEADS, HEAD, HEAD), F32)),
        grid=(bsz, B_HEADS // HG_HPS, nt),
        in_specs=[
            spec(OFF_QB), spec(OFF_FB), spec(OFF_IB), spec(OFF_OB),
            pl.BlockSpec((1, hw), lambda b, hf, t: (0, hf)),
            pl.BlockSpec((1, HEAD), lambda b, hf, t: (0, 0)),
        ],
        out_specs=(
            pl.BlockSpec((HG_TC, hw), lambda b, hf, t: (rowblk(b, hf, t), hf)),
            pl.BlockSpec((None, HG_HPS, HEAD, HEAD), lambda b, hf, t: (b, hf, 0, 0)),
        ),
        scratch_shapes=[pltpu.VMEM((HG_HPS, HEAD, HEAD), F32)],
        compiler_params=_cparams(("arbitrary", "arbitrary", "arbitrary")),
        name="hgrn_prompt",
    )(proj, proj, proj, proj, lb.reshape(1, B_WIDTH), g_norm.reshape(1, HEAD))


HS_SEQ = 32


def _hgrn_sample_kernel(q_ref, f_ref, i_ref, g_ref, lb_ref, gn_ref, s0_ref, ob_ref, st_ref, *, t_new):
    rows = HS_SEQ * t_new
    qs, kb, logf = _hgrn_gates(q_ref[...], f_ref[...], lb_ref[...])
    t_of = lax.broadcasted_iota(jnp.int32, (rows, HEAD), 0) % t_new
    b = logf
    sh = 1
    while sh < t_new:
        b = b + jnp.where(t_of >= sh, pltpu.roll(b, sh, axis=0), 0.0)
        sh *= 2
    bl = b
    for back in range(1, t_new):
        bl = jnp.where(t_of == t_new - 1 - back, pltpu.roll(b, rows - back, axis=0), bl)
    q_dec = (qs * jnp.exp(b)).astype(BF16)
    k_inv = (kb * jnp.exp(-b)).astype(BF16)
    k_end_t = jnp.transpose(kb * jnp.exp(bl - b))
    decay_t = jnp.exp(jnp.transpose(bl))
    v = i_ref[...].astype(BF16)
    row = lax.broadcasted_iota(jnp.int32, (rows, rows), 0)
    col = lax.broadcasted_iota(jnp.int32, (rows, rows), 1)
    causal = jnp.logical_and(row // t_new == col // t_new, col <= row)
    att = lax.dot_general(q_dec, k_inv, NT_DIMS, preferred_element_type=F32)
    att = jnp.where(causal, att, 0.0).astype(BF16)
    o_intra = jnp.dot(att, v, preferred_element_type=F32)
    seq_of_col = lax.broadcasted_iota(jnp.int32, (HEAD, rows), 1) // t_new
    seq_of_row = lax.broadcasted_iota(jnp.int32, (rows, HEAD), 0) // t_new
    o_inter = jnp.zeros((rows, HEAD), F32)
    for s_i in range(HS_SEQ):
        s = s0_ref[s_i, 0]
        part = jnp.dot(q_dec, s.astype(BF16), preferred_element_type=F32)
        o_inter = jnp.where(seq_of_row == s_i, part, o_inter)
        ke = jnp.where(seq_of_col == s_i, k_end_t, 0.0).astype(BF16)
        ds = jnp.dot(ke, v, preferred_element_type=F32)
        st_ref[s_i, 0] = decay_t[:, s_i * t_new:s_i * t_new + 1] * s + ds
    ob_ref[...] = _hgrn_out(o_intra + o_inter, g_ref[...], gn_ref[...])


def _hgrn_sample(proj, lb, g_norm, s0, t_new):
    m = proj.shape[0]
    nseq = m // t_new
    rows = HS_SEQ * t_new
    spec = lambda off: pl.BlockSpec((rows, HEAD), lambda i, h: (i, off // HEAD + h))
    st_spec = pl.BlockSpec((HS_SEQ, 1, HEAD, HEAD), lambda i, h: (i, h, 0, 0))
    return pl.pallas_call(
        functools.partial(_hgrn_sample_kernel, t_new=t_new),
        out_shape=(jax.ShapeDtypeStruct((m, B_WIDTH), BF16),
                   jax.ShapeDtypeStruct((nseq, B_HEADS, HEAD, HEAD), F32)),
        grid=(nseq // HS_SEQ, B_HEADS),
        in_specs=[
            spec(OFF_QB), spec(OFF_FB), spec(OFF_IB), spec(OFF_OB),
            pl.BlockSpec((1, HEAD), lambda i, h: (0, h)),
            pl.BlockSpec((1, HEAD), lambda i, h: (0, 0)),
            st_spec,
        ],
        out_specs=(pl.BlockSpec((rows, HEAD), lambda i, h: (i, h)), st_spec),
        compiler_params=_cparams(("arbitrary", "arbitrary")),
        name="hgrn_sample",
    )(proj, proj, proj, proj, lb.reshape(1, B_WIDTH), g_norm.reshape(1, HEAD), s0)


SA_SEQ = 2
KV_ROWS = 2 * A_HPG


def _sample_bias(rel_bias, t_new):
    def tile(tab, dist, valid):
        bias = _bias_lookup(tab, _t5_bucket(np.clip(dist, 0, REL_MAX_DIST)), valid)
        bias = jnp.moveaxis(bias, 0, -1)
        bias = jnp.concatenate([bias, jnp.zeros(bias.shape[:-1] + (KV_ROWS - A_HPG,), F32)], axis=-1)
        return jnp.broadcast_to(bias[..., None], bias.shape + (HEAD,))

    t = np.arange(t_new)[:, None]
    r = np.arange(A_KEYS + t_new)[None, :]
    is_new = r >= A_KEYS
    dist = np.where(is_new, t - (r - A_KEYS), A_KEYS + t - r)
    valid = (dist >= 0) & (dist < A_KEYS)
    b0 = tile(rel_bias[:, 0:A_HPG].astype(F32), dist, valid)
    b12 = []
    for g in (1, 2):
        _, dil = A_GROUPS[g]
        m = np.arange(A_KEYS + 1)
        dist = dil * (A_KEYS - m)
        b12.append(tile(rel_bias[:, g * A_HPG:(g + 1) * A_HPG].astype(F32), dist, m >= 1))
    return b0, jnp.stack(b12)


def _attn_sample_kernel(q_ref, n0_ref, n1_ref, n2_ref, c0_ref, c1_ref, c2_ref, b0_ref, b12_ref,
                        o_ref, qt_scr, ns_scr, l_scr, *, t_new):
    nq = SA_SEQ * t_new
    new_refs = (n0_ref, n1_ref, n2_ref)
    cache_refs = (c0_ref, c1_ref, c2_ref)
    qt_scr[...] = jnp.zeros_like(qt_scr)
    for g in range(3):
        for j in range(nq):
            for h in range(A_HPG):
                qt_scr[g, j, h:h + 1, :] = (
                    q_ref[j:j + 1, g * A_OUT + h * HEAD:g * A_OUT + (h + 1) * HEAD] * ATTN_SCALE)
                ns_scr[g, j, h:h + 1, :] = new_refs[g][j:j + 1, h * HEAD:(h + 1) * HEAD]
                ns_scr[g, j, A_HPG + h:A_HPG + h + 1, :] = (
                    new_refs[g][j:j + 1, A_OUT + h * HEAD:A_OUT + (h + 1) * HEAD])

    row_id = lax.broadcasted_iota(jnp.int32, (nq, HEAD), 0)

    def per_query(j, out):
        s = j // t_new
        t = j - s * t_new
        row0 = s * A_KEYS
        stats = []
        for g in range(3):
            c_ref = cache_refs[g]
            qt = qt_scr[g, j]
            if g == 0:
                sub0 = 0
                bias_at = lambda m: b0_ref[t, m]
                new_tiles = [(A_KEYS + tp, ns_scr[0, s * t_new + tp]) for tp in range(t_new)]
            else:
                sub0 = pl.multiple_of(t * KV_ROWS, KV_ROWS)
                bias_at = lambda m, g=g: b12_ref[g - 1, m]
                new_tiles = [(A_KEYS, ns_scr[g, j])]

            def score(tile_kv, m):
                l = jnp.sum(tile_kv * qt, axis=-1, keepdims=True) + bias_at(m)
                return l + pltpu.roll(l, A_HPG, axis=0)

            lanes = [jnp.full((KV_ROWS, HEAD), NEG_INF, F32) for _ in range(4)]
            for m in range(A_KEYS):
                l = score(c_ref[row0 + m, pl.ds(sub0, KV_ROWS), :], m)
                l_scr[m] = l
                lanes[m % 4] = jnp.maximum(lanes[m % 4], l)
            mx = jnp.maximum(jnp.maximum(lanes[0], lanes[1]), jnp.maximum(lanes[2], lanes[3]))
            new_l = [score(tile_kv, m) for m, tile_kv in new_tiles]
            for l in new_l:
                mx = jnp.maximum(mx, l)

            zero = jnp.zeros((KV_ROWS, HEAD), F32)
            dens, accs = [zero] * 4, [zero] * 4
            for m in range(A_KEYS):
                p = jnp.exp(l_scr[m] - mx)
                dens[m % 4] = dens[m % 4] + p
                accs[m % 4] = accs[m % 4] + p * c_ref[row0 + m, pl.ds(sub0, KV_ROWS), :]
            den = (dens[0] + dens[1]) + (dens[2] + dens[3])
            acc = (accs[0] + accs[1]) + (accs[2] + accs[3])
            for l, (_, tile_kv) in zip(new_l, new_tiles):
                p = jnp.exp(l - mx)
                den = den + p
                acc = acc + p * tile_kv
            stats.append((mx + jnp.log(den), den, acc))
        top = jnp.maximum(jnp.maximum(stats[0][0], stats[1][0]), stats[2][0])
        e = [jnp.exp(st[0] - top) for st in stats]
        tot = e[0] + e[1] + e[2]
        mix = jnp.zeros((KV_ROWS, HEAD), F32)
        for g in range(3):
            mix = mix + e[g] / (tot * stats[g][1]) * stats[g][2]
        pieces = [jnp.broadcast_to(mix[A_HPG + h:A_HPG + h + 1, :], (nq, HEAD)) for h in range(A_HPG)]
        return tuple(jnp.where(row_id == j, pieces[h], out[h]) for h in range(A_HPG))

    init = tuple(jnp.zeros((nq, HEAD), F32) for _ in range(A_HPG))
    out = lax.fori_loop(0, nq, per_query, init)
    for h in range(A_HPG):
        o_ref[:, h * HEAD:(h + 1) * HEAD] = out[h]


def _attn_sample(qn, kv_new, caches, rel_bias, t_new):
    m = qn.shape[0]
    nseq = m // t_new
    nq = SA_SEQ * t_new
    assert nq % SUBLANES == 0, "the new-token rows of one grid step must fill whole sublane tiles"
    kvw = 2 * A_OUT
    views, specs = [], []
    for g, (win, dil) in enumerate(A_GROUPS):
        c = caches[g]
        assert c.shape[1] == win, "cache must hold exactly one window of positions"
        assert g == 0 or dil >= t_new, "each new token must sit in its own residue class"
        views.append(c.reshape(nseq * A_KEYS, dil * KV_ROWS, HEAD))
        specs.append(pl.BlockSpec((SA_SEQ * A_KEYS, min(dil, t_new) * KV_ROWS, HEAD), lambda i: (i, 0, 0)))
    b0, b12 = _sample_bias(rel_bias, t_new)
    new_spec = pl.BlockSpec((nq, kvw), lambda i: (i, 0))
    return pl.pallas_call(
        functools.partial(_attn_sample_kernel, t_new=t_new),
        out_shape=jax.ShapeDtypeStruct((m, A_OUT), F32),
        grid=(nseq // SA_SEQ,),
        in_specs=[
            pl.BlockSpec((nq, A_WIDTH), lambda i: (i, 0)),
            new_spec, new_spec, new_spec,
            specs[0], specs[1], specs[2],
            pl.BlockSpec(b0.shape, lambda i: (0, 0, 0, 0)),
            pl.BlockSpec(b12.shape, lambda i: (0, 0, 0, 0)),
        ],
        out_specs=pl.BlockSpec((nq, A_OUT), lambda i: (i, 0)),
        scratch_shapes=[
            pltpu.VMEM((3, nq, KV_ROWS, HEAD), F32),
            pltpu.VMEM((3, nq, KV_ROWS, HEAD), F32),
            pltpu.VMEM((A_KEYS, KV_ROWS, HEAD), F32),
        ],
        compiler_params=_cparams(("arbitrary",)),
        name="attn_sample",
    )(qn, kv_new[0], kv_new[1], kv_new[2], views[0], views[1], views[2], b0, b12)


MIX_TM = 512
MIX_TN = 512


def _mix_kernel(oa_ref, ob_ref, ga_ref, gb_ref, wpa_ref, wpb_ref, out_ref):
    pa = jnp.dot(oa_ref[...].astype(BF16), wpa_ref[...], preferred_element_type=F32)
    pb = jnp.dot(ob_ref[...], wpb_ref[...], preferred_element_type=F32)
    out_ref[...] = (_sigmoid(ga_ref[...]) * pa + _sigmoid(gb_ref[...]) * pb).astype(BF16)


def _mix(oa, ob, proj, w_pa, w_pb):
    m = proj.shape[0]
    return pl.pallas_call(
        _mix_kernel,
        out_shape=jax.ShapeDtypeStruct((m, D_MODEL), BF16),
        grid=(m // MIX_TM, D_MODEL // MIX_TN),
        in_specs=[
            pl.BlockSpec((MIX_TM, A_OUT), lambda i, j: (i, 0)),
            pl.BlockSpec((MIX_TM, B_WIDTH), lambda i, j: (i, 0)),
            pl.BlockSpec((MIX_TM, MIX_TN), lambda i, j: (i, OFF_GA // MIX_TN + j)),
            pl.BlockSpec((MIX_TM, MIX_TN), lambda i, j: (i, OFF_GB // MIX_TN + j)),
            pl.BlockSpec((A_OUT, MIX_TN), lambda i, j: (0, j)),
            pl.BlockSpec((B_WIDTH, MIX_TN), lambda i, j: (0, j)),
        ],
        out_specs=pl.BlockSpec((MIX_TM, MIX_TN), lambda i, j: (i, j)),
        compiler_params=_cparams(("arbitrary", "arbitrary")),
        name="mix",
    )(oa, ob, proj, proj, w_pa, w_pb)


RT_LANES = 128


def _resid_kernel(x_ref, mixed_ref, wo_ref, nw_ref, wr_hi_ref, wr_lo_ref, br_ref, h_ref, hn_ref, lg_ref):
    h = x_ref[...] + jnp.dot(mixed_ref[...], wo_ref[...], preferred_element_type=F32)
    h_ref[...] = h
    hn = h * lax.rsqrt(jnp.mean(h * h, axis=-1, keepdims=True) + NORM_EPS) * nw_ref[...]
    hi = hn.astype(BF16)
    hn_ref[...] = hi
    lo = (hn - hi.astype(F32)).astype(BF16)
    lg = (jnp.dot(hi, wr_hi_ref[...], preferred_element_type=F32)
          + jnp.dot(lo, wr_hi_ref[...], preferred_element_type=F32)
          + jnp.dot(hi, wr_lo_ref[...], preferred_element_type=F32))
    lg_ref[...] = lg + br_ref[...]


def _resid(x, mixed, w_o, ffn_norm, wr_hi, wr_lo, br):
    m = x.shape[0]
    full = lambda shape: pl.BlockSpec(shape, lambda i: (0, 0))
    return pl.pallas_call(
        _resid_kernel,
        out_shape=(jax.ShapeDtypeStruct((m, D_MODEL), F32),
                   jax.ShapeDtypeStruct((m, D_MODEL), BF16),
                   jax.ShapeDtypeStruct((m, RT_LANES), F32)),
        grid=(m // MIX_TM,),
        in_specs=[
            pl.BlockSpec((MIX_TM, D_MODEL), lambda i: (i, 0)),
            pl.BlockSpec((MIX_TM, D_MODEL), lambda i: (i, 0)),
            full((D_MODEL, D_MODEL)), full((1, D_MODEL)),
            full((D_MODEL, RT_LANES)), full((D_MODEL, RT_LANES)), full((1, RT_LANES)),
        ],
        out_specs=(
            pl.BlockSpec((MIX_TM, D_MODEL), lambda i: (i, 0)),
            pl.BlockSpec((MIX_TM, D_MODEL), lambda i: (i, 0)),
            pl.BlockSpec((MIX_TM, RT_LANES), lambda i: (i, 0)),
        ),
        compiler_params=_cparams(("arbitrary",)),
        name="resid_router",
    )(x, mixed, w_o, ffn_norm.reshape(1, D_MODEL), wr_hi, wr_lo, br)


MOE_RB = 1280
MOE_SUB = 256
MOE_TF = 256
MOE_NF = D_FF // MOE_TF


MOE_TM = 512
MOE_GRAN = 16
MOE_SLOTS = 2560
MOE_CHUNK = 256
assert MOE_SLOTS >= MOE_TM * TOP_K + N_EXPERTS * (MOE_GRAN - 1) and MOE_SLOTS % MOE_CHUNK == 0


def _route(logits):
    n_tok = logits.shape[0]
    nt = n_tok // MOE_TM
    top_val, top_idx = lax.top_k(logits, TOP_K)
    gate = jax.nn.softmax(top_val, axis=-1)
    na = MOE_TM * TOP_K
    e3 = top_idx.reshape(nt, na)
    hot = e3[..., None] == jnp.arange(N_EXPERTS)
    onehot = hot.astype(jnp.int32)
    lower = (lax.broadcasted_iota(jnp.int32, (na, na), 0) >= lax.broadcasted_iota(jnp.int32, (na, na), 1))
    incl = jnp.einsum('ij,tjk->tik', lower.astype(BF16), hot.astype(BF16),
                      preferred_element_type=F32).astype(jnp.int32)
    rank = jnp.sum((incl - onehot) * onehot, axis=2)
    seg = (incl[:, -1] + MOE_GRAN - 1) // MOE_GRAN * MOE_GRAN
    loc_off = jnp.cumsum(seg, axis=1) - seg
    rows_e = jnp.sum(seg, axis=0)
    padded_sub = (rows_e + MOE_SUB - 1) // MOE_SUB * MOE_SUB
    padded_rb = (rows_e + MOE_RB - 1) // MOE_RB * MOE_RB
    blk_end = jnp.cumsum(padded_rb) // MOE_RB
    row_start = jnp.cumsum(padded_rb) - padded_rb
    seg_start = row_start[None, :] + jnp.cumsum(seg, axis=0) - seg
    slot = jnp.sum(onehot * loc_off[:, None, :], axis=2) + rank
    max_rows = n_tok * TOP_K + nt * N_EXPERTS * (MOE_GRAN - 1)
    n_blk = max_rows // MOE_RB + N_EXPERTS
    blk_ids = jnp.arange(n_blk)
    blk_e = jnp.minimum(jnp.searchsorted(blk_end, blk_ids, side='right'), N_EXPERTS - 1).astype(jnp.int32)
    rows_left = padded_sub[blk_e] - (blk_ids * MOE_RB - row_start[blk_e])
    blk_nsub = jnp.clip(rows_left // MOE_SUB, 0, MOE_RB // MOE_SUB).astype(jnp.int32)
    n_used = blk_end[-1:].astype(jnp.int32)
    tail_row = jnp.where(rows_e > 0, row_start + padded_sub - MOE_SUB, -1).astype(jnp.int32)
    layout = dict(seg_start=seg_start.reshape(-1).astype(jnp.int32),
                  seg_gran=(seg // MOE_GRAN).reshape(-1).astype(jnp.int32),
                  loc_off=loc_off.reshape(-1).astype(jnp.int32),
                  tail_row=tail_row)
    slot = slot.reshape(nt, MOE_TM, TOP_K).astype(jnp.int32)
    return blk_e, blk_nsub, n_used, layout, slot, gate.reshape(nt, MOE_TM, TOP_K), n_blk


def _segment_copies(layout_refs, tile, src_of, dst_of, sem, start):
    seg_start_ref, seg_gran_ref, loc_off_ref = layout_refs

    for e in range(N_EXPERTS):
        idx = tile * N_EXPERTS + e
        loc = loc_off_ref[idx]
        glob = seg_start_ref[idx]

        def body(c, carry, loc=loc, glob=glob):
            copy = pltpu.make_async_copy(
                src_of(pl.multiple_of(loc + c * MOE_GRAN, MOE_GRAN), pl.multiple_of(glob + c * MOE_GRAN, MOE_GRAN)),
                dst_of(pl.multiple_of(loc + c * MOE_GRAN, MOE_GRAN), pl.multiple_of(glob + c * MOE_GRAN, MOE_GRAN)),
                sem)
            if start:
                copy.start()
            else:
                copy.wait()
            return carry

        lax.fori_loop(0, seg_gran_ref[idx], body, 0)


def _slot_onehot(slot_vecs, base, shape, axis):
    ids = base + lax.broadcasted_iota(jnp.int32, shape, axis)
    hit = jnp.zeros(shape, F32)
    for v in slot_vecs:
        hit = jnp.where(ids == v, 1.0, hit)
    return hit.astype(BF16)


def _dispatch_kernel(seg_start_ref, seg_gran_ref, loc_off_ref, tail_ref, slot_ref, gate_ref, x_ref,
                     xs_ref, gs_ref, sorted_scr, gsorted_scr, zero_scr, gzero_scr, sem):
    i = pl.program_id(0)

    @pl.when(i == 0)
    def _():
        zero_scr[...] = jnp.zeros_like(zero_scr)
        gzero_scr[...] = jnp.zeros_like(gzero_scr)

        def tail_copies(e):
            rows = pl.ds(pl.multiple_of(jnp.maximum(tail_ref[e], 0), MOE_SUB), MOE_SUB)
            return (pltpu.make_async_copy(zero_scr, xs_ref.at[rows], sem),
                    pltpu.make_async_copy(gzero_scr, gs_ref.at[rows], sem))

        for e in range(N_EXPERTS):
            @pl.when(tail_ref[e] >= 0)
            def _():
                for copy in tail_copies(e):
                    copy.start()
        for e in range(N_EXPERTS):
            @pl.when(tail_ref[e] >= 0)
            def _():
                for copy in tail_copies(e):
                    copy.wait()

    x = x_ref[...]
    slot_vecs = [slot_ref[k:k + 1, :] for k in range(TOP_K)]
    for c in range(MOE_SLOTS // MOE_CHUNK):
        chunk = slice(c * MOE_CHUNK, (c + 1) * MOE_CHUNK)
        ids = c * MOE_CHUNK + lax.broadcasted_iota(jnp.int32, (MOE_CHUNK, MOE_TM), 0)
        hit = jnp.zeros((MOE_CHUNK, MOE_TM), F32)
        gval = jnp.zeros((MOE_CHUNK, MOE_TM), F32)
        for k in range(TOP_K):
            match = ids == slot_vecs[k]
            hit = jnp.where(match, 1.0, hit)
            gval = jnp.where(match, gate_ref[k:k + 1, :], gval)
        sorted_scr[chunk, :] = jnp.dot(hit.astype(BF16), x,
                                       preferred_element_type=F32).astype(BF16)
        gsorted_scr[chunk, :] = jnp.broadcast_to(jnp.sum(gval, axis=1, keepdims=True), (MOE_CHUNK, HEAD))

    layout_refs = (seg_start_ref, seg_gran_ref, loc_off_ref)
    streams = (
        (lambda loc, glob: sorted_scr.at[pl.ds(loc, MOE_GRAN)], lambda loc, glob: xs_ref.at[pl.ds(glob, MOE_GRAN)]),
        (lambda loc, glob: gsorted_scr.at[pl.ds(loc, MOE_GRAN)], lambda loc, glob: gs_ref.at[pl.ds(glob, MOE_GRAN)]),
    )
    for start in (True, False):
        for src_of, dst_of in streams:
            _segment_copies(layout_refs, i, src_of, dst_of, sem, start=start)


def _dispatch(layout, slot, gate, hn, n_rows):
    n_tok = hn.shape[0]
    slot_t = jnp.swapaxes(slot, 1, 2)
    gate_t = jnp.swapaxes(gate, 1, 2)
    grid_spec = pltpu.PrefetchScalarGridSpec(
        num_scalar_prefetch=4,
        grid=(n_tok // MOE_TM,),
        in_specs=[
            pl.BlockSpec((None, TOP_K, MOE_TM), lambda i, *_: (i, 0, 0)),
            pl.BlockSpec((None, TOP_K, MOE_TM), lambda i, *_: (i, 0, 0)),
            pl.BlockSpec((MOE_TM, D_MODEL), lambda i, *_: (i, 0)),
        ],
        out_specs=(pl.BlockSpec(memory_space=pl.ANY), pl.BlockSpec(memory_space=pl.ANY)),
        scratch_shapes=[
            pltpu.VMEM((MOE_SLOTS, D_MODEL), BF16),
            pltpu.VMEM((MOE_SLOTS, HEAD), F32),
            pltpu.VMEM((MOE_SUB, D_MODEL), BF16),
            pltpu.VMEM((MOE_SUB, HEAD), F32),
            pltpu.SemaphoreType.DMA,
        ],
    )
    return pl.pallas_call(
        _dispatch_kernel,
        out_shape=(jax.ShapeDtypeStruct((n_rows, D_MODEL), BF16), jax.ShapeDtypeStruct((n_rows, HEAD), F32)),
        grid_spec=grid_spec,
        compiler_params=_cparams(("arbitrary",)),
        name="moe_dispatch",
    )(layout["seg_start"], layout["seg_gran"], layout["loc_off"], layout["tail_row"], slot_t, gate_t, hn)


def _moe_kernel(blk_e_ref, blk_nsub_ref, n_used_ref, xs_ref, gate_ref, wg_ref, wl_ref, bg_ref, bl_ref,
                wd_ref, bd_ref, yb_ref, wg_s, wl_s, wd_s, act_s, y_s, pend_ref, sem):
    i = pl.program_id(0)
    f = pl.program_id(1)
    last_step = jnp.logical_and(i == pl.num_programs(0) - 1, f == MOE_NF - 1)

    def y_copy(blk, sb):
        rows = pl.ds(pl.multiple_of(sb * MOE_SUB, MOE_SUB), MOE_SUB)
        dst = pl.ds(pl.multiple_of(blk * MOE_RB + sb * MOE_SUB, MOE_SUB), MOE_SUB)
        return pltpu.make_async_copy(y_s.at[rows], yb_ref.at[dst], sem)

    def drain():
        blk = pend_ref[1]

        def body(sb, c):
            y_copy(blk, sb).wait()
            return c

        lax.fori_loop(0, pend_ref[0], body, 0)
        pend_ref[0] = 0

    @pl.when(jnp.logical_and(i == 0, f == 0))
    def _():
        pend_ref[0] = 0
        pend_ref[1] = 0

    def over_granules(nsub, fn):
        def pair(sb, c):
            fn(sb * (2 * MOE_SUB), 2 * MOE_SUB)
            return c

        lax.fori_loop(0, nsub // 2, pair, 0)

        @pl.when(nsub % 2 == 1)
        def _():
            fn((nsub - 1) * MOE_SUB, MOE_SUB)

    @pl.when(i < n_used_ref[0])
    def _():
        wg_s[...] = wg_ref[...].astype(BF16)
        wl_s[...] = wl_ref[...].astype(BF16)
        wd_s[pl.ds(pl.multiple_of(f * MOE_TF, MOE_TF), MOE_TF), :] = wd_ref[...].astype(BF16)
        bg = bg_ref[...]
        bl = bl_ref[...]
        nsub = blk_nsub_ref[i]

        def up(start, size):
            rows = pl.ds(pl.multiple_of(start, MOE_SUB), size)
            x = xs_ref[rows, :]
            hg = jnp.dot(x, wg_s[...], preferred_element_type=F32) + bg
            hl = jnp.dot(x, wl_s[...], preferred_element_type=F32) + bl
            glu = jnp.minimum(hg, SWIGLU_LIMIT)
            lin = jnp.clip(hl, -SWIGLU_LIMIT, SWIGLU_LIMIT)
            act = (glu * _sigmoid(SWIGLU_ALPHA * glu) * (lin + 1.0)).astype(BF16)
            for k in range(MOE_NF):
                @pl.when(f == k)
                def _():
                    act_s[rows, k * MOE_TF:(k + 1) * MOE_TF] = act

        over_granules(nsub, up)

        @pl.when(f == MOE_NF - 1)
        def _():
            drain()

            def down(start, size):
                rows = pl.ds(pl.multiple_of(start, MOE_SUB), size)
                y = jnp.dot(act_s[rows, :], wd_s[...], preferred_element_type=F32) + bd_ref[...]
                y_s[rows, :] = y * gate_ref[rows, 0:1]

            over_granules(nsub, down)

            def send(sb, c):
                y_copy(i, sb).start()
                return c

            lax.fori_loop(0, nsub, send, 0)
            pend_ref[0] = nsub
            pend_ref[1] = i

    @pl.when(last_step)
    def _():
        drain()


def _moe_experts(blk_e, blk_nsub, n_used, xs, gate_rows, w_up, b_up, w_down, b_down):
    n_blk = xs.shape[0] // MOE_RB

    def live(i, n_used_ref):
        return jnp.minimum(i, n_used_ref[0] - 1)

    def ff(i, f, n_used_ref):
        return jnp.where(i < n_used_ref[0], f, MOE_NF - 1)

    def expert(i, be, nu):
        return be[live(i, nu)]

    grid_spec = pltpu.PrefetchScalarGridSpec(
        num_scalar_prefetch=3,
        grid=(n_blk, MOE_NF),
        in_specs=[
            pl.BlockSpec((MOE_RB, D_MODEL), lambda i, f, be, ns, nu: (live(i, nu), 0)),
            pl.BlockSpec((MOE_RB, HEAD), lambda i, f, be, ns, nu: (live(i, nu), 0)),
            pl.BlockSpec((None, D_MODEL, MOE_TF), lambda i, f, be, ns, nu: (expert(i, be, nu), 0, ff(i, f, nu))),
            pl.BlockSpec((None, D_MODEL, MOE_TF),
                         lambda i, f, be, ns, nu: (expert(i, be, nu), 0, ff(i, f, nu) + MOE_NF)),
            pl.BlockSpec((None, 1, MOE_TF), lambda i, f, be, ns, nu: (expert(i, be, nu), 0, ff(i, f, nu))),
            pl.BlockSpec((None, 1, MOE_TF),
                         lambda i, f, be, ns, nu: (expert(i, be, nu), 0, ff(i, f, nu) + MOE_NF)),
            pl.BlockSpec((None, MOE_TF, D_MODEL), lambda i, f, be, ns, nu: (expert(i, be, nu), ff(i, f, nu), 0)),
            pl.BlockSpec((None, 1, D_MODEL), lambda i, f, be, ns, nu: (expert(i, be, nu), 0, 0)),
        ],
        out_specs=pl.BlockSpec(memory_space=pl.ANY),
        scratch_shapes=[
            pltpu.VMEM((D_MODEL, MOE_TF), BF16),
            pltpu.VMEM((D_MODEL, MOE_TF), BF16),
            pltpu.VMEM((D_FF, D_MODEL), BF16),
            pltpu.VMEM((MOE_RB, D_FF), BF16),
            pltpu.VMEM((MOE_RB, D_MODEL), F32),
            pltpu.SMEM((2,), jnp.int32),
            pltpu.SemaphoreType.DMA,
        ],
    )
    b_up3 = b_up.reshape(N_EXPERTS, 1, 2 * D_FF)
    return pl.pallas_call(
        _moe_kernel,
        out_shape=jax.ShapeDtypeStruct(xs.shape, F32),
        grid_spec=grid_spec,
        compiler_params=_cparams(("arbitrary", "arbitrary")),
        name="moe_experts",
    )(blk_e, blk_nsub, n_used, xs, gate_rows, w_up, w_up, b_up3, b_up3, w_down,
      b_down.reshape(N_EXPERTS, 1, D_MODEL))


def _combine_kernel(seg_start_ref, seg_gran_ref, loc_off_ref, slot_ref, h_ref, yb_ref, y_ref, rows_scr, sem, *,
                    tile0):
    i = pl.program_id(0)

    @pl.when(i == 0)
    def _():
        rows_scr[...] = jnp.zeros_like(rows_scr)

    layout_refs = (seg_start_ref, seg_gran_ref, loc_off_ref)
    src_of = lambda loc, glob: yb_ref.at[pl.ds(glob, MOE_GRAN)]
    dst_of = lambda loc, glob: rows_scr.at[pl.ds(loc, MOE_GRAN)]
    _segment_copies(layout_refs, tile0 + i, src_of, dst_of, sem, start=True)
    _segment_copies(layout_refs, tile0 + i, src_of, dst_of, sem, start=False)

    slot_vecs = [slot_ref[:, k:k + 1] for k in range(TOP_K)]
    acc = h_ref[...]
    for c in range(MOE_SLOTS // MOE_CHUNK):
        pick = _slot_onehot(slot_vecs, c * MOE_CHUNK, (MOE_TM, MOE_CHUNK), 1)
        rows = rows_scr[c * MOE_CHUNK:(c + 1) * MOE_CHUNK, :]
        hi = rows.astype(BF16)
        lo = (rows - hi.astype(F32)).astype(BF16)
        acc = acc + (jnp.dot(pick, hi, preferred_element_type=F32)
                     + jnp.dot(pick, lo, preferred_element_type=F32))
    y_ref[...] = acc


def _combine(layout, slot, h, yb, tile0):
    m = h.shape[0]
    grid_spec = pltpu.PrefetchScalarGridSpec(
        num_scalar_prefetch=3,
        grid=(m // MOE_TM,),
        in_specs=[
            pl.BlockSpec((None, MOE_TM, TOP_K), lambda i, *_: (i + tile0, 0, 0)),
            pl.BlockSpec((MOE_TM, D_MODEL), lambda i, *_: (i, 0)),
            pl.BlockSpec(memory_space=pl.ANY),
        ],
        out_specs=pl.BlockSpec((MOE_TM, D_MODEL), lambda i, *_: (i, 0)),
        scratch_shapes=[pltpu.VMEM((MOE_SLOTS, D_MODEL), F32), pltpu.SemaphoreType.DMA],
    )
    return pl.pallas_call(
        functools.partial(_combine_kernel, tile0=tile0),
        out_shape=jax.ShapeDtypeStruct((m, D_MODEL), F32),
        grid_spec=grid_spec,
        compiler_params=_cparams(("arbitrary",)),
        name="moe_combine",
    )(layout["seg_start"], layout["seg_gran"], layout["loc_off"], slot, h, yb)


def kernel(x_prompt, x_sample, cache_kv_w128, cache_kv_w512, cache_kv_w2048, state_hgrn, rel_bias, attn_norm,
           w_in, q_norm, k_norm, lb_raw, g_norm, w_pa, w_pb, w_o, ffn_norm, w_router, b_router, w_up, b_up,
           w_down, b_down):
    assert attn_norm.shape[0] == 1, "single-layer stack"
    bsz, seq, _ = x_prompt.shape
    nseq, t_new, _ = x_sample.shape
    n_p, n_s = bsz * seq, nseq * t_new
    lb = jax.nn.softmax(lb_raw.astype(F32), axis=0)[0]
    wpa, wpb, wo = w_pa[0].astype(BF16), w_pb[0].astype(BF16), w_o[0].astype(BF16)
    wr = jnp.pad(w_router[0].astype(F32), ((0, 0), (0, RT_LANES - N_EXPERTS)))
    wr_hi = wr.astype(BF16)
    wr_lo = (wr - wr_hi.astype(F32)).astype(BF16)
    br = jnp.pad(b_router[0].astype(F32), (0, RT_LANES - N_EXPERTS)).reshape(1, RT_LANES)
    caches = (cache_kv_w128[0], cache_kv_w512[0], cache_kv_w2048[0])

    xp = x_prompt.reshape(n_p, D_MODEL)
    xs = x_sample.reshape(n_s, D_MODEL)

    proj_p = _inproj(xp, attn_norm[0], w_in[0])
    qn_p, *kv_p = _qkv(proj_p, q_norm[0], k_norm[0])
    oa_p = _attn_prompt(qn_p, kv_p, jnp.stack([_prompt_bias(rel_bias, g) for g in range(3)]), bsz, seq)
    ob_p, st_p = _hgrn_prompt(proj_p, lb, g_norm[0], bsz, seq)
    mixed_p = _mix(oa_p, ob_p, proj_p, wpa, wpb)
    h_p, hn_p, lg_p = _resid(xp, mixed_p, wo, ffn_norm[0], wr_hi, wr_lo, br)

    proj_s = _inproj(xs, attn_norm[0], w_in[0])
    qn_s, *kv_s = _qkv(proj_s, q_norm[0], k_norm[0])
    oa_s = _attn_sample(qn_s, kv_s, caches, rel_bias, t_new)
    ob_s, st_s = _hgrn_sample(proj_s, lb, g_norm[0], state_hgrn[0], t_new)
    mixed_s = _mix(oa_s, ob_s, proj_s, wpa, wpb)
    h_s, hn_s, lg_s = _resid(xs, mixed_s, wo, ffn_norm[0], wr_hi, wr_lo, br)

    logits = jnp.concatenate([lg_p, lg_s], axis=0)[:, :N_EXPERTS]
    assert n_p % MOE_TM == 0 and n_s % MOE_TM == 0, "each pass must fill whole MoE token tiles"
    blk_e, blk_nsub, n_used, layout, slot, gate, n_blk = _route(logits)
    hn_all = jnp.concatenate([hn_p, hn_s], axis=0)
    xs_sorted, gate_rows = _dispatch(layout, slot, gate, hn_all, n_blk * MOE_RB)
    yb = _moe_experts(blk_e, blk_nsub, n_used, xs_sorted, gate_rows, w_up[0], b_up[0], w_down[0], b_down[0])
    y_p = _combine(layout, slot, h_p, yb, 0)
    y_s = _combine(layout, slot, h_s, yb, n_p // MOE_TM)

    def prompt_rows(kv, win):
        rows = kv.reshape(bsz, seq, 2, A_HPG, HEAD)[:, seq - min(win, seq):]
        return rows[None]

    kvp = [prompt_rows(kv_p[g], A_GROUPS[g][0]) for g in range(3)]
    kvs = [kv_s[g].reshape(1, nseq, t_new, 2, A_HPG, HEAD) for g in range(3)]
    return (y_p.reshape(bsz, seq, D_MODEL), y_s.reshape(nseq, t_new, D_MODEL),
            kvp[0], kvp[1], kvp[2], st_p[None],
            kvs[0], kvs[1], kvs[2], st_s[None])
```

```python
import functools

import numpy as np
import jax
import jax.numpy as jnp
from jax import lax
from jax.experimental import pallas as pl
from jax.experimental.pallas import tpu as pltpu

F32 = jnp.float32
BF16 = jnp.bfloat16

D_MODEL = 2048
HEAD = 128
A_GROUPS = ((128, 1), (512, 4), (2048, 16))
A_HPG = 4
A_KEYS = 128
A_WIDTH = 3 * A_HPG * HEAD
A_OUT = A_HPG * HEAD
B_HEADS = 8
B_WIDTH = B_HEADS * HEAD
B_CHUNK = 32
REL_BUCKETS = 32
REL_MAX_DIST = 2048
N_EXPERTS = 32
TOP_K = 4
D_FF = 2048
SWIGLU_ALPHA = 1.702
SWIGLU_LIMIT = 7.0
NORM_EPS = 1e-6
NEG_INF = -1e30
ATTN_SCALE = HEAD ** -0.5
IN_WIDTH = 3 * A_WIDTH + 4 * B_WIDTH + 2 * D_MODEL
OFF_Q, OFF_K, OFF_V = 0, A_WIDTH, 2 * A_WIDTH
OFF_QB = 3 * A_WIDTH
OFF_FB, OFF_IB, OFF_OB = OFF_QB + B_WIDTH, OFF_QB + 2 * B_WIDTH, OFF_QB + 3 * B_WIDTH
OFF_GA = OFF_QB + 4 * B_WIDTH
OFF_GB = OFF_GA + D_MODEL

V7X_VMEM_LIMIT = 56 * 1024 * 1024
SUBLANES = 8

NT_DIMS = (((1,), (1,)), ((), ()))


def _sigmoid(x):
    return 1.0 / (1.0 + jnp.exp(-x))


def _cparams(sem):
    return pltpu.CompilerParams(dimension_semantics=sem, vmem_limit_bytes=V7X_VMEM_LIMIT)


NORM_TM = 512
PROJ_TM = 1024
PROJ_TN = 1280


def _rmsnorm_kernel(x_ref, nw_ref, o_ref):
    x = x_ref[...]
    ms = jnp.mean(x * x, axis=-1, keepdims=True)
    o_ref[...] = (x * lax.rsqrt(ms + NORM_EPS) * nw_ref[...]).astype(BF16)


def _inproj_kernel(xn_ref, w_ref, o_ref, wbf_ref):
    @pl.when(pl.program_id(1) == 0)
    def _():
        wbf_ref[...] = w_ref[...].astype(BF16)

    o_ref[...] = jnp.dot(xn_ref[...], wbf_ref[...], preferred_element_type=F32)


def _inproj(x, norm_w, w_in):
    m = x.shape[0]
    xn = pl.pallas_call(
        _rmsnorm_kernel,
        out_shape=jax.ShapeDtypeStruct((m, D_MODEL), BF16),
        grid=(m // NORM_TM,),
        in_specs=[pl.BlockSpec((NORM_TM, D_MODEL), lambda i: (i, 0)), pl.BlockSpec((1, D_MODEL), lambda i: (0, 0))],
        out_specs=pl.BlockSpec((NORM_TM, D_MODEL), lambda i: (i, 0)),
        compiler_params=_cparams(("arbitrary",)),
        name="attn_rmsnorm",
    )(x, norm_w.reshape(1, D_MODEL))
    tm = min(PROJ_TM, m)
    return pl.pallas_call(
        _inproj_kernel,
        out_shape=jax.ShapeDtypeStruct((m, IN_WIDTH), F32),
        grid=(IN_WIDTH // PROJ_TN, m // tm),
        in_specs=[
            pl.BlockSpec((tm, D_MODEL), lambda j, i: (i, 0)),
            pl.BlockSpec((D_MODEL, PROJ_TN), lambda j, i: (0, j)),
        ],
        out_specs=pl.BlockSpec((tm, PROJ_TN), lambda j, i: (i, j)),
        scratch_shapes=[pltpu.VMEM((D_MODEL, PROJ_TN), BF16)],
        compiler_params=_cparams(("arbitrary", "arbitrary")),
        name="inproj",
    )(xn, w_in)


QKV_TM = 512


def _qkv_kernel(q_ref, k_ref, v_ref, qw_ref, kw_ref, qn_ref, kv0_ref, kv1_ref, kv2_ref):
    kv_refs = (kv0_ref, kv1_ref, kv2_ref)
    qw = qw_ref[...]
    kw = kw_ref[...]
    for h in range(3 * A_HPG):
        sl = slice(h * HEAD, (h + 1) * HEAD)
        q = q_ref[:, sl]
        qn_ref[:, sl] = q * lax.rsqrt(jnp.mean(q * q, axis=-1, keepdims=True) + NORM_EPS) * qw
        k = k_ref[:, sl]
        kn = k * lax.rsqrt(jnp.mean(k * k, axis=-1, keepdims=True) + NORM_EPS) * kw
        g, hh = divmod(h, A_HPG)
        kv_refs[g][:, hh * HEAD:(hh + 1) * HEAD] = kn
        kv_refs[g][:, A_OUT + hh * HEAD:A_OUT + (hh + 1) * HEAD] = v_ref[:, sl]


def _qkv(proj, q_norm, k_norm):
    m = proj.shape[0]
    kv_shape = jax.ShapeDtypeStruct((m, 2 * A_OUT), F32)
    return pl.pallas_call(
        _qkv_kernel,
        out_shape=(jax.ShapeDtypeStruct((m, A_WIDTH), F32), kv_shape, kv_shape, kv_shape),
        grid=(m // QKV_TM,),
        in_specs=[
            pl.BlockSpec((QKV_TM, A_WIDTH), lambda i: (i, OFF_Q // A_WIDTH)),
            pl.BlockSpec((QKV_TM, A_WIDTH), lambda i: (i, OFF_K // A_WIDTH)),
            pl.BlockSpec((QKV_TM, A_WIDTH), lambda i: (i, OFF_V // A_WIDTH)),
            pl.BlockSpec((1, HEAD), lambda i: (0, 0)),
            pl.BlockSpec((1, HEAD), lambda i: (0, 0)),
        ],
        out_specs=(
            pl.BlockSpec((QKV_TM, A_WIDTH), lambda i: (i, 0)),
            pl.BlockSpec((QKV_TM, 2 * A_OUT), lambda i: (i, 0)),
            pl.BlockSpec((QKV_TM, 2 * A_OUT), lambda i: (i, 0)),
            pl.BlockSpec((QKV_TM, 2 * A_OUT), lambda i: (i, 0)),
        ),
        compiler_params=_cparams(("arbitrary",)),
        name="qkv_norm",
    )(proj, proj, proj, q_norm.reshape(1, HEAD), k_norm.reshape(1, HEAD))


def _t5_bucket(dist):
    dist = np.asarray(dist, np.int64)
    max_exact = REL_BUCKETS // 2
    ratio = np.log(np.maximum(dist, 1) / max_exact) / np.log(REL_MAX_DIST / max_exact)
    large = np.minimum(max_exact + (ratio * (REL_BUCKETS - max_exact)).astype(np.int64), REL_BUCKETS - 1)
    return np.where(dist < max_exact, dist, large).astype(np.int32)


def _bias_lookup(tab, bucket, valid):
    bucket = jnp.asarray(np.where(valid, bucket, -1).astype(np.int32))[None]
    out = jnp.full((tab.shape[1],) + bucket.shape[1:], NEG_INF, F32)
    for b in range(REL_BUCKETS):
        out = jnp.where(bucket == b, tab[b].reshape((-1,) + (1,) * (bucket.ndim - 1)), out)
    return out


def _prompt_bias(rel_bias, g):
    _, dil = A_GROUPS[g]
    n = A_KEYS
    j = n + np.arange(n)[:, None] - np.arange(2 * n)[None, :]
    valid = (j >= 0) & (j < n)
    tab = rel_bias[:, g * A_HPG:(g + 1) * A_HPG].astype(F32)
    return _bias_lookup(tab, _t5_bucket(np.clip(j, 0, n - 1) * dil), valid)


ATT_MIX_ROWS = 256


def _attn_prompt_kernel(q0_ref, q1_ref, q2_ref, k0_ref, v0_ref, k1_ref, v1_ref, k2_ref, v2_ref, bias_ref,
                        o_ref, og_scr, lg_scr, *, seq):
    n = A_KEYS
    h = pl.program_id(1)
    q_refs = (q0_ref, q1_ref, q2_ref)
    k_refs = (k0_ref, k1_ref, k2_ref)
    v_refs = (v0_ref, v1_ref, v2_ref)
    for g, (win, dil) in enumerate(A_GROUPS):
        bias = bias_ref[g, h]
        for r in range(dil):
            for i in range(seq // win):
                def rows(blk, r=r, dil=dil):
                    if dil == 1:
                        return pl.ds(blk * n, n)
                    return pl.ds(r + blk * n * dil, n, stride=dil)

                q = q_refs[g][rows(i), :].astype(BF16)
                if i == 0:
                    kk = k_refs[g][rows(i), :].astype(BF16)
                    vv = v_refs[g][rows(i), :].astype(BF16)
                    b_tile = bias[:, n:]
                else:
                    kk = jnp.concatenate([k_refs[g][rows(i - 1), :], k_refs[g][rows(i), :]], axis=0).astype(BF16)
                    vv = jnp.concatenate([v_refs[g][rows(i - 1), :], v_refs[g][rows(i), :]], axis=0).astype(BF16)
                    b_tile = bias
                s = lax.dot_general(q, kk, NT_DIMS, preferred_element_type=F32) * ATTN_SCALE + b_tile
                mx = jnp.max(s, axis=-1, keepdims=True)
                p = jnp.exp(s - mx)
                den = jnp.sum(p, axis=-1, keepdims=True)
                og_scr[g, rows(i), :] = jnp.dot(p.astype(BF16), vv, preferred_element_type=F32) / den
                lg_scr[g, rows(i), :] = jnp.broadcast_to(mx + jnp.log(den), (n, HEAD))
    for c in range(seq // ATT_MIX_ROWS):
        sl = slice(c * ATT_MIX_ROWS, (c + 1) * ATT_MIX_ROWS)
        la, lb_, lc = lg_scr[0, sl, :], lg_scr[1, sl, :], lg_scr[2, sl, :]
        top = jnp.maximum(jnp.maximum(la, lb_), lc)
        ea, eb, ec = jnp.exp(la - top), jnp.exp(lb_ - top), jnp.exp(lc - top)
        o_ref[sl, :] = (ea * og_scr[0, sl, :] + eb * og_scr[1, sl, :] + ec * og_scr[2, sl, :]) / (ea + eb + ec)


def _attn_prompt(qn, kvs, bias, bsz, seq):
    m = qn.shape[0]
    col = lambda fn: pl.BlockSpec((seq, HEAD), lambda b, h: (b, fn(h)))
    in_specs = [col(lambda h, g=g: g * A_HPG + h) for g in range(3)]
    for _ in range(3):
        in_specs += [col(lambda h: h), col(lambda h: A_HPG + h)]
    in_specs.append(pl.BlockSpec(bias.shape, lambda b, h: (0, 0, 0, 0)))
    return pl.pallas_call(
        functools.partial(_attn_prompt_kernel, seq=seq),
        out_shape=jax.ShapeDtypeStruct((m, A_OUT), F32),
        grid=(bsz, A_HPG),
        in_specs=in_specs,
        out_specs=col(lambda h: h),
        scratch_shapes=[pltpu.VMEM((3, seq, HEAD), F32), pltpu.VMEM((3, seq, HEAD), F32)],
        compiler_params=_cparams(("arbitrary", "arbitrary")),
        name="attn_prompt",
    )(qn, qn, qn, kvs[0], kvs[0], kvs[1], kvs[1], kvs[2], kvs[2], bias)


def _hgrn_gates(q, z, lb):
    qs = q * _sigmoid(q)
    logf = jnp.log(lb + (1.0 - lb) * _sigmoid(z))
    kb = (1.0 - lb) * _sigmoid(-z)
    return qs, kb, logf


def _hgrn_out(o, gate, gn):
    on = o * lax.rsqrt(jnp.mean(o * o, axis=-1, keepdims=True) + NORM_EPS) * gn
    return (on * (gate * _sigmoid(gate))).astype(BF16)


def _split3(x):
    hi = x.astype(BF16)
    r1 = x - hi.astype(F32)
    mid = r1.astype(BF16)
    lo = (r1 - mid.astype(F32)).astype(BF16)
    return hi, mid, lo


HG_TC = 256
HG_NC = HG_TC // B_CHUNK
HG_HPS = 4


def _hgrn_prompt_kernel(q_ref, f_ref, i_ref, g_ref, lb_ref, gn_ref, ob_ref, st_ref, s_scr):
    tb = pl.program_id(2)

    @pl.when(tb == 0)
    def _():
        s_scr[...] = jnp.zeros_like(s_scr)

    row = lax.broadcasted_iota(jnp.int32, (HG_TC, HG_TC), 0)
    col = lax.broadcasted_iota(jnp.int32, (HG_TC, HG_TC), 1)
    same_chunk = (row // B_CHUNK) == (col // B_CHUNK)
    causal = jnp.logical_and(same_chunk, col <= row)
    cum_mat = jnp.where(causal, 1.0, 0.0).astype(BF16)
    tcol = lax.broadcasted_iota(jnp.int32, (HEAD, HG_TC), 1) // B_CHUNK
    trow = lax.broadcasted_iota(jnp.int32, (HG_TC, HEAD), 0) // B_CHUNK
    gn = gn_ref[...]
    for h in range(HG_HPS):
        sl = slice(h * HEAD, (h + 1) * HEAD)
        qs, kb, logf = _hgrn_gates(q_ref[:, sl], f_ref[:, sl], lb_ref[:, sl])
        b3 = jnp.dot(cum_mat, jnp.concatenate(_split3(logf), axis=1), preferred_element_type=F32)
        b = b3[:, :HEAD] + b3[:, HEAD:2 * HEAD] + b3[:, 2 * HEAD:]
        bl_rows = [b[c * B_CHUNK + B_CHUNK - 1:(c + 1) * B_CHUNK, :] for c in range(HG_NC)]
        bl = jnp.concatenate([jnp.broadcast_to(r, (B_CHUNK, HEAD)) for r in bl_rows], axis=0)
        q_dec_f = qs * jnp.exp(b)
        q_dec = q_dec_f.astype(BF16)
        k_inv = (kb * jnp.exp(-b)).astype(BF16)
        k_end_t = jnp.transpose(kb * jnp.exp(bl - b))
        v = i_ref[:, sl].astype(BF16)
        att = lax.dot_general(q_dec, k_inv, NT_DIMS, preferred_element_type=F32)
        att = jnp.where(causal, att, 0.0).astype(BF16)
        o_intra = jnp.dot(att, v, preferred_element_type=F32)
        decay_t = jnp.exp(jnp.transpose(bl))
        ke_stack = jnp.concatenate([jnp.where(tcol == c, k_end_t, 0.0) for c in range(HG_NC)], axis=0)
        ds_all = jnp.dot(ke_stack.astype(BF16), v, preferred_element_type=F32)
        s = s_scr[h]
        starts = []
        for c in range(HG_NC):
            starts.append(s)
            s = decay_t[:, c * B_CHUNK:c * B_CHUNK + 1] * s + ds_all[c * HEAD:(c + 1) * HEAD, :]
        s_scr[h] = s
        q_bd = jnp.concatenate([jnp.where(trow == c, q_dec_f, 0.0) for c in range(HG_NC)], axis=1).astype(BF16)
        o_inter = jnp.dot(q_bd, jnp.concatenate(starts, axis=0).astype(BF16), preferred_element_type=F32)
        ob_ref[:, sl] = _hgrn_out(o_intra + o_inter, g_ref[:, sl], gn)

    @pl.when(tb == pl.num_programs(2) - 1)
    def _():
        st_ref[...] = s_scr[...]


def _hgrn_prompt(proj, lb, g_norm, bsz, seq):
    m = proj.shape[0]
    hw = HG_HPS * HEAD
    nt = seq // HG_TC
    rowblk = lambda b, hf, t: b * nt + t
    spec = lambda off: pl.BlockSpec((HG_TC, hw), lambda b, hf, t: (rowblk(b, hf, t), off // hw + hf))
    return pl.pallas_call(
        _hgrn_prompt_kernel,
        out_shape=(jax.ShapeDtypeStruct((m, B_WIDTH), BF16),
                   jax.ShapeDtypeStruct((bsz, B_HEADS, HEAD, HEAD), F32)),
        grid=(bsz, B_HEADS // HG_HPS, nt),
        in_specs=[
            spec(OFF_QB), spec(OFF_FB), spec(OFF_IB), spec(OFF_OB),
            pl.BlockSpec((1, hw), lambda b, hf, t: (0, hf)),
            pl.BlockSpec((1, HEAD), lambda b, hf, t: (0, 0)),
        ],
        out_specs=(
            pl.BlockSpec((HG_TC, hw), lambda b, hf, t: (rowblk(b, hf, t), hf)),
            pl.BlockSpec((None, HG_HPS, HEAD, HEAD), lambda b, hf, t: (b, hf, 0, 0)),
        ),
        scratch_shapes=[pltpu.VMEM((HG_HPS, HEAD, HEAD), F32)],
        compiler_params=_cparams(("arbitrary", "arbitrary", "arbitrary")),
        name="hgrn_prompt",
    )(proj, proj, proj, proj, lb.reshape(1, B_WIDTH), g_norm.reshape(1, HEAD))


HS_SEQ = 32


def _hgrn_sample_kernel(q_ref, f_ref, i_ref, g_ref, lb_ref, gn_ref, s0_ref, ob_ref, st_ref, *, t_new):
    rows = HS_SEQ * t_new
    qs, kb, logf = _hgrn_gates(q_ref[...], f_ref[...], lb_ref[...])
    t_of = lax.broadcasted_iota(jnp.int32, (rows, HEAD), 0) % t_new
    b = logf
    sh = 1
    while sh < t_new:
        b = b + jnp.where(t_of >= sh, pltpu.roll(b, sh, axis=0), 0.0)
        sh *= 2
    bl = b
    for back in range(1, t_new):
        bl = jnp.where(t_of == t_new - 1 - back, pltpu.roll(b, rows - back, axis=0), bl)
    q_dec = (qs * jnp.exp(b)).astype(BF16)
    k_inv = (kb * jnp.exp(-b)).astype(BF16)
    k_end_t = jnp.transpose(kb * jnp.exp(bl - b))
    decay_t = jnp.exp(jnp.transpose(bl))
    v = i_ref[...].astype(BF16)
    row = lax.broadcasted_iota(jnp.int32, (rows, rows), 0)
    col = lax.broadcasted_iota(jnp.int32, (rows, rows), 1)
    causal = jnp.logical_and(row // t_new == col // t_new, col <= row)
    att = lax.dot_general(q_dec, k_inv, NT_DIMS, preferred_element_type=F32)
    att = jnp.where(causal, att, 0.0).astype(BF16)
    o_intra = jnp.dot(att, v, preferred_element_type=F32)
    seq_of_col = lax.broadcasted_iota(jnp.int32, (HEAD, rows), 1) // t_new
    seq_of_row = lax.broadcasted_iota(jnp.int32, (rows, HEAD), 0) // t_new
    o_inter = jnp.zeros((rows, HEAD), F32)
    for s_i in range(HS_SEQ):
        s = s0_ref[s_i, 0]
        part = jnp.dot(q_dec, s.astype(BF16), preferred_element_type=F32)
        o_inter = jnp.where(seq_of_row == s_i, part, o_inter)
        ke = jnp.where(seq_of_col == s_i, k_end_t, 0.0).astype(BF16)
        ds = jnp.dot(ke, v, preferred_element_type=F32)
        st_ref[s_i, 0] = decay_t[:, s_i * t_new:s_i * t_new + 1] * s + ds
    ob_ref[...] = _hgrn_out(o_intra + o_inter, g_ref[...], gn_ref[...])


def _hgrn_sample(proj, lb, g_norm, s0, t_new):
    m = proj.shape[0]
    nseq = m // t_new
    rows = HS_SEQ * t_new
    spec = lambda off: pl.BlockSpec((rows, HEAD), lambda i, h: (i, off // HEAD + h))
    st_spec = pl.BlockSpec((HS_SEQ, 1, HEAD, HEAD), lambda i, h: (i, h, 0, 0))
    return pl.pallas_call(
        functools.partial(_hgrn_sample_kernel, t_new=t_new),
        out_shape=(jax.ShapeDtypeStruct((m, B_WIDTH), BF16),
                   jax.ShapeDtypeStruct((nseq, B_HEADS, HEAD, HEAD), F32)),
        grid=(nseq // HS_SEQ, B_HEADS),
        in_specs=[
            spec(OFF_QB), spec(OFF_FB), spec(OFF_IB), spec(OFF_OB),
            pl.BlockSpec((1, HEAD), lambda i, h: (0, h)),
            pl.BlockSpec((1, HEAD), lambda i, h: (0, 0)),
            st_spec,
        ],
        out_specs=(pl.BlockSpec((rows, HEAD), lambda i, h: (i, h)), st_spec),
        compiler_params=_cparams(("arbitrary", "arbitrary")),
        name="hgrn_sample",
    )(proj, proj, proj, proj, lb.reshape(1, B_WIDTH), g_norm.reshape(1, HEAD), s0)


SA_SEQ = 2
KV_ROWS = 2 * A_HPG


def _sample_bias(rel_bias, t_new):
    def tile(tab, dist, valid):
        bias = _bias_lookup(tab, _t5_bucket(np.clip(dist, 0, REL_MAX_DIST)), valid)
        bias = jnp.moveaxis(bias, 0, -1)
        bias = jnp.concatenate([bias, jnp.zeros(bias.shape[:-1] + (KV_ROWS - A_HPG,), F32)], axis=-1)
        return jnp.broadcast_to(bias[..., None], bias.shape + (HEAD,))

    t = np.arange(t_new)[:, None]
    r = np.arange(A_KEYS + t_new)[None, :]
    is_new = r >= A_KEYS
    dist = np.where(is_new, t - (r - A_KEYS), A_KEYS + t - r)
    valid = (dist >= 0) & (dist < A_KEYS)
    b0 = tile(rel_bias[:, 0:A_HPG].astype(F32), dist, valid)
    b12 = []
    for g in (1, 2):
        _, dil = A_GROUPS[g]
        m = np.arange(A_KEYS + 1)
        dist = dil * (A_KEYS - m)
        b12.append(tile(rel_bias[:, g * A_HPG:(g + 1) * A_HPG].astype(F32), dist, m >= 1))
    return b0, jnp.stack(b12)


def _attn_sample_kernel(q_ref, n0_ref, n1_ref, n2_ref, c0_ref, c1_ref, c2_ref, b0_ref, b12_ref,
                        o_ref, qt_scr, ns_scr, l_scr, *, t_new):
    nq = SA_SEQ * t_new
    new_refs = (n0_ref, n1_ref, n2_ref)
    cache_refs = (c0_ref, c1_ref, c2_ref)
    qt_scr[...] = jnp.zeros_like(qt_scr)
    for g in range(3):
        for j in range(nq):
            for h in range(A_HPG):
                qt_scr[g, j, h:h + 1, :] = (
                    q_ref[j:j + 1, g * A_OUT + h * HEAD:g * A_OUT + (h + 1) * HEAD] * ATTN_SCALE)
                ns_scr[g, j, h:h + 1, :] = new_refs[g][j:j + 1, h * HEAD:(h + 1) * HEAD]
                ns_scr[g, j, A_HPG + h:A_HPG + h + 1, :] = (
                    new_refs[g][j:j + 1, A_OUT + h * HEAD:A_OUT + (h + 1) * HEAD])

    row_id = lax.broadcasted_iota(jnp.int32, (nq, HEAD), 0)

    def per_query(j, out):
        s = j // t_new
        t = j - s * t_new
        row0 = s * A_KEYS
        stats = []
        for g in range(3):
            c_ref = cache_refs[g]
            qt = qt_scr[g, j]
            if g == 0:
                sub0 = 0
                bias_at = lambda m: b0_ref[t, m]
                new_tiles = [(A_KEYS + tp, ns_scr[0, s * t_new + tp]) for tp in range(t_new)]
            else:
                sub0 = pl.multiple_of(t * KV_ROWS, KV_ROWS)
                bias_at = lambda m, g=g: b12_ref[g - 1, m]
                new_tiles = [(A_KEYS, ns_scr[g, j])]

            def score(tile_kv, m):
                l = jnp.sum(tile_kv * qt, axis=-1, keepdims=True) + bias_at(m)
                return l + pltpu.roll(l, A_HPG, axis=0)

            lanes = [jnp.full((KV_ROWS, HEAD), NEG_INF, F32) for _ in range(4)]
            for m in range(A_KEYS):
                l = score(c_ref[row0 + m, pl.ds(sub0, KV_ROWS), :], m)
                l_scr[m] = l
                lanes[m % 4] = jnp.maximum(lanes[m % 4], l)
            mx = jnp.maximum(jnp.maximum(lanes[0], lanes[1]), jnp.maximum(lanes[2], lanes[3]))
            new_l = [score(tile_kv, m) for m, tile_kv in new_tiles]
            for l in new_l:
                mx = jnp.maximum(mx, l)

            zero = jnp.zeros((KV_ROWS, HEAD), F32)
            dens, accs = [zero] * 4, [zero] * 4
            for m in range(A_KEYS):
                p = jnp.exp(l_scr[m] - mx)
                dens[m % 4] = dens[m % 4] + p
                accs[m % 4] = accs[m % 4] + p * c_ref[row0 + m, pl.ds(sub0, KV_ROWS), :]
            den = (dens[0] + dens[1]) + (dens[2] + dens[3])
            acc = (accs[0] + accs[1]) + (accs[2] + accs[3])
            for l, (_, tile_kv) in zip(new_l, new_tiles):
                p = jnp.exp(l - mx)
                den = den + p
                acc = acc + p * tile_kv
            stats.append((mx + jnp.log(den), den, acc))
        top = jnp.maximum(jnp.maximum(stats[0][0], stats[1][0]), stats[2][0])
        e = [jnp.exp(st[0] - top) for st in stats]
        tot = e[0] + e[1] + e[2]
        mix = jnp.zeros((KV_ROWS, HEAD), F32)
        for g in range(3):
            mix = mix + e[g] / (tot * stats[g][1]) * stats[g][2]
        pieces = [jnp.broadcast_to(mix[A_HPG + h:A_HPG + h + 1, :], (nq, HEAD)) for h in range(A_HPG)]
        return tuple(jnp.where(row_id == j, pieces[h], out[h]) for h in range(A_HPG))

    init = tuple(jnp.zeros((nq, HEAD), F32) for _ in range(A_HPG))
    out = lax.fori_loop(0, nq, per_query, init)
    for h in range(A_HPG):
        o_ref[:, h * HEAD:(h + 1) * HEAD] = out[h]


def _attn_sample(qn, kv_new, caches, rel_bias, t_new):
    m = qn.shape[0]
    nseq = m // t_new
    nq = SA_SEQ * t_new
    assert nq % SUBLANES == 0, "the new-token rows of one grid step must fill whole sublane tiles"
    kvw = 2 * A_OUT
    views, specs = [], []
    for g, (win, dil) in enumerate(A_GROUPS):
        c = caches[g]
        assert c.shape[1] == win, "cache must hold exactly one window of positions"
        assert g == 0 or dil >= t_new, "each new token must sit in its own residue class"
        views.append(c.reshape(nseq * A_KEYS, dil * KV_ROWS, HEAD))
        specs.append(pl.BlockSpec((SA_SEQ * A_KEYS, min(dil, t_new) * KV_ROWS, HEAD), lambda i: (i, 0, 0)))
    b0, b12 = _sample_bias(rel_bias, t_new)
    new_spec = pl.BlockSpec((nq, kvw), lambda i: (i, 0))
    return pl.pallas_call(
        functools.partial(_attn_sample_kernel, t_new=t_new),
        out_shape=jax.ShapeDtypeStruct((m, A_OUT), F32),
        grid=(nseq // SA_SEQ,),
        in_specs=[
            pl.BlockSpec((nq, A_WIDTH), lambda i: (i, 0)),
            new_spec, new_spec, new_spec,
            specs[0], specs[1], specs[2],
            pl.BlockSpec(b0.shape, lambda i: (0, 0, 0, 0)),
            pl.BlockSpec(b12.shape, lambda i: (0, 0, 0, 0)),
        ],
        out_specs=pl.BlockSpec((nq, A_OUT), lambda i: (i, 0)),
        scratch_shapes=[
            pltpu.VMEM((3, nq, KV_ROWS, HEAD), F32),
            pltpu.VMEM((3, nq, KV_ROWS, HEAD), F32),
            pltpu.VMEM((A_KEYS, KV_ROWS, HEAD), F32),
        ],
        compiler_params=_cparams(("arbitrary",)),
        name="attn_sample",
    )(qn, kv_new[0], kv_new[1], kv_new[2], views[0], views[1], views[2], b0, b12)


MIX_TM = 512
MIX_TN = 512


def _mix_kernel(oa_ref, ob_ref, ga_ref, gb_ref, wpa_ref, wpb_ref, out_ref):
    pa = jnp.dot(oa_ref[...].astype(BF16), wpa_ref[...], preferred_element_type=F32)
    pb = jnp.dot(ob_ref[...], wpb_ref[...], preferred_element_type=F32)
    out_ref[...] = (_sigmoid(ga_ref[...]) * pa + _sigmoid(gb_ref[...]) * pb).astype(BF16)


def _mix(oa, ob, proj, w_pa, w_pb):
    m = proj.shape[0]
    return pl.pallas_call(
        _mix_kernel,
        out_shape=jax.ShapeDtypeStruct((m, D_MODEL), BF16),
        grid=(m // MIX_TM, D_MODEL // MIX_TN),
        in_specs=[
            pl.BlockSpec((MIX_TM, A_OUT), lambda i, j: (i, 0)),
            pl.BlockSpec((MIX_TM, B_WIDTH), lambda i, j: (i, 0)),
            pl.BlockSpec((MIX_TM, MIX_TN), lambda i, j: (i, OFF_GA // MIX_TN + j)),
            pl.BlockSpec((MIX_TM, MIX_TN), lambda i, j: (i, OFF_GB // MIX_TN + j)),
            pl.BlockSpec((A_OUT, MIX_TN), lambda i, j: (0, j)),
            pl.BlockSpec((B_WIDTH, MIX_TN), lambda i, j: (0, j)),
        ],
        out_specs=pl.BlockSpec((MIX_TM, MIX_TN), lambda i, j: (i, j)),
        compiler_params=_cparams(("arbitrary", "arbitrary")),
        name="mix",
    )(oa, ob, proj, proj, w_pa, w_pb)


RT_LANES = 128


def _resid_kernel(x_ref, mixed_ref, wo_ref, nw_ref, wr_hi_ref, wr_lo_ref, br_ref, h_ref, hn_ref, lg_ref):
    h = x_ref[...] + jnp.dot(mixed_ref[...], wo_ref[...], preferred_element_type=F32)
    h_ref[...] = h
    hn = h * lax.rsqrt(jnp.mean(h * h, axis=-1, keepdims=True) + NORM_EPS) * nw_ref[...]
    hi = hn.astype(BF16)
    hn_ref[...] = hi
    lo = (hn - hi.astype(F32)).astype(BF16)
    lg = (jnp.dot(hi, wr_hi_ref[...], preferred_element_type=F32)
          + jnp.dot(lo, wr_hi_ref[...], preferred_element_type=F32)
          + jnp.dot(hi, wr_lo_ref[...], preferred_element_type=F32))
    lg_ref[...] = lg + br_ref[...]


def _resid(x, mixed, w_o, ffn_norm, wr_hi, wr_lo, br):
    m = x.shape[0]
    full = lambda shape: pl.BlockSpec(shape, lambda i: (0, 0))
    return pl.pallas_call(
        _resid_kernel,
        out_shape=(jax.ShapeDtypeStruct((m, D_MODEL), F32),
                   jax.ShapeDtypeStruct((m, D_MODEL), BF16),
                   jax.ShapeDtypeStruct((m, RT_LANES), F32)),
        grid=(m // MIX_TM,),
        in_specs=[
            pl.BlockSpec((MIX_TM, D_MODEL), lambda i: (i, 0)),
            pl.BlockSpec((MIX_TM, D_MODEL), lambda i: (i, 0)),
            full((D_MODEL, D_MODEL)), full((1, D_MODEL)),
            full((D_MODEL, RT_LANES)), full((D_MODEL, RT_LANES)), full((1, RT_LANES)),
        ],
        out_specs=(
            pl.BlockSpec((MIX_TM, D_MODEL), lambda i: (i, 0)),
            pl.BlockSpec((MIX_TM, D_MODEL), lambda i: (i, 0)),
            pl.BlockSpec((MIX_TM, RT_LANES), lambda i: (i, 0)),
        ),
        compiler_params=_cparams(("arbitrary",)),
        name="resid_router",
    )(x, mixed, w_o, ffn_norm.reshape(1, D_MODEL), wr_hi, wr_lo, br)


MOE_RB = 1280
MOE_SUB = 256
MOE_TF = 256
MOE_NF = D_FF // MOE_TF


MOE_TM = 512
MOE_GRAN = 16
MOE_SLOTS = 2560
MOE_CHUNK = 256
assert MOE_SLOTS >= MOE_TM * TOP_K + N_EXPERTS * (MOE_GRAN - 1) and MOE_SLOTS % MOE_CHUNK == 0


def _route(logits):
    n_tok = logits.shape[0]
    nt = n_tok // MOE_TM
    top_val, top_idx = lax.top_k(logits, TOP_K)
    gate = jax.nn.softmax(top_val, axis=-1)
    na = MOE_TM * TOP_K
    e3 = top_idx.reshape(nt, na)
    hot = e3[..., None] == jnp.arange(N_EXPERTS)
    onehot = hot.astype(jnp.int32)
    lower = (lax.broadcasted_iota(jnp.int32, (na, na), 0) >= lax.broadcasted_iota(jnp.int32, (na, na), 1))
    incl = jnp.einsum('ij,tjk->tik', lower.astype(BF16), hot.astype(BF16),
                      preferred_element_type=F32).astype(jnp.int32)
    rank = jnp.sum((incl - onehot) * onehot, axis=2)
    seg = (incl[:, -1] + MOE_GRAN - 1) // MOE_GRAN * MOE_GRAN
    loc_off = jnp.cumsum(seg, axis=1) - seg
    rows_e = jnp.sum(seg, axis=0)
    padded_sub = (rows_e + MOE_SUB - 1) // MOE_SUB * MOE_SUB
    padded_rb = (rows_e + MOE_RB - 1) // MOE_RB * MOE_RB
    blk_end = jnp.cumsum(padded_rb) // MOE_RB
    row_start = jnp.cumsum(padded_rb) - padded_rb
    seg_start = row_start[None, :] + jnp.cumsum(seg, axis=0) - seg
    slot = jnp.sum(onehot * loc_off[:, None, :], axis=2) + rank
    max_rows = n_tok * TOP_K + nt * N_EXPERTS * (MOE_GRAN - 1)
    n_blk = max_rows // MOE_RB + N_EXPERTS
    blk_ids = jnp.arange(n_blk)
    blk_e = jnp.minimum(jnp.searchsorted(blk_end, blk_ids, side='right'), N_EXPERTS - 1).astype(jnp.int32)
    rows_left = padded_sub[blk_e] - (blk_ids * MOE_RB - row_start[blk_e])
    blk_nsub = jnp.clip(rows_left // MOE_SUB, 0, MOE_RB // MOE_SUB).astype(jnp.int32)
    n_used = blk_end[-1:].astype(jnp.int32)
    tail_row = jnp.where(rows_e > 0, row_start + padded_sub - MOE_SUB, -1).astype(jnp.int32)
    layout = dict(seg_start=seg_start.reshape(-1).astype(jnp.int32),
                  seg_gran=(seg // MOE_GRAN).reshape(-1).astype(jnp.int32),
                  loc_off=loc_off.reshape(-1).astype(jnp.int32),
                  tail_row=tail_row)
    slot = slot.reshape(nt, MOE_TM, TOP_K).astype(jnp.int32)
    return blk_e, blk_nsub, n_used, layout, slot, gate.reshape(nt, MOE_TM, TOP_K), n_blk


def _segment_copies(layout_refs, tile, src_of, dst_of, sem, start):
    seg_start_ref, seg_gran_ref, loc_off_ref = layout_refs

    for e in range(N_EXPERTS):
        idx = tile * N_EXPERTS + e
        loc = loc_off_ref[idx]
        glob = seg_start_ref[idx]

        def body(c, carry, loc=loc, glob=glob):
            copy = pltpu.make_async_copy(
                src_of(pl.multiple_of(loc + c * MOE_GRAN, MOE_GRAN), pl.multiple_of(glob + c * MOE_GRAN, MOE_GRAN)),
                dst_of(pl.multiple_of(loc + c * MOE_GRAN, MOE_GRAN), pl.multiple_of(glob + c * MOE_GRAN, MOE_GRAN)),
                sem)
            if start:
                copy.start()
            else:
                copy.wait()
            return carry

        lax.fori_loop(0, seg_gran_ref[idx], body, 0)


def _slot_onehot(slot_vecs, base, shape, axis):
    ids = base + lax.broadcasted_iota(jnp.int32, shape, axis)
    hit = jnp.zeros(shape, F32)
    for v in slot_vecs:
        hit = jnp.where(ids == v, 1.0, hit)
    return hit.astype(BF16)


def _dispatch_kernel(seg_start_ref, seg_gran_ref, loc_off_ref, tail_ref, slot_ref, gate_ref, x_ref,
                     xs_ref, gs_ref, sorted_scr, gsorted_scr, zero_scr, gzero_scr, sem):
    i = pl.program_id(0)

    @pl.when(i == 0)
    def _():
        zero_scr[...] = jnp.zeros_like(zero_scr)
        gzero_scr[...] = jnp.zeros_like(gzero_scr)

        def tail_copies(e):
            rows = pl.ds(pl.multiple_of(jnp.maximum(tail_ref[e], 0), MOE_SUB), MOE_SUB)
            return (pltpu.make_async_copy(zero_scr, xs_ref.at[rows], sem),
                    pltpu.make_async_copy(gzero_scr, gs_ref.at[rows], sem))

        for e in range(N_EXPERTS):
            @pl.when(tail_ref[e] >= 0)
            def _():
                for copy in tail_copies(e):
                    copy.start()
        for e in range(N_EXPERTS):
            @pl.when(tail_ref[e] >= 0)
            def _():
                for copy in tail_copies(e):
                    copy.wait()

    x = x_ref[...]
    slot_vecs = [slot_ref[k:k + 1, :] for k in range(TOP_K)]
    for c in range(MOE_SLOTS // MOE_CHUNK):
        chunk = slice(c * MOE_CHUNK, (c + 1) * MOE_CHUNK)
        ids = c * MOE_CHUNK + lax.broadcasted_iota(jnp.int32, (MOE_CHUNK, MOE_TM), 0)
        hit = jnp.zeros((MOE_CHUNK, MOE_TM), F32)
        gval = jnp.zeros((MOE_CHUNK, MOE_TM), F32)
        for k in range(TOP_K):
            match = ids == slot_vecs[k]
            hit = jnp.where(match, 1.0, hit)
            gval = jnp.where(match, gate_ref[k:k + 1, :], gval)
        sorted_scr[chunk, :] = jnp.dot(hit.astype(BF16), x,
                                       preferred_element_type=F32).astype(BF16)
        gsorted_scr[chunk, :] = jnp.broadcast_to(jnp.sum(gval, axis=1, keepdims=True), (MOE_CHUNK, HEAD))

    layout_refs = (seg_start_ref, seg_gran_ref, loc_off_ref)
    streams = (
        (lambda loc, glob: sorted_scr.at[pl.ds(loc, MOE_GRAN)], lambda loc, glob: xs_ref.at[pl.ds(glob, MOE_GRAN)]),
        (lambda loc, glob: gsorted_scr.at[pl.ds(loc, MOE_GRAN)], lambda loc, glob: gs_ref.at[pl.ds(glob, MOE_GRAN)]),
    )
    for start in (True, False):
        for src_of, dst_of in streams:
            _segment_copies(layout_refs, i, src_of, dst_of, sem, start=start)


def _dispatch(layout, slot, gate, hn, n_rows):
    n_tok = hn.shape[0]
    slot_t = jnp.swapaxes(slot, 1, 2)
    gate_t = jnp.swapaxes(gate, 1, 2)
    grid_spec = pltpu.PrefetchScalarGridSpec(
        num_scalar_prefetch=4,
        grid=(n_tok // MOE_TM,),
        in_specs=[
            pl.BlockSpec((None, TOP_K, MOE_TM), lambda i, *_: (i, 0, 0)),
            pl.BlockSpec((None, TOP_K, MOE_TM), lambda i, *_: (i, 0, 0)),
            pl.BlockSpec((MOE_TM, D_MODEL), lambda i, *_: (i, 0)),
        ],
        out_specs=(pl.BlockSpec(memory_space=pl.ANY), pl.BlockSpec(memory_space=pl.ANY)),
        scratch_shapes=[
            pltpu.VMEM((MOE_SLOTS, D_MODEL), BF16),
            pltpu.VMEM((MOE_SLOTS, HEAD), F32),
            pltpu.VMEM((MOE_SUB, D_MODEL), BF16),
            pltpu.VMEM((MOE_SUB, HEAD), F32),
            pltpu.SemaphoreType.DMA,
        ],
    )
    return pl.pallas_call(
        _dispatch_kernel,
        out_shape=(jax.ShapeDtypeStruct((n_rows, D_MODEL), BF16), jax.ShapeDtypeStruct((n_rows, HEAD), F32)),
        grid_spec=grid_spec,
        compiler_params=_cparams(("arbitrary",)),
        name="moe_dispatch",
    )(layout["seg_start"], layout["seg_gran"], layout["loc_off"], layout["tail_row"], slot_t, gate_t, hn)


def _moe_kernel(blk_e_ref, blk_nsub_ref, n_used_ref, xs_ref, gate_ref, wg_ref, wl_ref, bg_ref, bl_ref,
                wd_ref, bd_ref, yb_ref, wu_s, wd_s, act_s, y_s, pend_ref, sem):
    i = pl.program_id(0)
    f = pl.program_id(1)
    last_step = jnp.logical_and(i == pl.num_programs(0) - 1, f == MOE_NF - 1)

    def y_copy(blk, sb):
        rows = pl.ds(pl.multiple_of(sb * MOE_SUB, MOE_SUB), MOE_SUB)
        dst = pl.ds(pl.multiple_of(blk * MOE_RB + sb * MOE_SUB, MOE_SUB), MOE_SUB)
        return pltpu.make_async_copy(y_s.at[rows], yb_ref.at[dst], sem)

    def drain():
        blk = pend_ref[1]

        def body(sb, c):
            y_copy(blk, sb).wait()
            return c

        lax.fori_loop(0, pend_ref[0], body, 0)
        pend_ref[0] = 0

    @pl.when(jnp.logical_and(i == 0, f == 0))
    def _():
        pend_ref[0] = 0
        pend_ref[1] = 0

    def over_granules(nsub, fn):
        def pair(sb, c):
            fn(sb * (2 * MOE_SUB), 2 * MOE_SUB)
            return c

        lax.fori_loop(0, nsub // 2, pair, 0)

        @pl.when(nsub % 2 == 1)
        def _():
            fn((nsub - 1) * MOE_SUB, MOE_SUB)

    @pl.when(i < n_used_ref[0])
    def _():
        wu_s[:, :MOE_TF] = wg_ref[...].astype(BF16)
        wu_s[:, MOE_TF:] = wl_ref[...].astype(BF16)
        wd_s[pl.ds(pl.multiple_of(f * MOE_TF, MOE_TF), MOE_TF), :] = wd_ref[...].astype(BF16)
        b_up = jnp.concatenate([bg_ref[...], bl_ref[...]], axis=1)
        nsub = blk_nsub_ref[i]

        def up(start, size):
            rows = pl.ds(pl.multiple_of(start, MOE_SUB), size)
            hh = jnp.dot(xs_ref[rows, :], wu_s[...], preferred_element_type=F32) + b_up
            glu = jnp.minimum(hh[:, :MOE_TF], SWIGLU_LIMIT)
            lin = jnp.clip(hh[:, MOE_TF:], -SWIGLU_LIMIT, SWIGLU_LIMIT)
            act_s[f, rows, :] = (glu * _sigmoid(SWIGLU_ALPHA * glu) * (lin + 1.0)).astype(BF16)

        over_granules(nsub, up)

        @pl.when(f == MOE_NF - 1)
        def _():
            drain()

            def down(start, size):
                rows = pl.ds(pl.multiple_of(start, MOE_SUB), size)
                act = jnp.concatenate([act_s[k, rows, :] for k in range(MOE_NF)], axis=1)
                y = jnp.dot(act, wd_s[...], preferred_element_type=F32) + bd_ref[...]
                y_s[rows, :] = y * gate_ref[rows, 0:1]

            over_granules(nsub, down)

            def send(sb, c):
                y_copy(i, sb).start()
                return c

            lax.fori_loop(0, nsub, send, 0)
            pend_ref[0] = nsub
            pend_ref[1] = i

    @pl.when(last_step)
    def _():
        drain()


def _moe_experts(blk_e, blk_nsub, n_used, xs, gate_rows, w_up, b_up, w_down, b_down):
    n_blk = xs.shape[0] // MOE_RB

    def live(i, n_used_ref):
        return jnp.minimum(i, n_used_ref[0] - 1)

    def ff(i, f, n_used_ref):
        return jnp.where(i < n_used_ref[0], f, MOE_NF - 1)

    def expert(i, be, nu):
        return be[live(i, nu)]

    grid_spec = pltpu.PrefetchScalarGridSpec(
        num_scalar_prefetch=3,
        grid=(n_blk, MOE_NF),
        in_specs=[
            pl.BlockSpec((MOE_RB, D_MODEL), lambda i, f, be, ns, nu: (live(i, nu), 0)),
            pl.BlockSpec((MOE_RB, HEAD), lambda i, f, be, ns, nu: (live(i, nu), 0)),
            pl.BlockSpec((None, D_MODEL, MOE_TF), lambda i, f, be, ns, nu: (expert(i, be, nu), 0, ff(i, f, nu))),
            pl.BlockSpec((None, D_MODEL, MOE_TF),
                         lambda i, f, be, ns, nu: (expert(i, be, nu), 0, ff(i, f, nu) + MOE_NF)),
            pl.BlockSpec((None, 1, MOE_TF), lambda i, f, be, ns, nu: (expert(i, be, nu), 0, ff(i, f, nu))),
            pl.BlockSpec((None, 1, MOE_TF),
                         lambda i, f, be, ns, nu: (expert(i, be, nu), 0, ff(i, f, nu) + MOE_NF)),
            pl.BlockSpec((None, MOE_TF, D_MODEL), lambda i, f, be, ns, nu: (expert(i, be, nu), ff(i, f, nu), 0)),
            pl.BlockSpec((None, 1, D_MODEL), lambda i, f, be, ns, nu: (expert(i, be, nu), 0, 0)),
        ],
        out_specs=pl.BlockSpec(memory_space=pl.ANY),
        scratch_shapes=[
            pltpu.VMEM((D_MODEL, 2 * MOE_TF), BF16),
            pltpu.VMEM((D_FF, D_MODEL), BF16),
            pltpu.VMEM((MOE_NF, MOE_RB, MOE_TF), BF16),
            pltpu.VMEM((MOE_RB, D_MODEL), F32),
            pltpu.SMEM((2,), jnp.int32),
            pltpu.SemaphoreType.DMA,
        ],
    )
    b_up3 = b_up.reshape(N_EXPERTS, 1, 2 * D_FF)
    return pl.pallas_call(
        _moe_kernel,
        out_shape=jax.ShapeDtypeStruct(xs.shape, F32),
        grid_spec=grid_spec,
        compiler_params=_cparams(("arbitrary", "arbitrary")),
        name="moe_experts",
    )(blk_e, blk_nsub, n_used, xs, gate_rows, w_up, w_up, b_up3, b_up3, w_down,
      b_down.reshape(N_EXPERTS, 1, D_MODEL))


def _combine_kernel(seg_start_ref, seg_gran_ref, loc_off_ref, slot_ref, h_ref, yb_ref, y_ref, rows_scr, sem, *,
                    tile0):
    i = pl.program_id(0)

    @pl.when(i == 0)
    def _():
        rows_scr[...] = jnp.zeros_like(rows_scr)

    layout_refs = (seg_start_ref, seg_gran_ref, loc_off_ref)
    src_of = lambda loc, glob: yb_ref.at[pl.ds(glob, MOE_GRAN)]
    dst_of = lambda loc, glob: rows_scr.at[pl.ds(loc, MOE_GRAN)]
    _segment_copies(layout_refs, tile0 + i, src_of, dst_of, sem, start=True)
    _segment_copies(layout_refs, tile0 + i, src_of, dst_of, sem, start=False)

    slot_vecs = [slot_ref[:, k:k + 1] for k in range(TOP_K)]
    acc = h_ref[...]
    for c in range(MOE_SLOTS // MOE_CHUNK):
        pick = _slot_onehot(slot_vecs, c * MOE_CHUNK, (MOE_TM, MOE_CHUNK), 1)
        rows = rows_scr[c * MOE_CHUNK:(c + 1) * MOE_CHUNK, :]
        hi = rows.astype(BF16)
        lo = (rows - hi.astype(F32)).astype(BF16)
        acc = acc + (jnp.dot(pick, hi, preferred_element_type=F32)
                     + jnp.dot(pick, lo, preferred_element_type=F32))
    y_ref[...] = acc


def _combine(layout, slot, h, yb, tile0):
    m = h.shape[0]
    grid_spec = pltpu.PrefetchScalarGridSpec(
        num_scalar_prefetch=3,
        grid=(m // MOE_TM,),
        in_specs=[
            pl.BlockSpec((None, MOE_TM, TOP_K), lambda i, *_: (i + tile0, 0, 0)),
            pl.BlockSpec((MOE_TM, D_MODEL), lambda i, *_: (i, 0)),
            pl.BlockSpec(memory_space=pl.ANY),
        ],
        out_specs=pl.BlockSpec((MOE_TM, D_MODEL), lambda i, *_: (i, 0)),
        scratch_shapes=[pltpu.VMEM((MOE_SLOTS, D_MODEL), F32), pltpu.SemaphoreType.DMA],
    )
    return pl.pallas_call(
        functools.partial(_combine_kernel, tile0=tile0),
        out_shape=jax.ShapeDtypeStruct((m, D_MODEL), F32),
        grid_spec=grid_spec,
        compiler_params=_cparams(("arbitrary",)),
        name="moe_combine",
    )(layout["seg_start"], layout["seg_gran"], layout["loc_off"], slot, h, yb)


def kernel(x_prompt, x_sample, cache_kv_w128, cache_kv_w512, cache_kv_w2048, state_hgrn, rel_bias, attn_norm,
           w_in, q_norm, k_norm, lb_raw, g_norm, w_pa, w_pb, w_o, ffn_norm, w_router, b_router, w_up, b_up,
           w_down, b_down):
    assert attn_norm.shape[0] == 1, "single-layer stack"
    bsz, seq, _ = x_prompt.shape
    nseq, t_new, _ = x_sample.shape
    n_p, n_s = bsz * seq, nseq * t_new
    lb = jax.nn.softmax(lb_raw.astype(F32), axis=0)[0]
    wpa, wpb, wo = w_pa[0].astype(BF16), w_pb[0].astype(BF16), w_o[0].astype(BF16)
    wr = jnp.pad(w_router[0].astype(F32), ((0, 0), (0, RT_LANES - N_EXPERTS)))
    wr_hi = wr.astype(BF16)
    wr_lo = (wr - wr_hi.astype(F32)).astype(BF16)
    br = jnp.pad(b_router[0].astype(F32), (0, RT_LANES - N_EXPERTS)).reshape(1, RT_LANES)
    caches = (cache_kv_w128[0], cache_kv_w512[0], cache_kv_w2048[0])

    xp = x_prompt.reshape(n_p, D_MODEL)
    xs = x_sample.reshape(n_s, D_MODEL)

    proj_p = _inproj(xp, attn_norm[0], w_in[0])
    qn_p, *kv_p = _qkv(proj_p, q_norm[0], k_norm[0])
    oa_p = _attn_prompt(qn_p, kv_p, jnp.stack([_prompt_bias(rel_bias, g) for g in range(3)]), bsz, seq)
    ob_p, st_p = _hgrn_prompt(proj_p, lb, g_norm[0], bsz, seq)
    mixed_p = _mix(oa_p, ob_p, proj_p, wpa, wpb)
    h_p, hn_p, lg_p = _resid(xp, mixed_p, wo, ffn_norm[0], wr_hi, wr_lo, br)

    proj_s = _inproj(xs, attn_norm[0], w_in[0])
    qn_s, *kv_s = _qkv(proj_s, q_norm[0], k_norm[0])
    oa_s = _attn_sample(qn_s, kv_s, caches, rel_bias, t_new)
    ob_s, st_s = _hgrn_sample(proj_s, lb, g_norm[0], state_hgrn[0], t_new)
    mixed_s = _mix(oa_s, ob_s, proj_s, wpa, wpb)
    h_s, hn_s, lg_s = _resid(xs, mixed_s, wo, ffn_norm[0], wr_hi, wr_lo, br)

    logits = jnp.concatenate([lg_p, lg_s], axis=0)[:, :N_EXPERTS]
    assert n_p % MOE_TM == 0 and n_s % MOE_TM == 0, "each pass must fill whole MoE token tiles"
    blk_e, blk_nsub, n_used, layout, slot, gate, n_blk = _route(logits)
    hn_all = jnp.concatenate([hn_p, hn_s], axis=0)
    xs_sorted, gate_rows = _dispatch(layout, slot, gate, hn_all, n_blk * MOE_RB)
    yb = _moe_experts(blk_e, blk_nsub, n_used, xs_sorted, gate_rows, w_up[0], b_up[0], w_down[0], b_down[0])
    y_p = _combine(layout, slot, h_p, yb, 0)
    y_s = _combine(layout, slot, h_s, yb, n_p // MOE_TM)

    def prompt_rows(kv, win):
        keep = min(win, seq)
        rows = kv.reshape(bsz, seq, 2 * A_OUT)[:, seq - keep:]
        return rows.reshape(1, bsz, keep, 2, A_HPG, HEAD)

    kvp = [prompt_rows(kv_p[g], A_GROUPS[g][0]) for g in range(3)]
    kvs = [kv_s[g].reshape(1, nseq, t_new, 2, A_HPG, HEAD) for g in range(3)]
    return (y_p.reshape(bsz, seq, D_MODEL), y_s.reshape(nseq, t_new, D_MODEL),
            kvp[0], kvp[1], kvp[2], st_p[None],
            kvs[0], kvs[1], kvs[2], st_s[None])
```

```python
import functools

import numpy as np
import jax
import jax.numpy as jnp
from jax import lax
from jax.experimental import pallas as pl
from jax.experimental.pallas import tpu as pltpu

F32 = jnp.float32
BF16 = jnp.bfloat16

D_MODEL = 2048
HEAD = 128
A_GROUPS = ((128, 1), (512, 4), (2048, 16))
A_HPG = 4
A_KEYS = 128
A_WIDTH = 3 * A_HPG * HEAD
A_OUT = A_HPG * HEAD
B_HEADS = 8
B_WIDTH = B_HEADS * HEAD
B_CHUNK = 32
REL_BUCKETS = 32
REL_MAX_DIST = 2048
N_EXPERTS = 32
TOP_K = 4
D_FF = 2048
SWIGLU_ALPHA = 1.702
SWIGLU_LIMIT = 7.0
NORM_EPS = 1e-6
NEG_INF = -1e30
ATTN_SCALE = HEAD ** -0.5
IN_WIDTH = 3 * A_WIDTH + 4 * B_WIDTH + 2 * D_MODEL
OFF_Q, OFF_K, OFF_V = 0, A_WIDTH, 2 * A_WIDTH
OFF_QB = 3 * A_WIDTH
OFF_FB, OFF_IB, OFF_OB = OFF_QB + B_WIDTH, OFF_QB + 2 * B_WIDTH, OFF_QB + 3 * B_WIDTH
OFF_GA = OFF_QB + 4 * B_WIDTH
OFF_GB = OFF_GA + D_MODEL

V7X_VMEM_LIMIT = 56 * 1024 * 1024
SUBLANES = 8

NT_DIMS = (((1,), (1,)), ((), ()))


def _sigmoid(x):
    return 1.0 / (1.0 + jnp.exp(-x))


def _cparams(sem):
    return pltpu.CompilerParams(dimension_semantics=sem, vmem_limit_bytes=V7X_VMEM_LIMIT)


NORM_TM = 512
PROJ_TM = 1024
PROJ_TN = 1280


def _rmsnorm_kernel(x_ref, nw_ref, o_ref):
    x = x_ref[...]
    ms = jnp.mean(x * x, axis=-1, keepdims=True)
    o_ref[...] = (x * lax.rsqrt(ms + NORM_EPS) * nw_ref[...]).astype(BF16)


def _inproj_kernel(xn_ref, w_ref, o_ref, wbf_ref):
    @pl.when(pl.program_id(1) == 0)
    def _():
        wbf_ref[...] = w_ref[...].astype(BF16)

    o_ref[...] = jnp.dot(xn_ref[...], wbf_ref[...], preferred_element_type=F32)


def _inproj(x, norm_w, w_in):
    m = x.shape[0]
    xn = pl.pallas_call(
        _rmsnorm_kernel,
        out_shape=jax.ShapeDtypeStruct((m, D_MODEL), BF16),
        grid=(m // NORM_TM,),
        in_specs=[pl.BlockSpec((NORM_TM, D_MODEL), lambda i: (i, 0)), pl.BlockSpec((1, D_MODEL), lambda i: (0, 0))],
        out_specs=pl.BlockSpec((NORM_TM, D_MODEL), lambda i: (i, 0)),
        compiler_params=_cparams(("arbitrary",)),
        name="attn_rmsnorm",
    )(x, norm_w.reshape(1, D_MODEL))
    tm = min(PROJ_TM, m)
    return pl.pallas_call(
        _inproj_kernel,
        out_shape=jax.ShapeDtypeStruct((m, IN_WIDTH), F32),
        grid=(IN_WIDTH // PROJ_TN, m // tm),
        in_specs=[
            pl.BlockSpec((tm, D_MODEL), lambda j, i: (i, 0)),
            pl.BlockSpec((D_MODEL, PROJ_TN), lambda j, i: (0, j)),
        ],
        out_specs=pl.BlockSpec((tm, PROJ_TN), lambda j, i: (i, j)),
        scratch_shapes=[pltpu.VMEM((D_MODEL, PROJ_TN), BF16)],
        compiler_params=_cparams(("arbitrary", "arbitrary")),
        name="inproj",
    )(xn, w_in)


QKV_TM = 512


def _qkv_kernel(q_ref, k_ref, v_ref, qw_ref, kw_ref, qn_ref, kv0_ref, kv1_ref, kv2_ref):
    kv_refs = (kv0_ref, kv1_ref, kv2_ref)
    qw = qw_ref[...]
    kw = kw_ref[...]
    for h in range(3 * A_HPG):
        sl = slice(h * HEAD, (h + 1) * HEAD)
        q = q_ref[:, sl]
        qn_ref[:, sl] = q * lax.rsqrt(jnp.mean(q * q, axis=-1, keepdims=True) + NORM_EPS) * qw
        k = k_ref[:, sl]
        kn = k * lax.rsqrt(jnp.mean(k * k, axis=-1, keepdims=True) + NORM_EPS) * kw
        g, hh = divmod(h, A_HPG)
        kv_refs[g][:, hh * HEAD:(hh + 1) * HEAD] = kn
        kv_refs[g][:, A_OUT + hh * HEAD:A_OUT + (hh + 1) * HEAD] = v_ref[:, sl]


def _qkv(proj, q_norm, k_norm):
    m = proj.shape[0]
    kv_shape = jax.ShapeDtypeStruct((m, 2 * A_OUT), F32)
    return pl.pallas_call(
        _qkv_kernel,
        out_shape=(jax.ShapeDtypeStruct((m, A_WIDTH), F32), kv_shape, kv_shape, kv_shape),
        grid=(m // QKV_TM,),
        in_specs=[
            pl.BlockSpec((QKV_TM, A_WIDTH), lambda i: (i, OFF_Q // A_WIDTH)),
            pl.BlockSpec((QKV_TM, A_WIDTH), lambda i: (i, OFF_K // A_WIDTH)),
            pl.BlockSpec((QKV_TM, A_WIDTH), lambda i: (i, OFF_V // A_WIDTH)),
            pl.BlockSpec((1, HEAD), lambda i: (0, 0)),
            pl.BlockSpec((1, HEAD), lambda i: (0, 0)),
        ],
        out_specs=(
            pl.BlockSpec((QKV_TM, A_WIDTH), lambda i: (i, 0)),
            pl.BlockSpec((QKV_TM, 2 * A_OUT), lambda i: (i, 0)),
            pl.BlockSpec((QKV_TM, 2 * A_OUT), lambda i: (i, 0)),
            pl.BlockSpec((QKV_TM, 2 * A_OUT), lambda i: (i, 0)),
        ),
        compiler_params=_cparams(("arbitrary",)),
        name="qkv_norm",
    )(proj, proj, proj, q_norm.reshape(1, HEAD), k_norm.reshape(1, HEAD))


def _t5_bucket(dist):
    dist = np.asarray(dist, np.int64)
    max_exact = REL_BUCKETS // 2
    ratio = np.log(np.maximum(dist, 1) / max_exact) / np.log(REL_MAX_DIST / max_exact)
    large = np.minimum(max_exact + (ratio * (REL_BUCKETS - max_exact)).astype(np.int64), REL_BUCKETS - 1)
    return np.where(dist < max_exact, dist, large).astype(np.int32)


def _bias_lookup(tab, bucket, valid):
    bucket = jnp.asarray(np.where(valid, bucket, -1).astype(np.int32))[None]
    out = jnp.full((tab.shape[1],) + bucket.shape[1:], NEG_INF, F32)
    for b in range(REL_BUCKETS):
        out = jnp.where(bucket == b, tab[b].reshape((-1,) + (1,) * (bucket.ndim - 1)), out)
    return out


def _prompt_bias(rel_bias, g):
    _, dil = A_GROUPS[g]
    n = A_KEYS
    j = n + np.arange(n)[:, None] - np.arange(2 * n)[None, :]
    valid = (j >= 0) & (j < n)
    tab = rel_bias[:, g * A_HPG:(g + 1) * A_HPG].astype(F32)
    return _bias_lookup(tab, _t5_bucket(np.clip(j, 0, n - 1) * dil), valid)


ATT_MIX_ROWS = 256


def _attn_prompt_kernel(q0_ref, q1_ref, q2_ref, k0_ref, v0_ref, k1_ref, v1_ref, k2_ref, v2_ref, bias_ref,
                        o_ref, og_scr, lg_scr, *, seq):
    n = A_KEYS
    h = pl.program_id(1)
    q_refs = (q0_ref, q1_ref, q2_ref)
    k_refs = (k0_ref, k1_ref, k2_ref)
    v_refs = (v0_ref, v1_ref, v2_ref)
    for g, (win, dil) in enumerate(A_GROUPS):
        bias = bias_ref[g, h]
        for r in range(dil):
            for i in range(seq // win):
                def rows(blk, r=r, dil=dil):
                    if dil == 1:
                        return pl.ds(blk * n, n)
                    return pl.ds(r + blk * n * dil, n, stride=dil)

                q = q_refs[g][rows(i), :].astype(BF16)
                if i == 0:
                    kk = k_refs[g][rows(i), :].astype(BF16)
                    vv = v_refs[g][rows(i), :].astype(BF16)
                    b_tile = bias[:, n:]
                else:
                    kk = jnp.concatenate([k_refs[g][rows(i - 1), :], k_refs[g][rows(i), :]], axis=0).astype(BF16)
                    vv = jnp.concatenate([v_refs[g][rows(i - 1), :], v_refs[g][rows(i), :]], axis=0).astype(BF16)
                    b_tile = bias
                s = lax.dot_general(q, kk, NT_DIMS, preferred_element_type=F32) * ATTN_SCALE + b_tile
                mx = jnp.max(s, axis=-1, keepdims=True)
                p = jnp.exp(s - mx)
                den = jnp.sum(p, axis=-1, keepdims=True)
                og_scr[g, rows(i), :] = jnp.dot(p.astype(BF16), vv, preferred_element_type=F32) / den
                lg_scr[g, rows(i), :] = jnp.broadcast_to(mx + jnp.log(den), (n, HEAD))
    for c in range(seq // ATT_MIX_ROWS):
        sl = slice(c * ATT_MIX_ROWS, (c + 1) * ATT_MIX_ROWS)
        la, lb_, lc = lg_scr[0, sl, :], lg_scr[1, sl, :], lg_scr[2, sl, :]
        top = jnp.maximum(jnp.maximum(la, lb_), lc)
        ea, eb, ec = jnp.exp(la - top), jnp.exp(lb_ - top), jnp.exp(lc - top)
        o_ref[sl, :] = (ea * og_scr[0, sl, :] + eb * og_scr[1, sl, :] + ec * og_scr[2, sl, :]) / (ea + eb + ec)


def _attn_prompt(qn, kvs, bias, bsz, seq):
    m = qn.shape[0]
    col = lambda fn: pl.BlockSpec((seq, HEAD), lambda b, h: (b, fn(h)))
    in_specs = [col(lambda h, g=g: g * A_HPG + h) for g in range(3)]
    for _ in range(3):
        in_specs += [col(lambda h: h), col(lambda h: A_HPG + h)]
    in_specs.append(pl.BlockSpec(bias.shape, lambda b, h: (0, 0, 0, 0)))
    return pl.pallas_call(
        functools.partial(_attn_prompt_kernel, seq=seq),
        out_shape=jax.ShapeDtypeStruct((m, A_OUT), F32),
        grid=(bsz, A_HPG),
        in_specs=in_specs,
        out_specs=col(lambda h: h),
        scratch_shapes=[pltpu.VMEM((3, seq, HEAD), F32), pltpu.VMEM((3, seq, HEAD), F32)],
        compiler_params=_cparams(("arbitrary", "arbitrary")),
        name="attn_prompt",
    )(qn, qn, qn, kvs[0], kvs[0], kvs[1], kvs[1], kvs[2], kvs[2], bias)


def _hgrn_gates(q, z, lb):
    qs = q * _sigmoid(q)
    logf = jnp.log(lb + (1.0 - lb) * _sigmoid(z))
    kb = (1.0 - lb) * _sigmoid(-z)
    return qs, kb, logf


def _hgrn_out(o, gate, gn):
    on = o * lax.rsqrt(jnp.mean(o * o, axis=-1, keepdims=True) + NORM_EPS) * gn
    return (on * (gate * _sigmoid(gate))).astype(BF16)


def _split3(x):
    hi = x.astype(BF16)
    r1 = x - hi.astype(F32)
    mid = r1.astype(BF16)
    lo = (r1 - mid.astype(F32)).astype(BF16)
    return hi, mid, lo


HG_TC = 256
HG_NC = HG_TC // B_CHUNK
HG_HPS = 4


def _hgrn_prompt_kernel(q_ref, f_ref, i_ref, g_ref, lb_ref, gn_ref, ob_ref, st_ref, s_scr):
    tb = pl.program_id(2)

    @pl.when(tb == 0)
    def _():
        s_scr[...] = jnp.zeros_like(s_scr)

    row = lax.broadcasted_iota(jnp.int32, (HG_TC, HG_TC), 0)
    col = lax.broadcasted_iota(jnp.int32, (HG_TC, HG_TC), 1)
    same_chunk = (row // B_CHUNK) == (col // B_CHUNK)
    causal = jnp.logical_and(same_chunk, col <= row)
    cum_mat = jnp.where(causal, 1.0, 0.0).astype(BF16)
    tcol = lax.broadcasted_iota(jnp.int32, (HEAD, HG_TC), 1) // B_CHUNK
    trow = lax.broadcasted_iota(jnp.int32, (HG_TC, HEAD), 0) // B_CHUNK
    gn = gn_ref[...]
    for h in range(HG_HPS):
        sl = slice(h * HEAD, (h + 1) * HEAD)
        qs, kb, logf = _hgrn_gates(q_ref[:, sl], f_ref[:, sl], lb_ref[:, sl])
        b3 = jnp.dot(cum_mat, jnp.concatenate(_split3(logf), axis=1), preferred_element_type=F32)
        b = b3[:, :HEAD] + b3[:, HEAD:2 * HEAD] + b3[:, 2 * HEAD:]
        bl_rows = [b[c * B_CHUNK + B_CHUNK - 1:(c + 1) * B_CHUNK, :] for c in range(HG_NC)]
        bl = jnp.concatenate([jnp.broadcast_to(r, (B_CHUNK, HEAD)) for r in bl_rows], axis=0)
        q_dec_f = qs * jnp.exp(b)
        q_dec = q_dec_f.astype(BF16)
        k_inv = (kb * jnp.exp(-b)).astype(BF16)
        k_end_t = jnp.transpose(kb * jnp.exp(bl - b))
        v = i_ref[:, sl].astype(BF16)
        att = lax.dot_general(q_dec, k_inv, NT_DIMS, preferred_element_type=F32)
        att = jnp.where(causal, att, 0.0).astype(BF16)
        o_intra = jnp.dot(att, v, preferred_element_type=F32)
        decay_t = jnp.exp(jnp.transpose(bl))
        ke_stack = jnp.concatenate([jnp.where(tcol == c, k_end_t, 0.0) for c in range(HG_NC)], axis=0)
        ds_all = jnp.dot(ke_stack.astype(BF16), v, preferred_element_type=F32)
        s = s_scr[h]
        starts = []
        for c in range(HG_NC):
            starts.append(s)
            s = decay_t[:, c * B_CHUNK:c * B_CHUNK + 1] * s + ds_all[c * HEAD:(c + 1) * HEAD, :]
        s_scr[h] = s
        q_bd = jnp.concatenate([jnp.where(trow == c, q_dec_f, 0.0) for c in range(HG_NC)], axis=1).astype(BF16)
        o_inter = jnp.dot(q_bd, jnp.concatenate(starts, axis=0).astype(BF16), preferred_element_type=F32)
        ob_ref[:, sl] = _hgrn_out(o_intra + o_inter, g_ref[:, sl], gn)

    @pl.when(tb == pl.num_programs(2) - 1)
    def _():
        st_ref[...] = s_scr[...]


def _hgrn_prompt(proj, lb, g_norm, bsz, seq):
    m = proj.shape[0]
    hw = HG_HPS * HEAD
    nt = seq // HG_TC
    rowblk = lambda b, hf, t: b * nt + t
    spec = lambda off: pl.BlockSpec((HG_TC, hw), lambda b, hf, t: (rowblk(b, hf, t), off // hw + hf))
    return pl.pallas_call(
        _hgrn_prompt_kernel,
        out_shape=(jax.ShapeDtypeStruct((m, B_WIDTH), BF16),
                   jax.ShapeDtypeStruct((bsz, B_HEADS, HEAD, HEAD), F32)),
        grid=(bsz, B_HEADS // HG_HPS, nt),
        in_specs=[
            spec(OFF_QB), spec(OFF_FB), spec(OFF_IB), spec(OFF_OB),
            pl.BlockSpec((1, hw), lambda b, hf, t: (0, hf)),
            pl.BlockSpec((1, HEAD), lambda b, hf, t: (0, 0)),
        ],
        out_specs=(
            pl.BlockSpec((HG_TC, hw), lambda b, hf, t: (rowblk(b, hf, t), hf)),
            pl.BlockSpec((None, HG_HPS, HEAD, HEAD), lambda b, hf, t: (b, hf, 0, 0)),
        ),
        scratch_shapes=[pltpu.VMEM((HG_HPS, HEAD, HEAD), F32)],
        compiler_params=_cparams(("arbitrary", "arbitrary", "arbitrary")),
        name="hgrn_prompt",
    )(proj, proj, proj, proj, lb.reshape(1, B_WIDTH), g_norm.reshape(1, HEAD))


HS_SEQ = 32


def _hgrn_sample_kernel(q_ref, f_ref, i_ref, g_ref, lb_ref, gn_ref, s0_ref, ob_ref, st_ref, *, t_new):
    rows = HS_SEQ * t_new
    qs, kb, logf = _hgrn_gates(q_ref[...], f_ref[...], lb_ref[...])
    t_of = lax.broadcasted_iota(jnp.int32, (rows, HEAD), 0) % t_new
    b = logf
    sh = 1
    while sh < t_new:
        b = b + jnp.where(t_of >= sh, pltpu.roll(b, sh, axis=0), 0.0)
        sh *= 2
    bl = b
    for back in range(1, t_new):
        bl = jnp.where(t_of == t_new - 1 - back, pltpu.roll(b, rows - back, axis=0), bl)
    q_dec = (qs * jnp.exp(b)).astype(BF16)
    k_inv = (kb * jnp.exp(-b)).astype(BF16)
    k_end_t = jnp.transpose(kb * jnp.exp(bl - b))
    decay_t = jnp.exp(jnp.transpose(bl))
    v = i_ref[...].astype(BF16)
    row = lax.broadcasted_iota(jnp.int32, (rows, rows), 0)
    col = lax.broadcasted_iota(jnp.int32, (rows, rows), 1)
    causal = jnp.logical_and(row // t_new == col // t_new, col <= row)
    att = lax.dot_general(q_dec, k_inv, NT_DIMS, preferred_element_type=F32)
    att = jnp.where(causal, att, 0.0).astype(BF16)
    o_intra = jnp.dot(att, v, preferred_element_type=F32)
    seq_of_col = lax.broadcasted_iota(jnp.int32, (HEAD, rows), 1) // t_new
    seq_of_row = lax.broadcasted_iota(jnp.int32, (rows, HEAD), 0) // t_new
    o_inter = jnp.zeros((rows, HEAD), F32)
    for s_i in range(HS_SEQ):
        s = s0_ref[s_i, 0]
        part = jnp.dot(q_dec, s.astype(BF16), preferred_element_type=F32)
        o_inter = jnp.where(seq_of_row == s_i, part, o_inter)
        ke = jnp.where(seq_of_col == s_i, k_end_t, 0.0).astype(BF16)
        ds = jnp.dot(ke, v, preferred_element_type=F32)
        st_ref[s_i, 0] = decay_t[:, s_i * t_new:s_i * t_new + 1] * s + ds
    ob_ref[...] = _hgrn_out(o_intra + o_inter, g_ref[...], gn_ref[...])


def _hgrn_sample(proj, lb, g_norm, s0, t_new):
    m = proj.shape[0]
    nseq = m // t_new
    rows = HS_SEQ * t_new
    spec = lambda off: pl.BlockSpec((rows, HEAD), lambda i, h: (i, off // HEAD + h))
    st_spec = pl.BlockSpec((HS_SEQ, 1, HEAD, HEAD), lambda i, h: (i, h, 0, 0))
    return pl.pallas_call(
        functools.partial(_hgrn_sample_kernel, t_new=t_new),
        out_shape=(jax.ShapeDtypeStruct((m, B_WIDTH), BF16),
                   jax.ShapeDtypeStruct((nseq, B_HEADS, HEAD, HEAD), F32)),
        grid=(nseq // HS_SEQ, B_HEADS),
        in_specs=[
            spec(OFF_QB), spec(OFF_FB), spec(OFF_IB), spec(OFF_OB),
            pl.BlockSpec((1, HEAD), lambda i, h: (0, h)),
            pl.BlockSpec((1, HEAD), lambda i, h: (0, 0)),
            st_spec,
        ],
        out_specs=(pl.BlockSpec((rows, HEAD), lambda i, h: (i, h)), st_spec),
        compiler_params=_cparams(("arbitrary", "arbitrary")),
        name="hgrn_sample",
    )(proj, proj, proj, proj, lb.reshape(1, B_WIDTH), g_norm.reshape(1, HEAD), s0)


SA_SEQ = 2
KV_ROWS = 2 * A_HPG


def _sample_bias(rel_bias, t_new):
    def tile(tab, dist, valid):
        bias = _bias_lookup(tab, _t5_bucket(np.clip(dist, 0, REL_MAX_DIST)), valid)
        bias = jnp.moveaxis(bias, 0, -1)
        bias = jnp.concatenate([bias, jnp.zeros(bias.shape[:-1] + (KV_ROWS - A_HPG,), F32)], axis=-1)
        return jnp.broadcast_to(bias[..., None], bias.shape + (HEAD,))

    t = np.arange(t_new)[:, None]
    r = np.arange(A_KEYS + t_new)[None, :]
    is_new = r >= A_KEYS
    dist = np.where(is_new, t - (r - A_KEYS), A_KEYS + t - r)
    valid = (dist >= 0) & (dist < A_KEYS)
    b0 = tile(rel_bias[:, 0:A_HPG].astype(F32), dist, valid)
    b12 = []
    for g in (1, 2):
        _, dil = A_GROUPS[g]
        m = np.arange(A_KEYS + 1)
        dist = dil * (A_KEYS - m)
        b12.append(tile(rel_bias[:, g * A_HPG:(g + 1) * A_HPG].astype(F32), dist, m >= 1))
    return b0, jnp.stack(b12)


def _attn_sample_kernel(q_ref, n0_ref, n1_ref, n2_ref, c0_ref, c1_ref, c2_ref, b0_ref, b12_ref,
                        o_ref, qt_scr, ns_scr, *, t_new):
    nq = SA_SEQ * t_new
    new_refs = (n0_ref, n1_ref, n2_ref)
    cache_refs = (c0_ref, c1_ref, c2_ref)
    qt_scr[...] = jnp.zeros_like(qt_scr)
    for g in range(3):
        for j in range(nq):
            for h in range(A_HPG):
                qt_scr[g, j, h:h + 1, :] = (
                    q_ref[j:j + 1, g * A_OUT + h * HEAD:g * A_OUT + (h + 1) * HEAD] * ATTN_SCALE)
                ns_scr[g, j, h:h + 1, :] = new_refs[g][j:j + 1, h * HEAD:(h + 1) * HEAD]
                ns_scr[g, j, A_HPG + h:A_HPG + h + 1, :] = (
                    new_refs[g][j:j + 1, A_OUT + h * HEAD:A_OUT + (h + 1) * HEAD])

    row_id = lax.broadcasted_iota(jnp.int32, (nq, HEAD), 0)
    ones = jnp.ones((HEAD, HEAD), BF16)

    def per_query(j, out):
        s = j // t_new
        t = j - s * t_new
        row0 = s * A_KEYS
        stats = []
        for g in range(3):
            c_ref = cache_refs[g]
            qt = qt_scr[g, j]
            if g == 0:
                sub0 = 0
                bias_c = b0_ref[t, 0:A_KEYS]
                new_tiles = [(b0_ref[t, A_KEYS + tp], ns_scr[0, s * t_new + tp]) for tp in range(t_new)]
            else:
                sub0 = pl.multiple_of(t * KV_ROWS, KV_ROWS)
                bias_c = b12_ref[g - 1, 0:A_KEYS]
                new_tiles = [(b12_ref[g - 1, A_KEYS], ns_scr[g, j])]

            def mirror(l):
                return l + pltpu.roll(l, A_HPG, axis=l.ndim - 2)

            tiles = c_ref[pl.ds(row0, A_KEYS), pl.ds(sub0, KV_ROWS), :]
            prod = (tiles * qt[None]).reshape(A_KEYS * KV_ROWS, HEAD).astype(BF16)
            sums = jnp.dot(prod, ones, preferred_element_type=F32).reshape(A_KEYS, KV_ROWS, HEAD)
            l_c = mirror(sums + bias_c)
            new_l = [mirror(jnp.sum(tile_kv * qt, axis=-1, keepdims=True) + bias) for bias, tile_kv in new_tiles]
            mx = jnp.max(l_c, axis=0)
            for l in new_l:
                mx = jnp.maximum(mx, l)
            p_c = jnp.exp(l_c - mx[None])
            den = jnp.sum(p_c, axis=0)
            acc = jnp.sum(p_c * tiles, axis=0)
            for l, (_, tile_kv) in zip(new_l, new_tiles):
                p = jnp.exp(l - mx)
                den = den + p
                acc = acc + p * tile_kv
            stats.append((mx + jnp.log(den), den, acc))
        top = jnp.maximum(jnp.maximum(stats[0][0], stats[1][0]), stats[2][0])
        e = [jnp.exp(st[0] - top) for st in stats]
        tot = e[0] + e[1] + e[2]
        mix = jnp.zeros((KV_ROWS, HEAD), F32)
        for g in range(3):
            mix = mix + e[g] / (tot * stats[g][1]) * stats[g][2]
        pieces = [jnp.broadcast_to(mix[A_HPG + h:A_HPG + h + 1, :], (nq, HEAD)) for h in range(A_HPG)]
        return tuple(jnp.where(row_id == j, pieces[h], out[h]) for h in range(A_HPG))

    init = tuple(jnp.zeros((nq, HEAD), F32) for _ in range(A_HPG))
    out = lax.fori_loop(0, nq, per_query, init)
    for h in range(A_HPG):
        o_ref[:, h * HEAD:(h + 1) * HEAD] = out[h]


def _attn_sample(qn, kv_new, caches, rel_bias, t_new):
    m = qn.shape[0]
    nseq = m // t_new
    nq = SA_SEQ * t_new
    assert nq % SUBLANES == 0, "the new-token rows of one grid step must fill whole sublane tiles"
    kvw = 2 * A_OUT
    views, specs = [], []
    for g, (win, dil) in enumerate(A_GROUPS):
        c = caches[g]
        assert c.shape[1] == win, "cache must hold exactly one window of positions"
        assert g == 0 or dil >= t_new, "each new token must sit in its own residue class"
        views.append(c.reshape(nseq * A_KEYS, dil * KV_ROWS, HEAD))
        specs.append(pl.BlockSpec((SA_SEQ * A_KEYS, min(dil, t_new) * KV_ROWS, HEAD), lambda i: (i, 0, 0)))
    b0, b12 = _sample_bias(rel_bias, t_new)
    new_spec = pl.BlockSpec((nq, kvw), lambda i: (i, 0))
    return pl.pallas_call(
        functools.partial(_attn_sample_kernel, t_new=t_new),
        out_shape=jax.ShapeDtypeStruct((m, A_OUT), F32),
        grid=(nseq // SA_SEQ,),
        in_specs=[
            pl.BlockSpec((nq, A_WIDTH), lambda i: (i, 0)),
            new_spec, new_spec, new_spec,
            specs[0], specs[1], specs[2],
            pl.BlockSpec(b0.shape, lambda i: (0, 0, 0, 0)),
            pl.BlockSpec(b12.shape, lambda i: (0, 0, 0, 0)),
        ],
        out_specs=pl.BlockSpec((nq, A_OUT), lambda i: (i, 0)),
        scratch_shapes=[
            pltpu.VMEM((3, nq, KV_ROWS, HEAD), F32),
            pltpu.VMEM((3, nq, KV_ROWS, HEAD), F32),
        ],
        compiler_params=_cparams(("arbitrary",)),
        name="attn_sample",
    )(qn, kv_new[0], kv_new[1], kv_new[2], views[0], views[1], views[2], b0, b12)


MIX_TM = 512
MIX_TN = 512


def _mix_kernel(oa_ref, ob_ref, ga_ref, gb_ref, wpa_ref, wpb_ref, out_ref):
    pa = jnp.dot(oa_ref[...].astype(BF16), wpa_ref[...], preferred_element_type=F32)
    pb = jnp.dot(ob_ref[...], wpb_ref[...], preferred_element_type=F32)
    out_ref[...] = (_sigmoid(ga_ref[...]) * pa + _sigmoid(gb_ref[...]) * pb).astype(BF16)


def _mix(oa, ob, proj, w_pa, w_pb):
    m = proj.shape[0]
    return pl.pallas_call(
        _mix_kernel,
        out_shape=jax.ShapeDtypeStruct((m, D_MODEL), BF16),
        grid=(m // MIX_TM, D_MODEL // MIX_TN),
        in_specs=[
            pl.BlockSpec((MIX_TM, A_OUT), lambda i, j: (i, 0)),
            pl.BlockSpec((MIX_TM, B_WIDTH), lambda i, j: (i, 0)),
            pl.BlockSpec((MIX_TM, MIX_TN), lambda i, j: (i, OFF_GA // MIX_TN + j)),
            pl.BlockSpec((MIX_TM, MIX_TN), lambda i, j: (i, OFF_GB // MIX_TN + j)),
            pl.BlockSpec((A_OUT, MIX_TN), lambda i, j: (0, j)),
            pl.BlockSpec((B_WIDTH, MIX_TN), lambda i, j: (0, j)),
        ],
        out_specs=pl.BlockSpec((MIX_TM, MIX_TN), lambda i, j: (i, j)),
        compiler_params=_cparams(("arbitrary", "arbitrary")),
        name="mix",
    )(oa, ob, proj, proj, w_pa, w_pb)


RT_LANES = 128


def _resid_kernel(x_ref, mixed_ref, wo_ref, nw_ref, wr_hi_ref, wr_lo_ref, br_ref, h_ref, hn_ref, lg_ref):
    h = x_ref[...] + jnp.dot(mixed_ref[...], wo_ref[...], preferred_element_type=F32)
    h_ref[...] = h
    hn = h * lax.rsqrt(jnp.mean(h * h, axis=-1, keepdims=True) + NORM_EPS) * nw_ref[...]
    hi = hn.astype(BF16)
    hn_ref[...] = hi
    lo = (hn - hi.astype(F32)).astype(BF16)
    lg = (jnp.dot(hi, wr_hi_ref[...], preferred_element_type=F32)
          + jnp.dot(lo, wr_hi_ref[...], preferred_element_type=F32)
          + jnp.dot(hi, wr_lo_ref[...], preferred_element_type=F32))
    lg_ref[...] = lg + br_ref[...]


def _resid(x, mixed, w_o, ffn_norm, wr_hi, wr_lo, br):
    m = x.shape[0]
    full = lambda shape: pl.BlockSpec(shape, lambda i: (0, 0))
    return pl.pallas_call(
        _resid_kernel,
        out_shape=(jax.ShapeDtypeStruct((m, D_MODEL), F32),
                   jax.ShapeDtypeStruct((m, D_MODEL), BF16),
                   jax.ShapeDtypeStruct((m, RT_LANES), F32)),
        grid=(m // MIX_TM,),
        in_specs=[
            pl.BlockSpec((MIX_TM, D_MODEL), lambda i: (i, 0)),
            pl.BlockSpec((MIX_TM, D_MODEL), lambda i: (i, 0)),
            full((D_MODEL, D_MODEL)), full((1, D_MODEL)),
            full((D_MODEL, RT_LANES)), full((D_MODEL, RT_LANES)), full((1, RT_LANES)),
        ],
        out_specs=(
            pl.BlockSpec((MIX_TM, D_MODEL), lambda i: (i, 0)),
            pl.BlockSpec((MIX_TM, D_MODEL), lambda i: (i, 0)),
            pl.BlockSpec((MIX_TM, RT_LANES), lambda i: (i, 0)),
        ),
        compiler_params=_cparams(("arbitrary",)),
        name="resid_router",
    )(x, mixed, w_o, ffn_norm.reshape(1, D_MODEL), wr_hi, wr_lo, br)


MOE_RB = 1280
MOE_SUB = 128
MOE_TF = 256
MOE_NF = D_FF // MOE_TF


MOE_TM = 512
MOE_GRAN = 16
MOE_SLOTS = 2560
MOE_CHUNK = 256
assert MOE_SLOTS >= MOE_TM * TOP_K + N_EXPERTS * (MOE_GRAN - 1) and MOE_SLOTS % MOE_CHUNK == 0


def _route(logits):
    n_tok = logits.shape[0]
    nt = n_tok // MOE_TM
    top_val, top_idx = lax.top_k(logits, TOP_K)
    gate = jax.nn.softmax(top_val, axis=-1)
    na = MOE_TM * TOP_K
    e3 = top_idx.reshape(nt, na)
    hot = e3[..., None] == jnp.arange(N_EXPERTS)
    onehot = hot.astype(jnp.int32)
    lower = (lax.broadcasted_iota(jnp.int32, (na, na), 0) >= lax.broadcasted_iota(jnp.int32, (na, na), 1))
    incl = jnp.einsum('ij,tjk->tik', lower.astype(BF16), hot.astype(BF16),
                      preferred_element_type=F32).astype(jnp.int32)
    rank = jnp.sum((incl - onehot) * onehot, axis=2)
    seg = (incl[:, -1] + MOE_GRAN - 1) // MOE_GRAN * MOE_GRAN
    loc_off = jnp.cumsum(seg, axis=1) - seg
    rows_e = jnp.sum(seg, axis=0)
    padded_sub = (rows_e + MOE_SUB - 1) // MOE_SUB * MOE_SUB
    padded_rb = (rows_e + MOE_RB - 1) // MOE_RB * MOE_RB
    blk_end = jnp.cumsum(padded_rb) // MOE_RB
    row_start = jnp.cumsum(padded_rb) - padded_rb
    seg_start = row_start[None, :] + jnp.cumsum(seg, axis=0) - seg
    slot = jnp.sum(onehot * loc_off[:, None, :], axis=2) + rank
    max_rows = n_tok * TOP_K + nt * N_EXPERTS * (MOE_GRAN - 1)
    n_blk = max_rows // MOE_RB + N_EXPERTS
    blk_ids = jnp.arange(n_blk)
    blk_e = jnp.minimum(jnp.searchsorted(blk_end, blk_ids, side='right'), N_EXPERTS - 1).astype(jnp.int32)
    rows_left = padded_sub[blk_e] - (blk_ids * MOE_RB - row_start[blk_e])
    blk_nsub = jnp.clip(rows_left // MOE_SUB, 0, MOE_RB // MOE_SUB).astype(jnp.int32)
    n_used = blk_end[-1:].astype(jnp.int32)
    tail_row = jnp.where(rows_e > 0, row_start + padded_sub - MOE_SUB, -1).astype(jnp.int32)
    layout = dict(seg_start=seg_start.reshape(-1).astype(jnp.int32),
                  seg_gran=(seg // MOE_GRAN).reshape(-1).astype(jnp.int32),
                  loc_off=loc_off.reshape(-1).astype(jnp.int32),
                  tail_row=tail_row)
    slot = slot.reshape(nt, MOE_TM, TOP_K).astype(jnp.int32)
    return blk_e, blk_nsub, n_used, layout, slot, gate.reshape(nt, MOE_TM, TOP_K), n_blk


def _segment_copies(layout_refs, tile, src_of, dst_of, sem, start):
    seg_start_ref, seg_gran_ref, loc_off_ref = layout_refs

    for e in range(N_EXPERTS):
        idx = tile * N_EXPERTS + e
        loc = loc_off_ref[idx]
        glob = seg_start_ref[idx]

        def body(c, carry, loc=loc, glob=glob):
            copy = pltpu.make_async_copy(
                src_of(pl.multiple_of(loc + c * MOE_GRAN, MOE_GRAN), pl.multiple_of(glob + c * MOE_GRAN, MOE_GRAN)),
                dst_of(pl.multiple_of(loc + c * MOE_GRAN, MOE_GRAN), pl.multiple_of(glob + c * MOE_GRAN, MOE_GRAN)),
                sem)
            if start:
                copy.start()
            else:
                copy.wait()
            return carry

        lax.fori_loop(0, seg_gran_ref[idx], body, 0)


def _slot_onehot(slot_vecs, base, shape, axis):
    ids = base + lax.broadcasted_iota(jnp.int32, shape, axis)
    hit = jnp.zeros(shape, F32)
    for v in slot_vecs:
        hit = jnp.where(ids == v, 1.0, hit)
    return hit.astype(BF16)


def _dispatch_kernel(seg_start_ref, seg_gran_ref, loc_off_ref, tail_ref, slot_ref, gate_ref, x_ref,
                     xs_ref, gs_ref, sorted_scr, gsorted_scr, zero_scr, gzero_scr, sem, buf_sem):
    i = pl.program_id(0)

    @pl.when(i == 0)
    def _():
        zero_scr[...] = jnp.zeros_like(zero_scr)
        gzero_scr[...] = jnp.zeros_like(gzero_scr)

        def tail_copies(e):
            rows = pl.ds(pl.multiple_of(jnp.maximum(tail_ref[e], 0), MOE_SUB), MOE_SUB)
            return (pltpu.make_async_copy(zero_scr, xs_ref.at[rows], sem),
                    pltpu.make_async_copy(gzero_scr, gs_ref.at[rows], sem))

        for e in range(N_EXPERTS):
            @pl.when(tail_ref[e] >= 0)
            def _():
                for copy in tail_copies(e):
                    copy.start()
        for e in range(N_EXPERTS):
            @pl.when(tail_ref[e] >= 0)
            def _():
                for copy in tail_copies(e):
                    copy.wait()

    buf = lax.rem(i, 2)
    x = x_ref[...]
    slot_vecs = [slot_ref[k:k + 1, :] for k in range(TOP_K)]
    for c in range(MOE_SLOTS // MOE_CHUNK):
        chunk = slice(c * MOE_CHUNK, (c + 1) * MOE_CHUNK)
        ids = c * MOE_CHUNK + lax.broadcasted_iota(jnp.int32, (MOE_CHUNK, MOE_TM), 0)
        hit = jnp.zeros((MOE_CHUNK, MOE_TM), F32)
        gval = jnp.zeros((MOE_CHUNK, MOE_TM), F32)
        for k in range(TOP_K):
            match = ids == slot_vecs[k]
            hit = jnp.where(match, 1.0, hit)
            gval = jnp.where(match, gate_ref[k:k + 1, :], gval)
        sorted_scr[buf, chunk, :] = jnp.dot(hit.astype(BF16), x,
                                            preferred_element_type=F32).astype(BF16)
        gsorted_scr[buf, chunk, :] = jnp.broadcast_to(jnp.sum(gval, axis=1, keepdims=True), (MOE_CHUNK, HEAD))

    layout_refs = (seg_start_ref, seg_gran_ref, loc_off_ref)

    def copies(tile, b, start):
        streams = (
            (lambda loc, glob: sorted_scr.at[b, pl.ds(loc, MOE_GRAN)],
             lambda loc, glob: xs_ref.at[pl.ds(glob, MOE_GRAN)]),
            (lambda loc, glob: gsorted_scr.at[b, pl.ds(loc, MOE_GRAN)],
             lambda loc, glob: gs_ref.at[pl.ds(glob, MOE_GRAN)]),
        )
        for src_of, dst_of in streams:
            _segment_copies(layout_refs, tile, src_of, dst_of, buf_sem.at[b], start=start)

    copies(i, buf, True)

    @pl.when(i > 0)
    def _():
        copies(i - 1, 1 - buf, False)

    @pl.when(i == pl.num_programs(0) - 1)
    def _():
        copies(i, buf, False)


def _dispatch(layout, slot, gate, hn, n_rows):
    n_tok = hn.shape[0]
    slot_t = jnp.swapaxes(slot, 1, 2)
    gate_t = jnp.swapaxes(gate, 1, 2)
    grid_spec = pltpu.PrefetchScalarGridSpec(
        num_scalar_prefetch=4,
        grid=(n_tok // MOE_TM,),
        in_specs=[
            pl.BlockSpec((None, TOP_K, MOE_TM), lambda i, *_: (i, 0, 0)),
            pl.BlockSpec((None, TOP_K, MOE_TM), lambda i, *_: (i, 0, 0)),
            pl.BlockSpec((MOE_TM, D_MODEL), lambda i, *_: (i, 0)),
        ],
        out_specs=(pl.BlockSpec(memory_space=pl.ANY), pl.BlockSpec(memory_space=pl.ANY)),
        scratch_shapes=[
            pltpu.VMEM((2, MOE_SLOTS, D_MODEL), BF16),
            pltpu.VMEM((2, MOE_SLOTS, HEAD), F32),
            pltpu.VMEM((MOE_SUB, D_MODEL), BF16),
            pltpu.VMEM((MOE_SUB, HEAD), F32),
            pltpu.SemaphoreType.DMA,
            pltpu.SemaphoreType.DMA((2,)),
        ],
    )
    return pl.pallas_call(
        _dispatch_kernel,
        out_shape=(jax.ShapeDtypeStruct((n_rows, D_MODEL), BF16), jax.ShapeDtypeStruct((n_rows, HEAD), F32)),
        grid_spec=grid_spec,
        compiler_params=_cparams(("arbitrary",)),
        name="moe_dispatch",
    )(layout["seg_start"], layout["seg_gran"], layout["loc_off"], layout["tail_row"], slot_t, gate_t, hn)


def _moe_kernel(blk_e_ref, blk_nsub_ref, n_used_ref, xs_ref, gate_ref, wg_ref, wl_ref, bg_ref, bl_ref,
                wd_ref, bd_ref, yb_ref, wu_s, wd_s, act_s, y_s, pend_ref, sem):
    i = pl.program_id(0)
    f = pl.program_id(1)
    last_step = jnp.logical_and(i == pl.num_programs(0) - 1, f == MOE_NF - 1)

    def y_copy(blk, sb):
        rows = pl.ds(pl.multiple_of(sb * MOE_SUB, MOE_SUB), MOE_SUB)
        dst = pl.ds(pl.multiple_of(blk * MOE_RB + sb * MOE_SUB, MOE_SUB), MOE_SUB)
        return pltpu.make_async_copy(y_s.at[rows], yb_ref.at[dst], sem)

    def drain():
        blk = pend_ref[1]

        def body(sb, c):
            y_copy(blk, sb).wait()
            return c

        lax.fori_loop(0, pend_ref[0], body, 0)
        pend_ref[0] = 0

    @pl.when(jnp.logical_and(i == 0, f == 0))
    def _():
        pend_ref[0] = 0
        pend_ref[1] = 0

    def over_granules(nsub, fn):
        def quad(sb, c):
            fn(sb * (4 * MOE_SUB), 4 * MOE_SUB)
            return c

        lax.fori_loop(0, nsub // 4, quad, 0)
        rem = nsub % 4
        done = nsub - rem

        @pl.when(rem >= 2)
        def _():
            fn(done * MOE_SUB, 2 * MOE_SUB)

        @pl.when(rem % 2 == 1)
        def _():
            fn((nsub - 1) * MOE_SUB, MOE_SUB)

    @pl.when(i < n_used_ref[0])
    def _():
        wu_s[:, :MOE_TF] = wg_ref[...].astype(BF16)
        wu_s[:, MOE_TF:] = wl_ref[...].astype(BF16)
        wd_s[pl.ds(pl.multiple_of(f * MOE_TF, MOE_TF), MOE_TF), :] = wd_ref[...].astype(BF16)
        b_up = jnp.concatenate([bg_ref[...], bl_ref[...]], axis=1)
        nsub = blk_nsub_ref[i]

        def up(start, size):
            rows = pl.ds(pl.multiple_of(start, MOE_SUB), size)
            hh = jnp.dot(xs_ref[rows, :], wu_s[...], preferred_element_type=F32) + b_up
            glu = jnp.minimum(hh[:, :MOE_TF], SWIGLU_LIMIT)
            lin = jnp.clip(hh[:, MOE_TF:], -SWIGLU_LIMIT, SWIGLU_LIMIT)
            act_s[f, rows, :] = (glu * _sigmoid(SWIGLU_ALPHA * glu) * (lin + 1.0)).astype(BF16)

        over_granules(nsub, up)

        @pl.when(f == MOE_NF - 1)
        def _():
            drain()

            def down(start, size):
                rows = pl.ds(pl.multiple_of(start, MOE_SUB), size)
                act = jnp.concatenate([act_s[k, rows, :] for k in range(MOE_NF)], axis=1)
                y = jnp.dot(act, wd_s[...], preferred_element_type=F32) + bd_ref[...]
                y_s[rows, :] = y * gate_ref[rows, 0:1]

            over_granules(nsub, down)

            def send(sb, c):
                y_copy(i, sb).start()
                return c

            lax.fori_loop(0, nsub, send, 0)
            pend_ref[0] = nsub
            pend_ref[1] = i

    @pl.when(last_step)
    def _():
        drain()


def _moe_experts(blk_e, blk_nsub, n_used, xs, gate_rows, w_up, b_up, w_down, b_down):
    n_blk = xs.shape[0] // MOE_RB

    def live(i, n_used_ref):
        return jnp.minimum(i, n_used_ref[0] - 1)

    def ff(i, f, n_used_ref):
        return jnp.where(i < n_used_ref[0], f, MOE_NF - 1)

    def expert(i, be, nu):
        return be[live(i, nu)]

    grid_spec = pltpu.PrefetchScalarGridSpec(
        num_scalar_prefetch=3,
        grid=(n_blk, MOE_NF),
        in_specs=[
            pl.BlockSpec((MOE_RB, D_MODEL), lambda i, f, be, ns, nu: (live(i, nu), 0)),
            pl.BlockSpec((MOE_RB, HEAD), lambda i, f, be, ns, nu: (live(i, nu), 0)),
            pl.BlockSpec((None, D_MODEL, MOE_TF), lambda i, f, be, ns, nu: (expert(i, be, nu), 0, ff(i, f, nu))),
            pl.BlockSpec((None, D_MODEL, MOE_TF),
                         lambda i, f, be, ns, nu: (expert(i, be, nu), 0, ff(i, f, nu) + MOE_NF)),
            pl.BlockSpec((None, 1, MOE_TF), lambda i, f, be, ns, nu: (expert(i, be, nu), 0, ff(i, f, nu))),
            pl.BlockSpec((None, 1, MOE_TF),
                         lambda i, f, be, ns, nu: (expert(i, be, nu), 0, ff(i, f, nu) + MOE_NF)),
            pl.BlockSpec((None, MOE_TF, D_MODEL), lambda i, f, be, ns, nu: (expert(i, be, nu), ff(i, f, nu), 0)),
            pl.BlockSpec((None, 1, D_MODEL), lambda i, f, be, ns, nu: (expert(i, be, nu), 0, 0)),
        ],
        out_specs=pl.BlockSpec(memory_space=pl.ANY),
        scratch_shapes=[
            pltpu.VMEM((D_MODEL, 2 * MOE_TF), BF16),
            pltpu.VMEM((D_FF, D_MODEL), BF16),
            pltpu.VMEM((MOE_NF, MOE_RB, MOE_TF), BF16),
            pltpu.VMEM((MOE_RB, D_MODEL), F32),
            pltpu.SMEM((2,), jnp.int32),
            pltpu.SemaphoreType.DMA,
        ],
    )
    b_up3 = b_up.reshape(N_EXPERTS, 1, 2 * D_FF)
    return pl.pallas_call(
        _moe_kernel,
        out_shape=jax.ShapeDtypeStruct(xs.shape, F32),
        grid_spec=grid_spec,
        compiler_params=_cparams(("arbitrary", "arbitrary")),
        name="moe_experts",
    )(blk_e, blk_nsub, n_used, xs, gate_rows, w_up, w_up, b_up3, b_up3, w_down,
      b_down.reshape(N_EXPERTS, 1, D_MODEL))


def _combine_kernel(seg_start_ref, seg_gran_ref, loc_off_ref, slot_ref, h_ref, yb_ref, y_ref, rows_scr, sem, *,
                    tile0):
    i = pl.program_id(0)

    @pl.when(i == 0)
    def _():
        rows_scr[...] = jnp.zeros_like(rows_scr)

    layout_refs = (seg_start_ref, seg_gran_ref, loc_off_ref)
    src_of = lambda loc, glob: yb_ref.at[pl.ds(glob, MOE_GRAN)]
    dst_of = lambda loc, glob: rows_scr.at[pl.ds(loc, MOE_GRAN)]
    _segment_copies(layout_refs, tile0 + i, src_of, dst_of, sem, start=True)
    _segment_copies(layout_refs, tile0 + i, src_of, dst_of, sem, start=False)

    slot_vecs = [slot_ref[:, k:k + 1] for k in range(TOP_K)]
    acc = h_ref[...]
    for c in range(MOE_SLOTS // MOE_CHUNK):
        pick = _slot_onehot(slot_vecs, c * MOE_CHUNK, (MOE_TM, MOE_CHUNK), 1)
        rows = rows_scr[c * MOE_CHUNK:(c + 1) * MOE_CHUNK, :]
        hi = rows.astype(BF16)
        lo = (rows - hi.astype(F32)).astype(BF16)
        acc = acc + (jnp.dot(pick, hi, preferred_element_type=F32)
                     + jnp.dot(pick, lo, preferred_element_type=F32))
    y_ref[...] = acc


def _combine(layout, slot, h, yb, tile0):
    m = h.shape[0]
    grid_spec = pltpu.PrefetchScalarGridSpec(
        num_scalar_prefetch=3,
        grid=(m // MOE_TM,),
        in_specs=[
            pl.BlockSpec((None, MOE_TM, TOP_K), lambda i, *_: (i + tile0, 0, 0)),
            pl.BlockSpec((MOE_TM, D_MODEL), lambda i, *_: (i, 0)),
            pl.BlockSpec(memory_space=pl.ANY),
        ],
        out_specs=pl.BlockSpec((MOE_TM, D_MODEL), lambda i, *_: (i, 0)),
        scratch_shapes=[pltpu.VMEM((MOE_SLOTS, D_MODEL), F32), pltpu.SemaphoreType.DMA],
    )
    return pl.pallas_call(
        functools.partial(_combine_kernel, tile0=tile0),
        out_shape=jax.ShapeDtypeStruct((m, D_MODEL), F32),
        grid_spec=grid_spec,
        compiler_params=_cparams(("arbitrary",)),
        name="moe_combine",
    )(layout["seg_start"], layout["seg_gran"], layout["loc_off"], slot, h, yb)


def kernel(x_prompt, x_sample, cache_kv_w128, cache_kv_w512, cache_kv_w2048, state_hgrn, rel_bias, attn_norm,
           w_in, q_norm, k_norm, lb_raw, g_norm, w_pa, w_pb, w_o, ffn_norm, w_router, b_router, w_up, b_up,
           w_down, b_down):
    assert attn_norm.shape[0] == 1, "single-layer stack"
    bsz, seq, _ = x_prompt.shape
    nseq, t_new, _ = x_sample.shape
    n_p, n_s = bsz * seq, nseq * t_new
    lb = jax.nn.softmax(lb_raw.astype(F32), axis=0)[0]
    wpa, wpb, wo = w_pa[0].astype(BF16), w_pb[0].astype(BF16), w_o[0].astype(BF16)
    wr = jnp.pad(w_router[0].astype(F32), ((0, 0), (0, RT_LANES - N_EXPERTS)))
    wr_hi = wr.astype(BF16)
    wr_lo = (wr - wr_hi.astype(F32)).astype(BF16)
    br = jnp.pad(b_router[0].astype(F32), (0, RT_LANES - N_EXPERTS)).reshape(1, RT_LANES)
    caches = (cache_kv_w128[0], cache_kv_w512[0], cache_kv_w2048[0])

    xp = x_prompt.reshape(n_p, D_MODEL)
    xs = x_sample.reshape(n_s, D_MODEL)

    proj_p = _inproj(xp, attn_norm[0], w_in[0])
    qn_p, *kv_p = _qkv(proj_p, q_norm[0], k_norm[0])
    oa_p = _attn_prompt(qn_p, kv_p, jnp.stack([_prompt_bias(rel_bias, g) for g in range(3)]), bsz, seq)
    ob_p, st_p = _hgrn_prompt(proj_p, lb, g_norm[0], bsz, seq)
    mixed_p = _mix(oa_p, ob_p, proj_p, wpa, wpb)
    h_p, hn_p, lg_p = _resid(xp, mixed_p, wo, ffn_norm[0], wr_hi, wr_lo, br)

    proj_s = _inproj(xs, attn_norm[0], w_in[0])
    qn_s, *kv_s = _qkv(proj_s, q_norm[0], k_norm[0])
    oa_s = _attn_sample(qn_s, kv_s, caches, rel_bias, t_new)
    ob_s, st_s = _hgrn_sample(proj_s, lb, g_norm[0], state_hgrn[0], t_new)
    mixed_s = _mix(oa_s, ob_s, proj_s, wpa, wpb)
    h_s, hn_s, lg_s = _resid(xs, mixed_s, wo, ffn_norm[0], wr_hi, wr_lo, br)

    logits = jnp.concatenate([lg_p, lg_s], axis=0)[:, :N_EXPERTS]
    assert n_p % MOE_TM == 0 and n_s % MOE_TM == 0, "each pass must fill whole MoE token tiles"
    blk_e, blk_nsub, n_used, layout, slot, gate, n_blk = _route(logits)
    hn_all = jnp.concatenate([hn_p, hn_s], axis=0)
    xs_sorted, gate_rows = _dispatch(layout, slot, gate, hn_all, n_blk * MOE_RB)
    yb = _moe_experts(blk_e, blk_nsub, n_used, xs_sorted, gate_rows, w_up[0], b_up[0], w_down[0], b_down[0])
    y_p = _combine(layout, slot, h_p, yb, 0)
    y_s = _combine(layout, slot, h_s, yb, n_p // MOE_TM)

    def prompt_rows(kv, win):
        keep = min(win, seq)
        rows = kv.reshape(bsz, seq, 2 * A_OUT)[:, seq - keep:]
        return rows.reshape(1, bsz, keep, 2, A_HPG, HEAD)

    kvp = [prompt_rows(kv_p[g], A_GROUPS[g][0]) for g in range(3)]
    kvs = [kv_s[g].reshape(1, nseq, t_new, 2, A_HPG, HEAD) for g in range(3)]
    return (y_p.reshape(bsz, seq, D_MODEL), y_s.reshape(nseq, t_new, D_MODEL),
            kvp[0], kvp[1], kvp[2], st_p[None],
            kvs[0], kvs[1], kvs[2], st_s[None])
```

```python
import functools

import numpy as np
import jax
import jax.numpy as jnp
from jax import lax
from jax.experimental import pallas as pl
from jax.experimental.pallas import tpu as pltpu

F32 = jnp.float32
BF16 = jnp.bfloat16

D_MODEL = 2048
HEAD = 128
A_GROUPS = ((128, 1), (512, 4), (2048, 16))
A_HPG = 4
A_KEYS = 128
A_WIDTH = 3 * A_HPG * HEAD
A_OUT = A_HPG * HEAD
B_HEADS = 8
B_WIDTH = B_HEADS * HEAD
B_CHUNK = 32
REL_BUCKETS = 32
REL_MAX_DIST = 2048
N_EXPERTS = 32
TOP_K = 4
D_FF = 2048
SWIGLU_ALPHA = 1.702
SWIGLU_LIMIT = 7.0
NORM_EPS = 1e-6
NEG_INF = -1e30
ATTN_SCALE = HEAD ** -0.5
IN_WIDTH = 3 * A_WIDTH + 4 * B_WIDTH + 2 * D_MODEL
OFF_Q, OFF_K, OFF_V = 0, A_WIDTH, 2 * A_WIDTH
OFF_QB = 3 * A_WIDTH
OFF_FB, OFF_IB, OFF_OB = OFF_QB + B_WIDTH, OFF_QB + 2 * B_WIDTH, OFF_QB + 3 * B_WIDTH
OFF_GA = OFF_QB + 4 * B_WIDTH
OFF_GB = OFF_GA + D_MODEL

V7X_VMEM_LIMIT = 56 * 1024 * 1024
SUBLANES = 8

NT_DIMS = (((1,), (1,)), ((), ()))


def _sigmoid(x):
    return 1.0 / (1.0 + jnp.exp(-x))


def _cparams(sem):
    return pltpu.CompilerParams(dimension_semantics=sem, vmem_limit_bytes=V7X_VMEM_LIMIT)


NORM_TM = 512
PROJ_TM = 1024
PROJ_TN = 1280


def _rmsnorm_kernel(x_ref, nw_ref, o_ref):
    x = x_ref[...]
    ms = jnp.mean(x * x, axis=-1, keepdims=True)
    o_ref[...] = (x * lax.rsqrt(ms + NORM_EPS) * nw_ref[...]).astype(BF16)


def _inproj_kernel(xn_ref, w_ref, o_ref, wbf_ref):
    @pl.when(pl.program_id(1) == 0)
    def _():
        wbf_ref[...] = w_ref[...].astype(BF16)

    o_ref[...] = jnp.dot(xn_ref[...], wbf_ref[...], preferred_element_type=F32)


def _inproj(x, norm_w, w_in):
    m = x.shape[0]
    xn = pl.pallas_call(
        _rmsnorm_kernel,
        out_shape=jax.ShapeDtypeStruct((m, D_MODEL), BF16),
        grid=(m // NORM_TM,),
        in_specs=[pl.BlockSpec((NORM_TM, D_MODEL), lambda i: (i, 0)), pl.BlockSpec((1, D_MODEL), lambda i: (0, 0))],
        out_specs=pl.BlockSpec((NORM_TM, D_MODEL), lambda i: (i, 0)),
        compiler_params=_cparams(("arbitrary",)),
        name="attn_rmsnorm",
    )(x, norm_w.reshape(1, D_MODEL))
    tm = min(PROJ_TM, m)
    return pl.pallas_call(
        _inproj_kernel,
        out_shape=jax.ShapeDtypeStruct((m, IN_WIDTH), F32),
        grid=(IN_WIDTH // PROJ_TN, m // tm),
        in_specs=[
            pl.BlockSpec((tm, D_MODEL), lambda j, i: (i, 0)),
            pl.BlockSpec((D_MODEL, PROJ_TN), lambda j, i: (0, j)),
        ],
        out_specs=pl.BlockSpec((tm, PROJ_TN), lambda j, i: (i, j)),
        scratch_shapes=[pltpu.VMEM((D_MODEL, PROJ_TN), BF16)],
        compiler_params=_cparams(("arbitrary", "arbitrary")),
        name="inproj",
    )(xn, w_in)


QKV_TM = 512


def _qkv_kernel(q_ref, k_ref, v_ref, qw_ref, kw_ref, qn_ref, kv0_ref, kv1_ref, kv2_ref):
    kv_refs = (kv0_ref, kv1_ref, kv2_ref)
    qw = qw_ref[...]
    kw = kw_ref[...]
    for h in range(3 * A_HPG):
        sl = slice(h * HEAD, (h + 1) * HEAD)
        q = q_ref[:, sl]
        qn_ref[:, sl] = q * lax.rsqrt(jnp.mean(q * q, axis=-1, keepdims=True) + NORM_EPS) * qw
        k = k_ref[:, sl]
        kn = k * lax.rsqrt(jnp.mean(k * k, axis=-1, keepdims=True) + NORM_EPS) * kw
        g, hh = divmod(h, A_HPG)
        kv_refs[g][:, hh * HEAD:(hh + 1) * HEAD] = kn
        kv_refs[g][:, A_OUT + hh * HEAD:A_OUT + (hh + 1) * HEAD] = v_ref[:, sl]


def _qkv(proj, q_norm, k_norm):
    m = proj.shape[0]
    kv_shape = jax.ShapeDtypeStruct((m, 2 * A_OUT), F32)
    return pl.pallas_call(
        _qkv_kernel,
        out_shape=(jax.ShapeDtypeStruct((m, A_WIDTH), F32), kv_shape, kv_shape, kv_shape),
        grid=(m // QKV_TM,),
        in_specs=[
            pl.BlockSpec((QKV_TM, A_WIDTH), lambda i: (i, OFF_Q // A_WIDTH)),
            pl.BlockSpec((QKV_TM, A_WIDTH), lambda i: (i, OFF_K // A_WIDTH)),
            pl.BlockSpec((QKV_TM, A_WIDTH), lambda i: (i, OFF_V // A_WIDTH)),
            pl.BlockSpec((1, HEAD), lambda i: (0, 0)),
            pl.BlockSpec((1, HEAD), lambda i: (0, 0)),
        ],
        out_specs=(
            pl.BlockSpec((QKV_TM, A_WIDTH), lambda i: (i, 0)),
            pl.BlockSpec((QKV_TM, 2 * A_OUT), lambda i: (i, 0)),
            pl.BlockSpec((QKV_TM, 2 * A_OUT), lambda i: (i, 0)),
            pl.BlockSpec((QKV_TM, 2 * A_OUT), lambda i: (i, 0)),
        ),
        compiler_params=_cparams(("arbitrary",)),
        name="qkv_norm",
    )(proj, proj, proj, q_norm.reshape(1, HEAD), k_norm.reshape(1, HEAD))


def _t5_bucket(dist):
    dist = np.asarray(dist, np.int64)
    max_exact = REL_BUCKETS // 2
    ratio = np.log(np.maximum(dist, 1) / max_exact) / np.log(REL_MAX_DIST / max_exact)
    large = np.minimum(max_exact + (ratio * (REL_BUCKETS - max_exact)).astype(np.int64), REL_BUCKETS - 1)
    return np.where(dist < max_exact, dist, large).astype(np.int32)


def _bias_lookup(tab, bucket, valid):
    bucket = jnp.asarray(np.where(valid, bucket, -1).astype(np.int32))[None]
    out = jnp.full((tab.shape[1],) + bucket.shape[1:], NEG_INF, F32)
    for b in range(REL_BUCKETS):
        out = jnp.where(bucket == b, tab[b].reshape((-1,) + (1,) * (bucket.ndim - 1)), out)
    return out


def _prompt_bias(rel_bias, g):
    _, dil = A_GROUPS[g]
    n = A_KEYS
    j = n + np.arange(n)[:, None] - np.arange(2 * n)[None, :]
    valid = (j >= 0) & (j < n)
    tab = rel_bias[:, g * A_HPG:(g + 1) * A_HPG].astype(F32)
    return _bias_lookup(tab, _t5_bucket(np.clip(j, 0, n - 1) * dil), valid)


ATT_MIX_ROWS = 256


def _attn_prompt_kernel(q0_ref, q1_ref, q2_ref, k0_ref, v0_ref, k1_ref, v1_ref, k2_ref, v2_ref, bias_ref,
                        o_ref, og_scr, lg_scr, *, seq):
    n = A_KEYS
    h = pl.program_id(1)
    q_refs = (q0_ref, q1_ref, q2_ref)
    k_refs = (k0_ref, k1_ref, k2_ref)
    v_refs = (v0_ref, v1_ref, v2_ref)
    for g, (win, dil) in enumerate(A_GROUPS):
        bias = bias_ref[g, h]
        for r in range(dil):
            for i in range(seq // win):
                def rows(blk, r=r, dil=dil):
                    if dil == 1:
                        return pl.ds(blk * n, n)
                    return pl.ds(r + blk * n * dil, n, stride=dil)

                q = q_refs[g][rows(i), :].astype(BF16)
                if i == 0:
                    kk = k_refs[g][rows(i), :].astype(BF16)
                    vv = v_refs[g][rows(i), :].astype(BF16)
                    b_tile = bias[:, n:]
                else:
                    kk = jnp.concatenate([k_refs[g][rows(i - 1), :], k_refs[g][rows(i), :]], axis=0).astype(BF16)
                    vv = jnp.concatenate([v_refs[g][rows(i - 1), :], v_refs[g][rows(i), :]], axis=0).astype(BF16)
                    b_tile = bias
                s = lax.dot_general(q, kk, NT_DIMS, preferred_element_type=F32) * ATTN_SCALE + b_tile
                mx = jnp.max(s, axis=-1, keepdims=True)
                p = jnp.exp(s - mx)
                den = jnp.sum(p, axis=-1, keepdims=True)
                og_scr[g, rows(i), :] = jnp.dot(p.astype(BF16), vv, preferred_element_type=F32) / den
                lg_scr[g, rows(i), :] = jnp.broadcast_to(mx + jnp.log(den), (n, HEAD))
    for c in range(seq // ATT_MIX_ROWS):
        sl = slice(c * ATT_MIX_ROWS, (c + 1) * ATT_MIX_ROWS)
        la, lb_, lc = lg_scr[0, sl, :], lg_scr[1, sl, :], lg_scr[2, sl, :]
        top = jnp.maximum(jnp.maximum(la, lb_), lc)
        ea, eb, ec = jnp.exp(la - top), jnp.exp(lb_ - top), jnp.exp(lc - top)
        o_ref[sl, :] = (ea * og_scr[0, sl, :] + eb * og_scr[1, sl, :] + ec * og_scr[2, sl, :]) / (ea + eb + ec)


def _attn_prompt(qn, kvs, bias, bsz, seq):
    m = qn.shape[0]
    col = lambda fn: pl.BlockSpec((seq, HEAD), lambda b, h: (b, fn(h)))
    in_specs = [col(lambda h, g=g: g * A_HPG + h) for g in range(3)]
    for _ in range(3):
        in_specs += [col(lambda h: h), col(lambda h: A_HPG + h)]
    in_specs.append(pl.BlockSpec(bias.shape, lambda b, h: (0, 0, 0, 0)))
    return pl.pallas_call(
        functools.partial(_attn_prompt_kernel, seq=seq),
        out_shape=jax.ShapeDtypeStruct((m, A_OUT), F32),
        grid=(bsz, A_HPG),
        in_specs=in_specs,
        out_specs=col(lambda h: h),
        scratch_shapes=[pltpu.VMEM((3, seq, HEAD), F32), pltpu.VMEM((3, seq, HEAD), F32)],
        compiler_params=_cparams(("arbitrary", "arbitrary")),
        name="attn_prompt",
    )(qn, qn, qn, kvs[0], kvs[0], kvs[1], kvs[1], kvs[2], kvs[2], bias)


def _hgrn_gates(q, z, lb):
    qs = q * _sigmoid(q)
    logf = jnp.log(lb + (1.0 - lb) * _sigmoid(z))
    kb = (1.0 - lb) * _sigmoid(-z)
    return qs, kb, logf


def _hgrn_out(o, gate, gn):
    on = o * lax.rsqrt(jnp.mean(o * o, axis=-1, keepdims=True) + NORM_EPS) * gn
    return (on * (gate * _sigmoid(gate))).astype(BF16)


def _split3(x):
    hi = x.astype(BF16)
    r1 = x - hi.astype(F32)
    mid = r1.astype(BF16)
    lo = (r1 - mid.astype(F32)).astype(BF16)
    return hi, mid, lo


HG_TC = 256
HG_NC = HG_TC // B_CHUNK
HG_HPS = 4


def _hgrn_prompt_kernel(q_ref, f_ref, i_ref, g_ref, lb_ref, gn_ref, ob_ref, st_ref, s_scr):
    tb = pl.program_id(2)

    @pl.when(tb == 0)
    def _():
        s_scr[...] = jnp.zeros_like(s_scr)

    row = lax.broadcasted_iota(jnp.int32, (HG_TC, HG_TC), 0)
    col = lax.broadcasted_iota(jnp.int32, (HG_TC, HG_TC), 1)
    same_chunk = (row // B_CHUNK) == (col // B_CHUNK)
    causal = jnp.logical_and(same_chunk, col <= row)
    cum_mat = jnp.where(causal, 1.0, 0.0).astype(BF16)
    tcol = lax.broadcasted_iota(jnp.int32, (HEAD, HG_TC), 1) // B_CHUNK
    trow = lax.broadcasted_iota(jnp.int32, (HG_TC, HEAD), 0) // B_CHUNK
    gn = gn_ref[...]
    for h in range(HG_HPS):
        sl = slice(h * HEAD, (h + 1) * HEAD)
        qs, kb, logf = _hgrn_gates(q_ref[:, sl], f_ref[:, sl], lb_ref[:, sl])
        b3 = jnp.dot(cum_mat, jnp.concatenate(_split3(logf), axis=1), preferred_element_type=F32)
        b = b3[:, :HEAD] + b3[:, HEAD:2 * HEAD] + b3[:, 2 * HEAD:]
        bl_rows = [b[c * B_CHUNK + B_CHUNK - 1:(c + 1) * B_CHUNK, :] for c in range(HG_NC)]
        bl = jnp.concatenate([jnp.broadcast_to(r, (B_CHUNK, HEAD)) for r in bl_rows], axis=0)
        q_dec_f = qs * jnp.exp(b)
        q_dec = q_dec_f.astype(BF16)
        k_inv = (kb * jnp.exp(-b)).astype(BF16)
        k_end_t = jnp.transpose(kb * jnp.exp(bl - b))
        v = i_ref[:, sl].astype(BF16)
        att = lax.dot_general(q_dec, k_inv, NT_DIMS, preferred_element_type=F32)
        att = jnp.where(causal, att, 0.0).astype(BF16)
        o_intra = jnp.dot(att, v, preferred_element_type=F32)
        decay_t = jnp.exp(jnp.transpose(bl))
        ke_stack = jnp.concatenate([jnp.where(tcol == c, k_end_t, 0.0) for c in range(HG_NC)], axis=0)
        ds_all = jnp.dot(ke_stack.astype(BF16), v, preferred_element_type=F32)
        s = s_scr[h]
        starts = []
        for c in range(HG_NC):
            starts.append(s)
            s = decay_t[:, c * B_CHUNK:c * B_CHUNK + 1] * s + ds_all[c * HEAD:(c + 1) * HEAD, :]
        s_scr[h] = s
        q_bd = jnp.concatenate([jnp.where(trow == c, q_dec_f, 0.0) for c in range(HG_NC)], axis=1).astype(BF16)
        o_inter = jnp.dot(q_bd, jnp.concatenate(starts, axis=0).astype(BF16), preferred_element_type=F32)
        ob_ref[:, sl] = _hgrn_out(o_intra + o_inter, g_ref[:, sl], gn)

    @pl.when(tb == pl.num_programs(2) - 1)
    def _():
        st_ref[...] = s_scr[...]


def _hgrn_prompt(proj, lb, g_norm, bsz, seq):
    m = proj.shape[0]
    hw = HG_HPS * HEAD
    nt = seq // HG_TC
    rowblk = lambda b, hf, t: b * nt + t
    spec = lambda off: pl.BlockSpec((HG_TC, hw), lambda b, hf, t: (rowblk(b, hf, t), off // hw + hf))
    return pl.pallas_call(
        _hgrn_prompt_kernel,
        out_shape=(jax.ShapeDtypeStruct((m, B_WIDTH), BF16),
                   jax.ShapeDtypeStruct((bsz, B_HEADS, HEAD, HEAD), F32)),
        grid=(bsz, B_HEADS // HG_HPS, nt),
        in_specs=[
            spec(OFF_QB), spec(OFF_FB), spec(OFF_IB), spec(OFF_OB),
            pl.BlockSpec((1, hw), lambda b, hf, t: (0, hf)),
            pl.BlockSpec((1, HEAD), lambda b, hf, t: (0, 0)),
        ],
        out_specs=(
            pl.BlockSpec((HG_TC, hw), lambda b, hf, t: (rowblk(b, hf, t), hf)),
            pl.BlockSpec((None, HG_HPS, HEAD, HEAD), lambda b, hf, t: (b, hf, 0, 0)),
        ),
        scratch_shapes=[pltpu.VMEM((HG_HPS, HEAD, HEAD), F32)],
        compiler_params=_cparams(("arbitrary", "arbitrary", "arbitrary")),
        name="hgrn_prompt",
    )(proj, proj, proj, proj, lb.reshape(1, B_WIDTH), g_norm.reshape(1, HEAD))


HS_SEQ = 32


def _hgrn_sample_kernel(q_ref, f_ref, i_ref, g_ref, lb_ref, gn_ref, s0_ref, ob_ref, st_ref, *, t_new):
    rows = HS_SEQ * t_new
    qs, kb, logf = _hgrn_gates(q_ref[...], f_ref[...], lb_ref[...])
    t_of = lax.broadcasted_iota(jnp.int32, (rows, HEAD), 0) % t_new
    b = logf
    sh = 1
    while sh < t_new:
        b = b + jnp.where(t_of >= sh, pltpu.roll(b, sh, axis=0), 0.0)
        sh *= 2
    bl = b
    for back in range(1, t_new):
        bl = jnp.where(t_of == t_new - 1 - back, pltpu.roll(b, rows - back, axis=0), bl)
    q_dec = (qs * jnp.exp(b)).astype(BF16)
    k_inv = (kb * jnp.exp(-b)).astype(BF16)
    k_end_t = jnp.transpose(kb * jnp.exp(bl - b))
    decay_t = jnp.exp(jnp.transpose(bl))
    v = i_ref[...].astype(BF16)
    row = lax.broadcasted_iota(jnp.int32, (rows, rows), 0)
    col = lax.broadcasted_iota(jnp.int32, (rows, rows), 1)
    causal = jnp.logical_and(row // t_new == col // t_new, col <= row)
    att = lax.dot_general(q_dec, k_inv, NT_DIMS, preferred_element_type=F32)
    att = jnp.where(causal, att, 0.0).astype(BF16)
    o_intra = jnp.dot(att, v, preferred_element_type=F32)
    seq_of_col = lax.broadcasted_iota(jnp.int32, (HEAD, rows), 1) // t_new
    seq_of_row = lax.broadcasted_iota(jnp.int32, (rows, HEAD), 0) // t_new
    o_inter = jnp.zeros((rows, HEAD), F32)
    for s_i in range(HS_SEQ):
        s = s0_ref[s_i, 0]
        part = jnp.dot(q_dec, s.astype(BF16), preferred_element_type=F32)
        o_inter = jnp.where(seq_of_row == s_i, part, o_inter)
        ke = jnp.where(seq_of_col == s_i, k_end_t, 0.0).astype(BF16)
        ds = jnp.dot(ke, v, preferred_element_type=F32)
        st_ref[s_i, 0] = decay_t[:, s_i * t_new:s_i * t_new + 1] * s + ds
    ob_ref[...] = _hgrn_out(o_intra + o_inter, g_ref[...], gn_ref[...])


def _hgrn_sample(proj, lb, g_norm, s0, t_new):
    m = proj.shape[0]
    nseq = m // t_new
    rows = HS_SEQ * t_new
    spec = lambda off: pl.BlockSpec((rows, HEAD), lambda i, h: (i, off // HEAD + h))
    st_spec = pl.BlockSpec((HS_SEQ, 1, HEAD, HEAD), lambda i, h: (i, h, 0, 0))
    return pl.pallas_call(
        functools.partial(_hgrn_sample_kernel, t_new=t_new),
        out_shape=(jax.ShapeDtypeStruct((m, B_WIDTH), BF16),
                   jax.ShapeDtypeStruct((nseq, B_HEADS, HEAD, HEAD), F32)),
        grid=(nseq // HS_SEQ, B_HEADS),
        in_specs=[
            spec(OFF_QB), spec(OFF_FB), spec(OFF_IB), spec(OFF_OB),
            pl.BlockSpec((1, HEAD), lambda i, h: (0, h)),
            pl.BlockSpec((1, HEAD), lambda i, h: (0, 0)),
            st_spec,
        ],
        out_specs=(pl.BlockSpec((rows, HEAD), lambda i, h: (i, h)), st_spec),
        compiler_params=_cparams(("arbitrary", "arbitrary")),
        name="hgrn_sample",
    )(proj, proj, proj, proj, lb.reshape(1, B_WIDTH), g_norm.reshape(1, HEAD), s0)


SA_SEQ = 2
KV_ROWS = 2 * A_HPG


def _sample_bias(rel_bias, t_new):
    def tile(tab, dist, valid):
        bias = _bias_lookup(tab, _t5_bucket(np.clip(dist, 0, REL_MAX_DIST)), valid)
        bias = jnp.moveaxis(bias, 0, -1)
        bias = jnp.concatenate([bias, jnp.zeros(bias.shape[:-1] + (KV_ROWS - A_HPG,), F32)], axis=-1)
        return jnp.broadcast_to(bias[..., None], bias.shape + (HEAD,))

    t = np.arange(t_new)[:, None]
    r = np.arange(A_KEYS + t_new)[None, :]
    is_new = r >= A_KEYS
    dist = np.where(is_new, t - (r - A_KEYS), A_KEYS + t - r)
    valid = (dist >= 0) & (dist < A_KEYS)
    b0 = tile(rel_bias[:, 0:A_HPG].astype(F32), dist, valid)
    b12 = []
    for g in (1, 2):
        _, dil = A_GROUPS[g]
        m = np.arange(A_KEYS + 1)
        dist = dil * (A_KEYS - m)
        b12.append(tile(rel_bias[:, g * A_HPG:(g + 1) * A_HPG].astype(F32), dist, m >= 1))
    return b0, jnp.stack(b12)


def _attn_sample_kernel(q_ref, n0_ref, n1_ref, n2_ref, c0_ref, c1_ref, c2_ref, b0_ref, b12_ref,
                        o_ref, qt_scr, ns_scr, *, t_new):
    nq = SA_SEQ * t_new
    new_refs = (n0_ref, n1_ref, n2_ref)
    cache_refs = (c0_ref, c1_ref, c2_ref)
    qt_scr[...] = jnp.zeros_like(qt_scr)
    for g in range(3):
        for j in range(nq):
            for h in range(A_HPG):
                qt_scr[g, j, h:h + 1, :] = (
                    q_ref[j:j + 1, g * A_OUT + h * HEAD:g * A_OUT + (h + 1) * HEAD] * ATTN_SCALE)
                ns_scr[g, j, h:h + 1, :] = new_refs[g][j:j + 1, h * HEAD:(h + 1) * HEAD]
                ns_scr[g, j, A_HPG + h:A_HPG + h + 1, :] = (
                    new_refs[g][j:j + 1, A_OUT + h * HEAD:A_OUT + (h + 1) * HEAD])

    row_id = lax.broadcasted_iota(jnp.int32, (nq, HEAD), 0)
    ones = jnp.ones((HEAD, HEAD), BF16)

    def per_query(j, out):
        s = j // t_new
        t = j - s * t_new
        row0 = s * A_KEYS
        stats = []
        for g in range(3):
            c_ref = cache_refs[g]
            qt = qt_scr[g, j]
            if g == 0:
                sub0 = 0
                bias_c = b0_ref[t, 0:A_KEYS]
                new_tiles = [(b0_ref[t, A_KEYS + tp], ns_scr[0, s * t_new + tp]) for tp in range(t_new)]
            else:
                sub0 = pl.multiple_of(t * KV_ROWS, KV_ROWS)
                bias_c = b12_ref[g - 1, 0:A_KEYS]
                new_tiles = [(b12_ref[g - 1, A_KEYS], ns_scr[g, j])]

            def mirror(l):
                return l + pltpu.roll(l, A_HPG, axis=l.ndim - 2)

            tiles = c_ref[pl.ds(row0, A_KEYS), pl.ds(sub0, KV_ROWS), :]
            prod = (tiles * qt[None]).reshape(A_KEYS * KV_ROWS, HEAD).astype(BF16)
            sums = jnp.dot(prod, ones, preferred_element_type=F32).reshape(A_KEYS, KV_ROWS, HEAD)
            l_c = mirror(sums + bias_c)
            new_l = [mirror(jnp.sum(tile_kv * qt, axis=-1, keepdims=True) + bias) for bias, tile_kv in new_tiles]
            mx = jnp.max(l_c, axis=0)
            for l in new_l:
                mx = jnp.maximum(mx, l)
            p_c = jnp.exp(l_c - mx[None])
            den = jnp.sum(p_c, axis=0)
            acc = jnp.sum(p_c * tiles, axis=0)
            for l, (_, tile_kv) in zip(new_l, new_tiles):
                p = jnp.exp(l - mx)
                den = den + p
                acc = acc + p * tile_kv
            stats.append((mx + jnp.log(den), den, acc))
        top = jnp.maximum(jnp.maximum(stats[0][0], stats[1][0]), stats[2][0])
        e = [jnp.exp(st[0] - top) for st in stats]
        tot = e[0] + e[1] + e[2]
        mix = jnp.zeros((KV_ROWS, HEAD), F32)
        for g in range(3):
            mix = mix + e[g] / (tot * stats[g][1]) * stats[g][2]
        pieces = [jnp.broadcast_to(mix[A_HPG + h:A_HPG + h + 1, :], (nq, HEAD)) for h in range(A_HPG)]
        return tuple(jnp.where(row_id == j, pieces[h], out[h]) for h in range(A_HPG))

    init = tuple(jnp.zeros((nq, HEAD), F32) for _ in range(A_HPG))
    out = lax.fori_loop(0, nq, per_query, init)
    for h in range(A_HPG):
        o_ref[:, h * HEAD:(h + 1) * HEAD] = out[h]


def _attn_sample(qn, kv_new, caches, rel_bias, t_new):
    m = qn.shape[0]
    nseq = m // t_new
    nq = SA_SEQ * t_new
    assert nq % SUBLANES == 0, "the new-token rows of one grid step must fill whole sublane tiles"
    kvw = 2 * A_OUT
    views, specs = [], []
    for g, (win, dil) in enumerate(A_GROUPS):
        c = caches[g]
        assert c.shape[1] == win, "cache must hold exactly one window of positions"
        assert g == 0 or dil >= t_new, "each new token must sit in its own residue class"
        views.append(c.reshape(nseq * A_KEYS, dil * KV_ROWS, HEAD))
        specs.append(pl.BlockSpec((SA_SEQ * A_KEYS, min(dil, t_new) * KV_ROWS, HEAD), lambda i: (i, 0, 0)))
    b0, b12 = _sample_bias(rel_bias, t_new)
    new_spec = pl.BlockSpec((nq, kvw), lambda i: (i, 0))
    return pl.pallas_call(
        functools.partial(_attn_sample_kernel, t_new=t_new),
        out_shape=jax.ShapeDtypeStruct((m, A_OUT), F32),
        grid=(nseq // SA_SEQ,),
        in_specs=[
            pl.BlockSpec((nq, A_WIDTH), lambda i: (i, 0)),
            new_spec, new_spec, new_spec,
            specs[0], specs[1], specs[2],
            pl.BlockSpec(b0.shape, lambda i: (0, 0, 0, 0)),
            pl.BlockSpec(b12.shape, lambda i: (0, 0, 0, 0)),
        ],
        out_specs=pl.BlockSpec((nq, A_OUT), lambda i: (i, 0)),
        scratch_shapes=[
            pltpu.VMEM((3, nq, KV_ROWS, HEAD), F32),
            pltpu.VMEM((3, nq, KV_ROWS, HEAD), F32),
        ],
        compiler_params=_cparams(("arbitrary",)),
        name="attn_sample",
    )(qn, kv_new[0], kv_new[1], kv_new[2], views[0], views[1], views[2], b0, b12)


MIX_TM = 512
MIX_TN = 512


def _mix_kernel(oa_ref, ob_ref, ga_ref, gb_ref, wpa_ref, wpb_ref, out_ref):
    pa = jnp.dot(oa_ref[...].astype(BF16), wpa_ref[...], preferred_element_type=F32)
    pb = jnp.dot(ob_ref[...], wpb_ref[...], preferred_element_type=F32)
    out_ref[...] = (_sigmoid(ga_ref[...]) * pa + _sigmoid(gb_ref[...]) * pb).astype(BF16)


def _mix(oa, ob, proj, w_pa, w_pb):
    m = proj.shape[0]
    return pl.pallas_call(
        _mix_kernel,
        out_shape=jax.ShapeDtypeStruct((m, D_MODEL), BF16),
        grid=(m // MIX_TM, D_MODEL // MIX_TN),
        in_specs=[
            pl.BlockSpec((MIX_TM, A_OUT), lambda i, j: (i, 0)),
            pl.BlockSpec((MIX_TM, B_WIDTH), lambda i, j: (i, 0)),
            pl.BlockSpec((MIX_TM, MIX_TN), lambda i, j: (i, OFF_GA // MIX_TN + j)),
            pl.BlockSpec((MIX_TM, MIX_TN), lambda i, j: (i, OFF_GB // MIX_TN + j)),
            pl.BlockSpec((A_OUT, MIX_TN), lambda i, j: (0, j)),
            pl.BlockSpec((B_WIDTH, MIX_TN), lambda i, j: (0, j)),
        ],
        out_specs=pl.BlockSpec((MIX_TM, MIX_TN), lambda i, j: (i, j)),
        compiler_params=_cparams(("arbitrary", "arbitrary")),
        name="mix",
    )(oa, ob, proj, proj, w_pa, w_pb)


RT_LANES = 128


def _resid_kernel(x_ref, mixed_ref, wo_ref, nw_ref, wr_hi_ref, wr_lo_ref, br_ref, h_ref, hn_ref, lg_ref):
    h = x_ref[...] + jnp.dot(mixed_ref[...], wo_ref[...], preferred_element_type=F32)
    h_ref[...] = h
    hn = h * lax.rsqrt(jnp.mean(h * h, axis=-1, keepdims=True) + NORM_EPS) * nw_ref[...]
    hi = hn.astype(BF16)
    hn_ref[...] = hi
    lo = (hn - hi.astype(F32)).astype(BF16)
    lg = (jnp.dot(hi, wr_hi_ref[...], preferred_element_type=F32)
          + jnp.dot(lo, wr_hi_ref[...], preferred_element_type=F32)
          + jnp.dot(hi, wr_lo_ref[...], preferred_element_type=F32))
    lg_ref[...] = lg + br_ref[...]


def _resid(x, mixed, w_o, ffn_norm, wr_hi, wr_lo, br):
    m = x.shape[0]
    full = lambda shape: pl.BlockSpec(shape, lambda i: (0, 0))
    return pl.pallas_call(
        _resid_kernel,
        out_shape=(jax.ShapeDtypeStruct((m, D_MODEL), F32),
                   jax.ShapeDtypeStruct((m, D_MODEL), BF16),
                   jax.ShapeDtypeStruct((m, RT_LANES), F32)),
        grid=(m // MIX_TM,),
        in_specs=[
            pl.BlockSpec((MIX_TM, D_MODEL), lambda i: (i, 0)),
            pl.BlockSpec((MIX_TM, D_MODEL), lambda i: (i, 0)),
            full((D_MODEL, D_MODEL)), full((1, D_MODEL)),
            full((D_MODEL, RT_LANES)), full((D_MODEL, RT_LANES)), full((1, RT_LANES)),
        ],
        out_specs=(
            pl.BlockSpec((MIX_TM, D_MODEL), lambda i: (i, 0)),
            pl.BlockSpec((MIX_TM, D_MODEL), lambda i: (i, 0)),
            pl.BlockSpec((MIX_TM, RT_LANES), lambda i: (i, 0)),
        ),
        compiler_params=_cparams(("arbitrary",)),
        name="resid_router",
    )(x, mixed, w_o, ffn_norm.reshape(1, D_MODEL), wr_hi, wr_lo, br)


MOE_RB = 1280
MOE_SUB = 128
MOE_TF = 256
MOE_NF = D_FF // MOE_TF


MOE_TM = 512
MOE_GRAN = 16
MOE_SLOTS = 2560
MOE_CHUNK = 256
assert MOE_SLOTS >= MOE_TM * TOP_K + N_EXPERTS * (MOE_GRAN - 1) and MOE_SLOTS % MOE_CHUNK == 0


def _route(logits):
    n_tok = logits.shape[0]
    nt = n_tok // MOE_TM
    top_val, top_idx = lax.top_k(logits, TOP_K)
    gate = jax.nn.softmax(top_val, axis=-1)
    na = MOE_TM * TOP_K
    e3 = top_idx.reshape(nt, na)
    hot = e3[..., None] == jnp.arange(N_EXPERTS)
    onehot = hot.astype(jnp.int32)
    lower = (lax.broadcasted_iota(jnp.int32, (na, na), 0) >= lax.broadcasted_iota(jnp.int32, (na, na), 1))
    incl = jnp.einsum('ij,tjk->tik', lower.astype(BF16), hot.astype(BF16),
                      preferred_element_type=F32).astype(jnp.int32)
    rank = jnp.sum((incl - onehot) * onehot, axis=2)
    seg = (incl[:, -1] + MOE_GRAN - 1) // MOE_GRAN * MOE_GRAN
    loc_off = jnp.cumsum(seg, axis=1) - seg
    rows_e = jnp.sum(seg, axis=0)
    padded_sub = (rows_e + MOE_SUB - 1) // MOE_SUB * MOE_SUB
    padded_rb = (rows_e + MOE_RB - 1) // MOE_RB * MOE_RB
    blk_end = jnp.cumsum(padded_rb) // MOE_RB
    row_start = jnp.cumsum(padded_rb) - padded_rb
    seg_start = row_start[None, :] + jnp.cumsum(seg, axis=0) - seg
    slot = jnp.sum(onehot * loc_off[:, None, :], axis=2) + rank
    max_rows = n_tok * TOP_K + nt * N_EXPERTS * (MOE_GRAN - 1)
    n_blk = max_rows // MOE_RB + N_EXPERTS
    blk_ids = jnp.arange(n_blk)
    blk_e = jnp.minimum(jnp.searchsorted(blk_end, blk_ids, side='right'), N_EXPERTS - 1).astype(jnp.int32)
    rows_left = padded_sub[blk_e] - (blk_ids * MOE_RB - row_start[blk_e])
    blk_nsub = jnp.clip(rows_left // MOE_SUB, 0, MOE_RB // MOE_SUB).astype(jnp.int32)
    n_used = blk_end[-1:].astype(jnp.int32)
    tail_row = jnp.where(rows_e > 0, row_start + padded_sub - MOE_SUB, -1).astype(jnp.int32)
    layout = dict(seg_start=seg_start.reshape(-1).astype(jnp.int32),
                  seg_gran=(seg // MOE_GRAN).reshape(-1).astype(jnp.int32),
                  loc_off=loc_off.reshape(-1).astype(jnp.int32),
                  tail_row=tail_row)
    slot = slot.reshape(nt, MOE_TM, TOP_K).astype(jnp.int32)
    return blk_e, blk_nsub, n_used, layout, slot, gate.reshape(nt, MOE_TM, TOP_K), n_blk


def _segment_copies(layout_refs, tile, src_of, dst_of, sem, start):
    seg_start_ref, seg_gran_ref, loc_off_ref = layout_refs

    for e in range(N_EXPERTS):
        idx = tile * N_EXPERTS + e
        loc = loc_off_ref[idx]
        glob = seg_start_ref[idx]

        def body(c, carry, loc=loc, glob=glob):
            copy = pltpu.make_async_copy(
                src_of(pl.multiple_of(loc + c * MOE_GRAN, MOE_GRAN), pl.multiple_of(glob + c * MOE_GRAN, MOE_GRAN)),
                dst_of(pl.multiple_of(loc + c * MOE_GRAN, MOE_GRAN), pl.multiple_of(glob + c * MOE_GRAN, MOE_GRAN)),
                sem)
            if start:
                copy.start()
            else:
                copy.wait()
            return carry

        lax.fori_loop(0, seg_gran_ref[idx], body, 0)


def _slot_onehot(slot_vecs, base, shape, axis):
    ids = base + lax.broadcasted_iota(jnp.int32, shape, axis)
    hit = jnp.zeros(shape, F32)
    for v in slot_vecs:
        hit = jnp.where(ids == v, 1.0, hit)
    return hit.astype(BF16)


def _dispatch_kernel(seg_start_ref, seg_gran_ref, loc_off_ref, tail_ref, slot_ref, gate_ref, x_ref,
                     xs_ref, gs_ref, sorted_scr, gsorted_scr, zero_scr, gzero_scr, sem, buf_sem):
    i = pl.program_id(0)

    @pl.when(i == 0)
    def _():
        zero_scr[...] = jnp.zeros_like(zero_scr)
        gzero_scr[...] = jnp.zeros_like(gzero_scr)

        def tail_copies(e):
            rows = pl.ds(pl.multiple_of(jnp.maximum(tail_ref[e], 0), MOE_SUB), MOE_SUB)
            return (pltpu.make_async_copy(zero_scr, xs_ref.at[rows], sem),
                    pltpu.make_async_copy(gzero_scr, gs_ref.at[rows], sem))

        for e in range(N_EXPERTS):
            @pl.when(tail_ref[e] >= 0)
            def _():
                for copy in tail_copies(e):
                    copy.start()
        for e in range(N_EXPERTS):
            @pl.when(tail_ref[e] >= 0)
            def _():
                for copy in tail_copies(e):
                    copy.wait()

    buf = lax.rem(i, 2)
    x = x_ref[...]
    slot_vecs = [slot_ref[k:k + 1, :] for k in range(TOP_K)]
    for c in range(MOE_SLOTS // MOE_CHUNK):
        chunk = slice(c * MOE_CHUNK, (c + 1) * MOE_CHUNK)
        ids = c * MOE_CHUNK + lax.broadcasted_iota(jnp.int32, (MOE_CHUNK, MOE_TM), 0)
        hit = jnp.zeros((MOE_CHUNK, MOE_TM), F32)
        gval = jnp.zeros((MOE_CHUNK, MOE_TM), F32)
        for k in range(TOP_K):
            match = ids == slot_vecs[k]
            hit = jnp.where(match, 1.0, hit)
            gval = jnp.where(match, gate_ref[k:k + 1, :], gval)
        sorted_scr[buf, chunk, :] = jnp.dot(hit.astype(BF16), x,
                                            preferred_element_type=F32).astype(BF16)
        gsorted_scr[buf, chunk, :] = jnp.broadcast_to(jnp.sum(gval, axis=1, keepdims=True), (MOE_CHUNK, HEAD))

    layout_refs = (seg_start_ref, seg_gran_ref, loc_off_ref)

    def copies(tile, b, start):
        streams = (
            (lambda loc, glob: sorted_scr.at[b, pl.ds(loc, MOE_GRAN)],
             lambda loc, glob: xs_ref.at[pl.ds(glob, MOE_GRAN)]),
            (lambda loc, glob: gsorted_scr.at[b, pl.ds(loc, MOE_GRAN)],
             lambda loc, glob: gs_ref.at[pl.ds(glob, MOE_GRAN)]),
        )
        for src_of, dst_of in streams:
            _segment_copies(layout_refs, tile, src_of, dst_of, buf_sem.at[b], start=start)

    copies(i, buf, True)

    @pl.when(i > 0)
    def _():
        copies(i - 1, 1 - buf, False)

    @pl.when(i == pl.num_programs(0) - 1)
    def _():
        copies(i, buf, False)


def _dispatch(layout, slot, gate, hn, n_rows):
    n_tok = hn.shape[0]
    slot_t = jnp.swapaxes(slot, 1, 2)
    gate_t = jnp.swapaxes(gate, 1, 2)
    grid_spec = pltpu.PrefetchScalarGridSpec(
        num_scalar_prefetch=4,
        grid=(n_tok // MOE_TM,),
        in_specs=[
            pl.BlockSpec((None, TOP_K, MOE_TM), lambda i, *_: (i, 0, 0)),
            pl.BlockSpec((None, TOP_K, MOE_TM), lambda i, *_: (i, 0, 0)),
            pl.BlockSpec((MOE_TM, D_MODEL), lambda i, *_: (i, 0)),
        ],
        out_specs=(pl.BlockSpec(memory_space=pl.ANY), pl.BlockSpec(memory_space=pl.ANY)),
        scratch_shapes=[
            pltpu.VMEM((2, MOE_SLOTS, D_MODEL), BF16),
            pltpu.VMEM((2, MOE_SLOTS, HEAD), F32),
            pltpu.VMEM((MOE_SUB, D_MODEL), BF16),
            pltpu.VMEM((MOE_SUB, HEAD), F32),
            pltpu.SemaphoreType.DMA,
            pltpu.SemaphoreType.DMA((2,)),
        ],
    )
    return pl.pallas_call(
        _dispatch_kernel,
        out_shape=(jax.ShapeDtypeStruct((n_rows, D_MODEL), BF16), jax.ShapeDtypeStruct((n_rows, HEAD), F32)),
        grid_spec=grid_spec,
        compiler_params=_cparams(("arbitrary",)),
        name="moe_dispatch",
    )(layout["seg_start"], layout["seg_gran"], layout["loc_off"], layout["tail_row"], slot_t, gate_t, hn)


def _moe_kernel(blk_e_ref, blk_nsub_ref, n_used_ref, xs_ref, gate_ref, wg_ref, wl_ref, bg_ref, bl_ref,
                wd_ref, bd_ref, yb_ref, wu_s, wd_s, act_s, y_s, pend_ref, sem):
    i = pl.program_id(0)
    f = pl.program_id(1)
    last_step = jnp.logical_and(i == pl.num_programs(0) - 1, f == MOE_NF - 1)

    def y_copies(blk, sb):
        rows = pl.ds(pl.multiple_of(sb * MOE_SUB, MOE_SUB), MOE_SUB)
        dst = pl.ds(pl.multiple_of(blk * MOE_RB + sb * MOE_SUB, MOE_SUB), MOE_SUB)
        return [pltpu.make_async_copy(y_s.at[part, rows], yb_ref.at[part, dst], sem) for part in range(2)]

    def drain():
        blk = pend_ref[1]

        def body(sb, c):
            for copy in y_copies(blk, sb):
                copy.wait()
            return c

        lax.fori_loop(0, pend_ref[0], body, 0)
        pend_ref[0] = 0

    @pl.when(jnp.logical_and(i == 0, f == 0))
    def _():
        pend_ref[0] = 0
        pend_ref[1] = 0

    def over_granules(nsub, fn):
        def octet(sb, c):
            fn(sb * (8 * MOE_SUB), 4 * MOE_SUB)
            fn(sb * (8 * MOE_SUB) + 4 * MOE_SUB, 4 * MOE_SUB)
            return c

        lax.fori_loop(0, nsub // 8, octet, 0)
        rem8 = nsub % 8
        rem = nsub % 4
        done = nsub - rem

        @pl.when(rem8 >= 4)
        def _():
            fn((nsub - rem8) * MOE_SUB, 4 * MOE_SUB)

        @pl.when(rem >= 2)
        def _():
            fn(done * MOE_SUB, 2 * MOE_SUB)

        @pl.when(rem % 2 == 1)
        def _():
            fn((nsub - 1) * MOE_SUB, MOE_SUB)

    @pl.when(i < n_used_ref[0])
    def _():
        wu_s[:, :MOE_TF] = wg_ref[...].astype(BF16)
        wu_s[:, MOE_TF:] = wl_ref[...].astype(BF16)
        wd_s[pl.ds(pl.multiple_of(f * MOE_TF, MOE_TF), MOE_TF), :] = wd_ref[...].astype(BF16)
        b_up = jnp.concatenate([bg_ref[...], bl_ref[...]], axis=1)
        nsub = blk_nsub_ref[i]

        def up(start, size):
            rows = pl.ds(pl.multiple_of(start, MOE_SUB), size)
            hh = jnp.dot(xs_ref[rows, :], wu_s[...], preferred_element_type=F32) + b_up
            glu = jnp.minimum(hh[:, :MOE_TF], SWIGLU_LIMIT)
            lin = jnp.clip(hh[:, MOE_TF:], -SWIGLU_LIMIT, SWIGLU_LIMIT)
            act_s[f, rows, :] = (glu * _sigmoid(SWIGLU_ALPHA * glu) * (lin + 1.0)).astype(BF16)

        over_granules(nsub, up)

        @pl.when(f == MOE_NF - 1)
        def _():
            drain()

            def down(start, size):
                rows = pl.ds(pl.multiple_of(start, MOE_SUB), size)
                act = jnp.concatenate([act_s[k, rows, :] for k in range(MOE_NF)], axis=1)
                y = (jnp.dot(act, wd_s[...], preferred_element_type=F32) + bd_ref[...]) * gate_ref[rows, 0:1]
                hi = y.astype(BF16)
                y_s[0, rows, :] = hi
                y_s[1, rows, :] = (y - hi.astype(F32)).astype(BF16)

            over_granules(nsub, down)

            def send(sb, c):
                for copy in y_copies(i, sb):
                    copy.start()
                return c

            lax.fori_loop(0, nsub, send, 0)
            pend_ref[0] = nsub
            pend_ref[1] = i

    @pl.when(last_step)
    def _():
        drain()


def _moe_experts(blk_e, blk_nsub, n_used, xs, gate_rows, w_up, b_up, w_down, b_down):
    n_blk = xs.shape[0] // MOE_RB

    def live(i, n_used_ref):
        return jnp.minimum(i, n_used_ref[0] - 1)

    def ff(i, f, n_used_ref):
        return jnp.where(i < n_used_ref[0], f, MOE_NF - 1)

    def expert(i, be, nu):
        return be[live(i, nu)]

    grid_spec = pltpu.PrefetchScalarGridSpec(
        num_scalar_prefetch=3,
        grid=(n_blk, MOE_NF),
        in_specs=[
            pl.BlockSpec((MOE_RB, D_MODEL), lambda i, f, be, ns, nu: (live(i, nu), 0)),
            pl.BlockSpec((MOE_RB, HEAD), lambda i, f, be, ns, nu: (live(i, nu), 0)),
            pl.BlockSpec((None, D_MODEL, MOE_TF), lambda i, f, be, ns, nu: (expert(i, be, nu), 0, ff(i, f, nu))),
            pl.BlockSpec((None, D_MODEL, MOE_TF),
                         lambda i, f, be, ns, nu: (expert(i, be, nu), 0, ff(i, f, nu) + MOE_NF)),
            pl.BlockSpec((None, 1, MOE_TF), lambda i, f, be, ns, nu: (expert(i, be, nu), 0, ff(i, f, nu))),
            pl.BlockSpec((None, 1, MOE_TF),
                         lambda i, f, be, ns, nu: (expert(i, be, nu), 0, ff(i, f, nu) + MOE_NF)),
            pl.BlockSpec((None, MOE_TF, D_MODEL), lambda i, f, be, ns, nu: (expert(i, be, nu), ff(i, f, nu), 0)),
            pl.BlockSpec((None, 1, D_MODEL), lambda i, f, be, ns, nu: (expert(i, be, nu), 0, 0)),
        ],
        out_specs=pl.BlockSpec(memory_space=pl.ANY),
        scratch_shapes=[
            pltpu.VMEM((D_MODEL, 2 * MOE_TF), BF16),
            pltpu.VMEM((D_FF, D_MODEL), BF16),
            pltpu.VMEM((MOE_NF, MOE_RB, MOE_TF), BF16),
            pltpu.VMEM((2, MOE_RB, D_MODEL), BF16),
            pltpu.SMEM((2,), jnp.int32),
            pltpu.SemaphoreType.DMA,
        ],
    )
    b_up3 = b_up.reshape(N_EXPERTS, 1, 2 * D_FF)
    return pl.pallas_call(
        _moe_kernel,
        out_shape=jax.ShapeDtypeStruct((2,) + xs.shape, BF16),
        grid_spec=grid_spec,
        compiler_params=_cparams(("arbitrary", "arbitrary")),
        name="moe_experts",
    )(blk_e, blk_nsub, n_used, xs, gate_rows, w_up, w_up, b_up3, b_up3, w_down,
      b_down.reshape(N_EXPERTS, 1, D_MODEL))


def _combine_kernel(seg_start_ref, seg_gran_ref, loc_off_ref, slot_ref, h_ref, yb_ref, y_ref, rows_scr, sem, *,
                    tile0):
    i = pl.program_id(0)

    @pl.when(i == 0)
    def _():
        rows_scr[...] = jnp.zeros_like(rows_scr)

    tile = tile0 + i
    per_chunk = MOE_CHUNK // MOE_GRAN
    last = tile * N_EXPERTS + N_EXPERTS - 1
    n_gran = loc_off_ref[last] // MOE_GRAN + seg_gran_ref[last]
    for e in range(N_EXPERTS):
        idx = tile * N_EXPERTS + e
        loc0 = loc_off_ref[idx]
        glob0 = seg_start_ref[idx]

        def issue(c, carry, loc0=loc0, glob0=glob0):
            loc = pl.multiple_of(loc0 + c * MOE_GRAN, MOE_GRAN)
            glob = pl.multiple_of(glob0 + c * MOE_GRAN, MOE_GRAN)
            for part in range(2):
                pltpu.make_async_copy(yb_ref.at[part, pl.ds(glob, MOE_GRAN)],
                                      rows_scr.at[part, pl.ds(loc, MOE_GRAN)], sem.at[loc // MOE_CHUNK]).start()
            return carry

        lax.fori_loop(0, seg_gran_ref[idx], issue, 0)

    slot_vecs = [slot_ref[:, k:k + 1] for k in range(TOP_K)]
    n_chunks = MOE_SLOTS // MOE_CHUNK
    y_ref[...] = h_ref[...]
    for half in (range(0, n_chunks // 2), range(n_chunks // 2, n_chunks)):
        for c in half:
            def landed(g, carry, c=c):
                for part in range(2):
                    pltpu.make_async_copy(yb_ref.at[part, pl.ds(0, MOE_GRAN)],
                                          rows_scr.at[part, pl.ds(c * MOE_CHUNK, MOE_GRAN)], sem.at[c]).wait()
                return carry

            lax.fori_loop(0, jnp.clip(n_gran - c * per_chunk, 0, per_chunk), landed, 0)
        for c in half:
            pick = _slot_onehot(slot_vecs, c * MOE_CHUNK, (MOE_TM, MOE_CHUNK), 1)
            chunk = slice(c * MOE_CHUNK, (c + 1) * MOE_CHUNK)
            y_ref[...] += (jnp.dot(pick, rows_scr[0, chunk, :], preferred_element_type=F32)
                           + jnp.dot(pick, rows_scr[1, chunk, :], preferred_element_type=F32))


def _combine(layout, slot, h, yb, tile0):
    m = h.shape[0]
    grid_spec = pltpu.PrefetchScalarGridSpec(
        num_scalar_prefetch=3,
        grid=(m // MOE_TM,),
        in_specs=[
            pl.BlockSpec((None, MOE_TM, TOP_K), lambda i, *_: (i + tile0, 0, 0)),
            pl.BlockSpec((MOE_TM, D_MODEL), lambda i, *_: (i, 0)),
            pl.BlockSpec(memory_space=pl.ANY),
        ],
        out_specs=pl.BlockSpec((MOE_TM, D_MODEL), lambda i, *_: (i, 0)),
        scratch_shapes=[pltpu.VMEM((2, MOE_SLOTS, D_MODEL), BF16),
                        pltpu.SemaphoreType.DMA((MOE_SLOTS // MOE_CHUNK,))],
    )
    return pl.pallas_call(
        functools.partial(_combine_kernel, tile0=tile0),
        out_shape=jax.ShapeDtypeStruct((m, D_MODEL), F32),
        grid_spec=grid_spec,
        compiler_params=_cparams(("arbitrary",)),
        name="moe_combine",
    )(layout["seg_start"], layout["seg_gran"], layout["loc_off"], slot, h, yb)


def kernel(x_prompt, x_sample, cache_kv_w128, cache_kv_w512, cache_kv_w2048, state_hgrn, rel_bias, attn_norm,
           w_in, q_norm, k_norm, lb_raw, g_norm, w_pa, w_pb, w_o, ffn_norm, w_router, b_router, w_up, b_up,
           w_down, b_down):
    assert attn_norm.shape[0] == 1, "single-layer stack"
    bsz, seq, _ = x_prompt.shape
    nseq, t_new, _ = x_sample.shape
    n_p, n_s = bsz * seq, nseq * t_new
    lb = jax.nn.softmax(lb_raw.astype(F32), axis=0)[0]
    wpa, wpb, wo = w_pa[0].astype(BF16), w_pb[0].astype(BF16), w_o[0].astype(BF16)
    wr = jnp.pad(w_router[0].astype(F32), ((0, 0), (0, RT_LANES - N_EXPERTS)))
    wr_hi = wr.astype(BF16)
    wr_lo = (wr - wr_hi.astype(F32)).astype(BF16)
    br = jnp.pad(b_router[0].astype(F32), (0, RT_LANES - N_EXPERTS)).reshape(1, RT_LANES)
    caches = (cache_kv_w128[0], cache_kv_w512[0], cache_kv_w2048[0])

    xp = x_prompt.reshape(n_p, D_MODEL)
    xs = x_sample.reshape(n_s, D_MODEL)

    proj_p = _inproj(xp, attn_norm[0], w_in[0])
    qn_p, *kv_p = _qkv(proj_p, q_norm[0], k_norm[0])
    oa_p = _attn_prompt(qn_p, kv_p, jnp.stack([_prompt_bias(rel_bias, g) for g in range(3)]), bsz, seq)
    ob_p, st_p = _hgrn_prompt(proj_p, lb, g_norm[0], bsz, seq)
    mixed_p = _mix(oa_p, ob_p, proj_p, wpa, wpb)
    h_p, hn_p, lg_p = _resid(xp, mixed_p, wo, ffn_norm[0], wr_hi, wr_lo, br)

    proj_s = _inproj(xs, attn_norm[0], w_in[0])
    qn_s, *kv_s = _qkv(proj_s, q_norm[0], k_norm[0])
    oa_s = _attn_sample(qn_s, kv_s, caches, rel_bias, t_new)
    ob_s, st_s = _hgrn_sample(proj_s, lb, g_norm[0], state_hgrn[0], t_new)
    mixed_s = _mix(oa_s, ob_s, proj_s, wpa, wpb)
    h_s, hn_s, lg_s = _resid(xs, mixed_s, wo, ffn_norm[0], wr_hi, wr_lo, br)

    logits = jnp.concatenate([lg_p, lg_s], axis=0)[:, :N_EXPERTS]
    assert n_p % MOE_TM == 0 and n_s % MOE_TM == 0, "each pass must fill whole MoE token tiles"
    blk_e, blk_nsub, n_used, layout, slot, gate, n_blk = _route(logits)
    hn_all = jnp.concatenate([hn_p, hn_s], axis=0)
    xs_sorted, gate_rows = _dispatch(layout, slot, gate, hn_all, n_blk * MOE_RB)
    yb = _moe_experts(blk_e, blk_nsub, n_used, xs_sorted, gate_rows, w_up[0], b_up[0], w_down[0], b_down[0])
    y_p = _combine(layout, slot, h_p, yb, 0)
    y_s = _combine(layout, slot, h_s, yb, n_p // MOE_TM)

    def prompt_rows(kv, win):
        keep = min(win, seq)
        rows = kv.reshape(bsz, seq, 2 * A_OUT)[:, seq - keep:]
        return rows.reshape(1, bsz, keep, 2, A_HPG, HEAD)

    kvp = [prompt_rows(kv_p[g], A_GROUPS[g][0]) for g in range(3)]
    kvs = [kv_s[g].reshape(1, nseq, t_new, 2, A_HPG, HEAD) for g in range(3)]
    return (y_p.reshape(bsz, seq, D_MODEL), y_s.reshape(nseq, t_new, D_MODEL),
            kvp[0], kvp[1], kvp[2], st_p[None],
            kvs[0], kvs[1], kvs[2], st_s[None])
```

```python
import functools

import numpy as np
import jax
import jax.numpy as jnp
from jax import lax
from jax.experimental import pallas as pl
from jax.experimental.pallas import tpu as pltpu

F32 = jnp.float32
BF16 = jnp.bfloat16

D_MODEL = 2048
HEAD = 128
A_GROUPS = ((128, 1), (512, 4), (2048, 16))
A_HPG = 4
A_KEYS = 128
A_WIDTH = 3 * A_HPG * HEAD
A_OUT = A_HPG * HEAD
B_HEADS = 8
B_WIDTH = B_HEADS * HEAD
B_CHUNK = 32
REL_BUCKETS = 32
REL_MAX_DIST = 2048
N_EXPERTS = 32
TOP_K = 4
D_FF = 2048
SWIGLU_ALPHA = 1.702
SWIGLU_LIMIT = 7.0
NORM_EPS = 1e-6
NEG_INF = -1e30
ATTN_SCALE = HEAD ** -0.5
IN_WIDTH = 3 * A_WIDTH + 4 * B_WIDTH + 2 * D_MODEL
OFF_Q, OFF_K, OFF_V = 0, A_WIDTH, 2 * A_WIDTH
OFF_QB = 3 * A_WIDTH
OFF_FB, OFF_IB, OFF_OB = OFF_QB + B_WIDTH, OFF_QB + 2 * B_WIDTH, OFF_QB + 3 * B_WIDTH
OFF_GA = OFF_QB + 4 * B_WIDTH
OFF_GB = OFF_GA + D_MODEL

V7X_VMEM_LIMIT = 56 * 1024 * 1024
SUBLANES = 8

NT_DIMS = (((1,), (1,)), ((), ()))


def _sigmoid(x):
    return 1.0 / (1.0 + jnp.exp(-x))


def _cparams(sem):
    return pltpu.CompilerParams(dimension_semantics=sem, vmem_limit_bytes=V7X_VMEM_LIMIT)


NORM_TM = 512
PROJ_TM = 1024
PROJ_TN = 1280


def _rmsnorm_kernel(x_ref, nw_ref, o_ref):
    x = x_ref[...]
    ms = jnp.mean(x * x, axis=-1, keepdims=True)
    o_ref[...] = (x * lax.rsqrt(ms + NORM_EPS) * nw_ref[...]).astype(BF16)


def _inproj_kernel(xn_ref, w_ref, o_ref, wbf_ref):
    @pl.when(pl.program_id(1) == 0)
    def _():
        wbf_ref[...] = w_ref[...].astype(BF16)

    o_ref[...] = jnp.dot(xn_ref[...], wbf_ref[...], preferred_element_type=F32)


def _inproj(x, norm_w, w_in):
    m = x.shape[0]
    xn = pl.pallas_call(
        _rmsnorm_kernel,
        out_shape=jax.ShapeDtypeStruct((m, D_MODEL), BF16),
        grid=(m // NORM_TM,),
        in_specs=[pl.BlockSpec((NORM_TM, D_MODEL), lambda i: (i, 0)), pl.BlockSpec((1, D_MODEL), lambda i: (0, 0))],
        out_specs=pl.BlockSpec((NORM_TM, D_MODEL), lambda i: (i, 0)),
        compiler_params=_cparams(("arbitrary",)),
        name="attn_rmsnorm",
    )(x, norm_w.reshape(1, D_MODEL))
    tm = min(PROJ_TM, m)
    return pl.pallas_call(
        _inproj_kernel,
        out_shape=jax.ShapeDtypeStruct((m, IN_WIDTH), F32),
        grid=(IN_WIDTH // PROJ_TN, m // tm),
        in_specs=[
            pl.BlockSpec((tm, D_MODEL), lambda j, i: (i, 0)),
            pl.BlockSpec((D_MODEL, PROJ_TN), lambda j, i: (0, j)),
        ],
        out_specs=pl.BlockSpec((tm, PROJ_TN), lambda j, i: (i, j)),
        scratch_shapes=[pltpu.VMEM((D_MODEL, PROJ_TN), BF16)],
        compiler_params=_cparams(("arbitrary", "arbitrary")),
        name="inproj",
    )(xn, w_in)


QKV_TM = 512


def _qkv_kernel(q_ref, k_ref, v_ref, qw_ref, kw_ref, qn_ref, kv0_ref, kv1_ref, kv2_ref, kvt_ref):
    kv_refs = (kv0_ref, kv1_ref, kv2_ref)
    qw = qw_ref[...]
    kw = kw_ref[...]
    for h in range(3 * A_HPG):
        sl = slice(h * HEAD, (h + 1) * HEAD)
        q = q_ref[:, sl]
        qn_ref[:, sl] = q * lax.rsqrt(jnp.mean(q * q, axis=-1, keepdims=True) + NORM_EPS) * qw
        k = k_ref[:, sl]
        kn = k * lax.rsqrt(jnp.mean(k * k, axis=-1, keepdims=True) + NORM_EPS) * kw
        g, hh = divmod(h, A_HPG)
        kv_refs[g][:, hh * HEAD:(hh + 1) * HEAD] = kn
        kv_refs[g][:, A_OUT + hh * HEAD:A_OUT + (hh + 1) * HEAD] = v_ref[:, sl]
        if g == 2:
            kvt_ref[:, hh, :] = kn
            kvt_ref[:, A_HPG + hh, :] = v_ref[:, sl]


def _qkv(proj, q_norm, k_norm):
    m = proj.shape[0]
    kv_shape = jax.ShapeDtypeStruct((m, 2 * A_OUT), F32)
    return pl.pallas_call(
        _qkv_kernel,
        out_shape=(jax.ShapeDtypeStruct((m, A_WIDTH), F32), kv_shape, kv_shape, kv_shape,
                   jax.ShapeDtypeStruct((m, 2 * A_HPG, HEAD), F32)),
        grid=(m // QKV_TM,),
        in_specs=[
            pl.BlockSpec((QKV_TM, A_WIDTH), lambda i: (i, OFF_Q // A_WIDTH)),
            pl.BlockSpec((QKV_TM, A_WIDTH), lambda i: (i, OFF_K // A_WIDTH)),
            pl.BlockSpec((QKV_TM, A_WIDTH), lambda i: (i, OFF_V // A_WIDTH)),
            pl.BlockSpec((1, HEAD), lambda i: (0, 0)),
            pl.BlockSpec((1, HEAD), lambda i: (0, 0)),
        ],
        out_specs=(
            pl.BlockSpec((QKV_TM, A_WIDTH), lambda i: (i, 0)),
            pl.BlockSpec((QKV_TM, 2 * A_OUT), lambda i: (i, 0)),
            pl.BlockSpec((QKV_TM, 2 * A_OUT), lambda i: (i, 0)),
            pl.BlockSpec((QKV_TM, 2 * A_OUT), lambda i: (i, 0)),
            pl.BlockSpec((QKV_TM, 2 * A_HPG, HEAD), lambda i: (i, 0, 0)),
        ),
        compiler_params=_cparams(("arbitrary",)),
        name="qkv_norm",
    )(proj, proj, proj, q_norm.reshape(1, HEAD), k_norm.reshape(1, HEAD))


def _t5_bucket(dist):
    dist = np.asarray(dist, np.int64)
    max_exact = REL_BUCKETS // 2
    ratio = np.log(np.maximum(dist, 1) / max_exact) / np.log(REL_MAX_DIST / max_exact)
    large = np.minimum(max_exact + (ratio * (REL_BUCKETS - max_exact)).astype(np.int64), REL_BUCKETS - 1)
    return np.where(dist < max_exact, dist, large).astype(np.int32)


def _bias_lookup(tab, bucket, valid):
    bucket = jnp.asarray(np.where(valid, bucket, -1).astype(np.int32))[None]
    out = jnp.full((tab.shape[1],) + bucket.shape[1:], NEG_INF, F32)
    for b in range(REL_BUCKETS):
        out = jnp.where(bucket == b, tab[b].reshape((-1,) + (1,) * (bucket.ndim - 1)), out)
    return out


def _prompt_bias(rel_bias, g):
    _, dil = A_GROUPS[g]
    n = A_KEYS
    j = n + np.arange(n)[:, None] - np.arange(2 * n)[None, :]
    valid = (j >= 0) & (j < n)
    tab = rel_bias[:, g * A_HPG:(g + 1) * A_HPG].astype(F32)
    return _bias_lookup(tab, _t5_bucket(np.clip(j, 0, n - 1) * dil), valid)


ATT_MIX_ROWS = 256


def _attn_prompt_kernel(q0_ref, q1_ref, q2_ref, k0_ref, v0_ref, k1_ref, v1_ref, k2_ref, v2_ref, bias_ref,
                        o_ref, og_scr, lg_scr, *, seq):
    n = A_KEYS
    h = pl.program_id(1)
    q_refs = (q0_ref, q1_ref, q2_ref)
    k_refs = (k0_ref, k1_ref, k2_ref)
    v_refs = (v0_ref, v1_ref, v2_ref)
    for g, (win, dil) in enumerate(A_GROUPS):
        bias = bias_ref[g, h]
        for r in range(dil):
            for i in range(seq // win):
                def rows(blk, r=r, dil=dil):
                    if dil == 1:
                        return pl.ds(blk * n, n)
                    return pl.ds(r + blk * n * dil, n, stride=dil)

                q = q_refs[g][rows(i), :].astype(BF16)
                if i == 0:
                    kk = k_refs[g][rows(i), :].astype(BF16)
                    vv = v_refs[g][rows(i), :].astype(BF16)
                    b_tile = bias[:, n:]
                else:
                    kk = jnp.concatenate([k_refs[g][rows(i - 1), :], k_refs[g][rows(i), :]], axis=0).astype(BF16)
                    vv = jnp.concatenate([v_refs[g][rows(i - 1), :], v_refs[g][rows(i), :]], axis=0).astype(BF16)
                    b_tile = bias
                s = lax.dot_general(q, kk, NT_DIMS, preferred_element_type=F32) * ATTN_SCALE + b_tile
                mx = jnp.max(s, axis=-1, keepdims=True)
                p = jnp.exp(s - mx)
                den = jnp.sum(p, axis=-1, keepdims=True)
                og_scr[g, rows(i), :] = jnp.dot(p.astype(BF16), vv, preferred_element_type=F32) / den
                lg_scr[g, rows(i), :] = jnp.broadcast_to(mx + jnp.log(den), (n, HEAD))
    for c in range(seq // ATT_MIX_ROWS):
        sl = slice(c * ATT_MIX_ROWS, (c + 1) * ATT_MIX_ROWS)
        la, lb_, lc = lg_scr[0, sl, :], lg_scr[1, sl, :], lg_scr[2, sl, :]
        top = jnp.maximum(jnp.maximum(la, lb_), lc)
        ea, eb, ec = jnp.exp(la - top), jnp.exp(lb_ - top), jnp.exp(lc - top)
        o_ref[sl, :] = (ea * og_scr[0, sl, :] + eb * og_scr[1, sl, :] + ec * og_scr[2, sl, :]) / (ea + eb + ec)


def _attn_prompt(qn, kvs, bias, bsz, seq):
    m = qn.shape[0]
    col = lambda fn: pl.BlockSpec((seq, HEAD), lambda b, h: (b, fn(h)))
    in_specs = [col(lambda h, g=g: g * A_HPG + h) for g in range(3)]
    for _ in range(3):
        in_specs += [col(lambda h: h), col(lambda h: A_HPG + h)]
    in_specs.append(pl.BlockSpec(bias.shape, lambda b, h: (0, 0, 0, 0)))
    return pl.pallas_call(
        functools.partial(_attn_prompt_kernel, seq=seq),
        out_shape=jax.ShapeDtypeStruct((m, A_OUT), F32),
        grid=(bsz, A_HPG),
        in_specs=in_specs,
        out_specs=col(lambda h: h),
        scratch_shapes=[pltpu.VMEM((3, seq, HEAD), F32), pltpu.VMEM((3, seq, HEAD), F32)],
        compiler_params=_cparams(("arbitrary", "arbitrary")),
        name="attn_prompt",
    )(qn, qn, qn, kvs[0], kvs[0], kvs[1], kvs[1], kvs[2], kvs[2], bias)


def _hgrn_gates(q, z, lb):
    qs = q * _sigmoid(q)
    logf = jnp.log(lb + (1.0 - lb) * _sigmoid(z))
    kb = (1.0 - lb) * _sigmoid(-z)
    return qs, kb, logf


def _hgrn_out(o, gate, gn):
    on = o * lax.rsqrt(jnp.mean(o * o, axis=-1, keepdims=True) + NORM_EPS) * gn
    return (on * (gate * _sigmoid(gate))).astype(BF16)


def _split3(x):
    hi = x.astype(BF16)
    r1 = x - hi.astype(F32)
    mid = r1.astype(BF16)
    lo = (r1 - mid.astype(F32)).astype(BF16)
    return hi, mid, lo


HG_TC = 256
HG_NC = HG_TC // B_CHUNK
HG_HPS = 4


def _hgrn_prompt_kernel(q_ref, f_ref, i_ref, g_ref, lb_ref, gn_ref, ob_ref, st_ref, s_scr):
    tb = pl.program_id(2)

    @pl.when(tb == 0)
    def _():
        s_scr[...] = jnp.zeros_like(s_scr)

    row = lax.broadcasted_iota(jnp.int32, (HG_TC, HG_TC), 0)
    col = lax.broadcasted_iota(jnp.int32, (HG_TC, HG_TC), 1)
    same_chunk = (row // B_CHUNK) == (col // B_CHUNK)
    causal = jnp.logical_and(same_chunk, col <= row)
    cum_mat = jnp.where(causal, 1.0, 0.0).astype(BF16)
    tcol = lax.broadcasted_iota(jnp.int32, (HEAD, HG_TC), 1) // B_CHUNK
    trow = lax.broadcasted_iota(jnp.int32, (HG_TC, HEAD), 0) // B_CHUNK
    gn = gn_ref[...]
    for h in range(HG_HPS):
        sl = slice(h * HEAD, (h + 1) * HEAD)
        qs, kb, logf = _hgrn_gates(q_ref[:, sl], f_ref[:, sl], lb_ref[:, sl])
        b3 = jnp.dot(cum_mat, jnp.concatenate(_split3(logf), axis=1), preferred_element_type=F32)
        b = b3[:, :HEAD] + b3[:, HEAD:2 * HEAD] + b3[:, 2 * HEAD:]
        bl_rows = [b[c * B_CHUNK + B_CHUNK - 1:(c + 1) * B_CHUNK, :] for c in range(HG_NC)]
        bl = jnp.concatenate([jnp.broadcast_to(r, (B_CHUNK, HEAD)) for r in bl_rows], axis=0)
        q_dec_f = qs * jnp.exp(b)
        q_dec = q_dec_f.astype(BF16)
        k_inv = (kb * jnp.exp(-b)).astype(BF16)
        k_end_t = jnp.transpose(kb * jnp.exp(bl - b))
        v = i_ref[:, sl].astype(BF16)
        att = lax.dot_general(q_dec, k_inv, NT_DIMS, preferred_element_type=F32)
        att = jnp.where(causal, att, 0.0).astype(BF16)
        o_intra = jnp.dot(att, v, preferred_element_type=F32)
        decay_t = jnp.exp(jnp.transpose(bl))
        ke_stack = jnp.concatenate([jnp.where(tcol == c, k_end_t, 0.0) for c in range(HG_NC)], axis=0)
        ds_all = jnp.dot(ke_stack.astype(BF16), v, preferred_element_type=F32)
        s = s_scr[h]
        starts = []
        for c in range(HG_NC):
            starts.append(s)
            s = decay_t[:, c * B_CHUNK:c * B_CHUNK + 1] * s + ds_all[c * HEAD:(c + 1) * HEAD, :]
        s_scr[h] = s
        q_bd = jnp.concatenate([jnp.where(trow == c, q_dec_f, 0.0) for c in range(HG_NC)], axis=1).astype(BF16)
        o_inter = jnp.dot(q_bd, jnp.concatenate(starts, axis=0).astype(BF16), preferred_element_type=F32)
        ob_ref[:, sl] = _hgrn_out(o_intra + o_inter, g_ref[:, sl], gn)

    @pl.when(tb == pl.num_programs(2) - 1)
    def _():
        st_ref[...] = s_scr[...]


def _hgrn_prompt(proj, lb, g_norm, bsz, seq):
    m = proj.shape[0]
    hw = HG_HPS * HEAD
    nt = seq // HG_TC
    rowblk = lambda b, hf, t: b * nt + t
    spec = lambda off: pl.BlockSpec((HG_TC, hw), lambda b, hf, t: (rowblk(b, hf, t), off // hw + hf))
    return pl.pallas_call(
        _hgrn_prompt_kernel,
        out_shape=(jax.ShapeDtypeStruct((m, B_WIDTH), BF16),
                   jax.ShapeDtypeStruct((bsz, B_HEADS, HEAD, HEAD), F32)),
        grid=(bsz, B_HEADS // HG_HPS, nt),
        in_specs=[
            spec(OFF_QB), spec(OFF_FB), spec(OFF_IB), spec(OFF_OB),
            pl.BlockSpec((1, hw), lambda b, hf, t: (0, hf)),
            pl.BlockSpec((1, HEAD), lambda b, hf, t: (0, 0)),
        ],
        out_specs=(
            pl.BlockSpec((HG_TC, hw), lambda b, hf, t: (rowblk(b, hf, t), hf)),
            pl.BlockSpec((None, HG_HPS, HEAD, HEAD), lambda b, hf, t: (b, hf, 0, 0)),
        ),
        scratch_shapes=[pltpu.VMEM((HG_HPS, HEAD, HEAD), F32)],
        compiler_params=_cparams(("arbitrary", "arbitrary", "arbitrary")),
        name="hgrn_prompt",
    )(proj, proj, proj, proj, lb.reshape(1, B_WIDTH), g_norm.reshape(1, HEAD))


HS_SEQ = 32


def _hgrn_sample_kernel(q_ref, f_ref, i_ref, g_ref, lb_ref, gn_ref, s0_ref, ob_ref, st_ref, *, t_new):
    rows = HS_SEQ * t_new
    qs, kb, logf = _hgrn_gates(q_ref[...], f_ref[...], lb_ref[...])
    t_of = lax.broadcasted_iota(jnp.int32, (rows, HEAD), 0) % t_new
    b = logf
    sh = 1
    while sh < t_new:
        b = b + jnp.where(t_of >= sh, pltpu.roll(b, sh, axis=0), 0.0)
        sh *= 2
    bl = b
    for back in range(1, t_new):
        bl = jnp.where(t_of == t_new - 1 - back, pltpu.roll(b, rows - back, axis=0), bl)
    q_dec = (qs * jnp.exp(b)).astype(BF16)
    k_inv = (kb * jnp.exp(-b)).astype(BF16)
    k_end_t = jnp.transpose(kb * jnp.exp(bl - b))
    decay_t = jnp.exp(jnp.transpose(bl))
    v = i_ref[...].astype(BF16)
    row = lax.broadcasted_iota(jnp.int32, (rows, rows), 0)
    col = lax.broadcasted_iota(jnp.int32, (rows, rows), 1)
    causal = jnp.logical_and(row // t_new == col // t_new, col <= row)
    att = lax.dot_general(q_dec, k_inv, NT_DIMS, preferred_element_type=F32)
    att = jnp.where(causal, att, 0.0).astype(BF16)
    o_intra = jnp.dot(att, v, preferred_element_type=F32)
    seq_of_col = lax.broadcasted_iota(jnp.int32, (HEAD, rows), 1) // t_new
    seq_of_row = lax.broadcasted_iota(jnp.int32, (rows, HEAD), 0) // t_new
    o_inter = jnp.zeros((rows, HEAD), F32)
    for s_i in range(HS_SEQ):
        s = s0_ref[s_i, 0]
        part = jnp.dot(q_dec, s.astype(BF16), preferred_element_type=F32)
        o_inter = jnp.where(seq_of_row == s_i, part, o_inter)
        ke = jnp.where(seq_of_col == s_i, k_end_t, 0.0).astype(BF16)
        ds = jnp.dot(ke, v, preferred_element_type=F32)
        st_ref[s_i, 0] = decay_t[:, s_i * t_new:s_i * t_new + 1] * s + ds
    ob_ref[...] = _hgrn_out(o_intra + o_inter, g_ref[...], gn_ref[...])


def _hgrn_sample(proj, lb, g_norm, s0, t_new):
    m = proj.shape[0]
    nseq = m // t_new
    rows = HS_SEQ * t_new
    spec = lambda off: pl.BlockSpec((rows, HEAD), lambda i, h: (i, off // HEAD + h))
    st_spec = pl.BlockSpec((HS_SEQ, 1, HEAD, HEAD), lambda i, h: (i, h, 0, 0))
    return pl.pallas_call(
        functools.partial(_hgrn_sample_kernel, t_new=t_new),
        out_shape=(jax.ShapeDtypeStruct((m, B_WIDTH), BF16),
                   jax.ShapeDtypeStruct((nseq, B_HEADS, HEAD, HEAD), F32)),
        grid=(nseq // HS_SEQ, B_HEADS),
        in_specs=[
            spec(OFF_QB), spec(OFF_FB), spec(OFF_IB), spec(OFF_OB),
            pl.BlockSpec((1, HEAD), lambda i, h: (0, h)),
            pl.BlockSpec((1, HEAD), lambda i, h: (0, 0)),
            st_spec,
        ],
        out_specs=(pl.BlockSpec((rows, HEAD), lambda i, h: (i, h)), st_spec),
        compiler_params=_cparams(("arbitrary", "arbitrary")),
        name="hgrn_sample",
    )(proj, proj, proj, proj, lb.reshape(1, B_WIDTH), g_norm.reshape(1, HEAD), s0)


SA_SEQ = 2
KV_ROWS = 2 * A_HPG


def _sample_bias(rel_bias, t_new):
    def tile(tab, dist, valid):
        bias = _bias_lookup(tab, _t5_bucket(np.clip(dist, 0, REL_MAX_DIST)), valid)
        bias = jnp.moveaxis(bias, 0, -1)
        bias = jnp.concatenate([bias, jnp.zeros(bias.shape[:-1] + (KV_ROWS - A_HPG,), F32)], axis=-1)
        return jnp.broadcast_to(bias[..., None], bias.shape + (HEAD,))

    t = np.arange(t_new)[:, None]
    r = np.arange(A_KEYS + t_new)[None, :]
    is_new = r >= A_KEYS
    dist = np.where(is_new, t - (r - A_KEYS), A_KEYS + t - r)
    valid = (dist >= 0) & (dist < A_KEYS)
    b0 = tile(rel_bias[:, 0:A_HPG].astype(F32), dist, valid)
    b12 = []
    for g in (1, 2):
        _, dil = A_GROUPS[g]
        m = np.arange(A_KEYS + 1)
        dist = dil * (A_KEYS - m)
        b12.append(tile(rel_bias[:, g * A_HPG:(g + 1) * A_HPG].astype(F32), dist, m >= 1))
    return b0, jnp.stack(b12)


def _attn_sample_kernel(q_ref, n0_ref, n1_ref, n2_ref, c0_ref, c1_ref, c2_ref, b0_ref, b12_ref,
                        o_ref, qt_scr, ns_scr, *, t_new):
    nq = SA_SEQ * t_new
    new_refs = (n0_ref, n1_ref, n2_ref)
    cache_refs = (c0_ref, c1_ref, c2_ref)
    qt_scr[...] = jnp.zeros_like(qt_scr)
    for g in range(3):
        for j in range(nq):
            for h in range(A_HPG):
                qt_scr[g, j, h:h + 1, :] = (
                    q_ref[j:j + 1, g * A_OUT + h * HEAD:g * A_OUT + (h + 1) * HEAD] * ATTN_SCALE)
                ns_scr[g, j, h:h + 1, :] = new_refs[g][j:j + 1, h * HEAD:(h + 1) * HEAD]
                ns_scr[g, j, A_HPG + h:A_HPG + h + 1, :] = (
                    new_refs[g][j:j + 1, A_OUT + h * HEAD:A_OUT + (h + 1) * HEAD])

    row_id = lax.broadcasted_iota(jnp.int32, (nq, HEAD), 0)
    ones = jnp.ones((HEAD, HEAD), BF16)

    def per_query(j, out):
        s = j // t_new
        t = j - s * t_new
        row0 = s * A_KEYS
        stats = []
        for g in range(3):
            c_ref = cache_refs[g]
            qt = qt_scr[g, j]
            if g == 0:
                sub0 = 0
                bias_c = b0_ref[t, 0:A_KEYS]
                new_tiles = [(b0_ref[t, A_KEYS + tp], ns_scr[0, s * t_new + tp]) for tp in range(t_new)]
            else:
                sub0 = pl.multiple_of(t * KV_ROWS, KV_ROWS)
                bias_c = b12_ref[g - 1, 0:A_KEYS]
                new_tiles = [(b12_ref[g - 1, A_KEYS], ns_scr[g, j])]

            def mirror(l):
                return l + pltpu.roll(l, A_HPG, axis=l.ndim - 2)

            tiles = c_ref[pl.ds(row0, A_KEYS), pl.ds(sub0, KV_ROWS), :]
            prod = (tiles * qt[None]).reshape(A_KEYS * KV_ROWS, HEAD).astype(BF16)
            sums = jnp.dot(prod, ones, preferred_element_type=F32).reshape(A_KEYS, KV_ROWS, HEAD)
            l_c = mirror(sums + bias_c)
            new_l = [mirror(jnp.sum(tile_kv * qt, axis=-1, keepdims=True) + bias) for bias, tile_kv in new_tiles]
            mx = jnp.max(l_c, axis=0)
            for l in new_l:
                mx = jnp.maximum(mx, l)
            p_c = jnp.exp(l_c - mx[None])
            den = jnp.sum(p_c, axis=0)
            acc = jnp.sum(p_c * tiles, axis=0)
            for l, (_, tile_kv) in zip(new_l, new_tiles):
                p = jnp.exp(l - mx)
                den = den + p
                acc = acc + p * tile_kv
            stats.append((mx + jnp.log(den), den, acc))
        top = jnp.maximum(jnp.maximum(stats[0][0], stats[1][0]), stats[2][0])
        e = [jnp.exp(st[0] - top) for st in stats]
        tot = e[0] + e[1] + e[2]
        mix = jnp.zeros((KV_ROWS, HEAD), F32)
        for g in range(3):
            mix = mix + e[g] / (tot * stats[g][1]) * stats[g][2]
        pieces = [jnp.broadcast_to(mix[A_HPG + h:A_HPG + h + 1, :], (nq, HEAD)) for h in range(A_HPG)]
        return tuple(jnp.where(row_id == j, pieces[h], out[h]) for h in range(A_HPG))

    init = tuple(jnp.zeros((nq, HEAD), F32) for _ in range(A_HPG))
    out = lax.fori_loop(0, nq, per_query, init)
    for h in range(A_HPG):
        o_ref[:, h * HEAD:(h + 1) * HEAD] = out[h]


def _attn_sample(qn, kv_new, caches, rel_bias, t_new):
    m = qn.shape[0]
    nseq = m // t_new
    nq = SA_SEQ * t_new
    assert nq % SUBLANES == 0, "the new-token rows of one grid step must fill whole sublane tiles"
    kvw = 2 * A_OUT
    views, specs = [], []
    for g, (win, dil) in enumerate(A_GROUPS):
        c = caches[g]
        assert c.shape[1] == win, "cache must hold exactly one window of positions"
        assert g == 0 or dil >= t_new, "each new token must sit in its own residue class"
        views.append(c.reshape(nseq * A_KEYS, dil * KV_ROWS, HEAD))
        specs.append(pl.BlockSpec((SA_SEQ * A_KEYS, min(dil, t_new) * KV_ROWS, HEAD), lambda i: (i, 0, 0)))
    b0, b12 = _sample_bias(rel_bias, t_new)
    new_spec = pl.BlockSpec((nq, kvw), lambda i: (i, 0))
    return pl.pallas_call(
        functools.partial(_attn_sample_kernel, t_new=t_new),
        out_shape=jax.ShapeDtypeStruct((m, A_OUT), F32),
        grid=(nseq // SA_SEQ,),
        in_specs=[
            pl.BlockSpec((nq, A_WIDTH), lambda i: (i, 0)),
            new_spec, new_spec, new_spec,
            specs[0], specs[1], specs[2],
            pl.BlockSpec(b0.shape, lambda i: (0, 0, 0, 0)),
            pl.BlockSpec(b12.shape, lambda i: (0, 0, 0, 0)),
        ],
        out_specs=pl.BlockSpec((nq, A_OUT), lambda i: (i, 0)),
        scratch_shapes=[
            pltpu.VMEM((3, nq, KV_ROWS, HEAD), F32),
            pltpu.VMEM((3, nq, KV_ROWS, HEAD), F32),
        ],
        compiler_params=_cparams(("arbitrary",)),
        name="attn_sample",
    )(qn, kv_new[0], kv_new[1], kv_new[2], views[0], views[1], views[2], b0, b12)


MIX_TM = 512
MIX_TN = 512


def _mix_kernel(oa_ref, ob_ref, ga_ref, gb_ref, wpa_ref, wpb_ref, out_ref):
    pa = jnp.dot(oa_ref[...].astype(BF16), wpa_ref[...], preferred_element_type=F32)
    pb = jnp.dot(ob_ref[...], wpb_ref[...], preferred_element_type=F32)
    out_ref[...] = (_sigmoid(ga_ref[...]) * pa + _sigmoid(gb_ref[...]) * pb).astype(BF16)


def _mix(oa, ob, proj, w_pa, w_pb):
    m = proj.shape[0]
    return pl.pallas_call(
        _mix_kernel,
        out_shape=jax.ShapeDtypeStruct((m, D_MODEL), BF16),
        grid=(m // MIX_TM, D_MODEL // MIX_TN),
        in_specs=[
            pl.BlockSpec((MIX_TM, A_OUT), lambda i, j: (i, 0)),
            pl.BlockSpec((MIX_TM, B_WIDTH), lambda i, j: (i, 0)),
            pl.BlockSpec((MIX_TM, MIX_TN), lambda i, j: (i, OFF_GA // MIX_TN + j)),
            pl.BlockSpec((MIX_TM, MIX_TN), lambda i, j: (i, OFF_GB // MIX_TN + j)),
            pl.BlockSpec((A_OUT, MIX_TN), lambda i, j: (0, j)),
            pl.BlockSpec((B_WIDTH, MIX_TN), lambda i, j: (0, j)),
        ],
        out_specs=pl.BlockSpec((MIX_TM, MIX_TN), lambda i, j: (i, j)),
        compiler_params=_cparams(("arbitrary", "arbitrary")),
        name="mix",
    )(oa, ob, proj, proj, w_pa, w_pb)


RT_LANES = 128


def _resid_kernel(x_ref, mixed_ref, wo_ref, nw_ref, wr_hi_ref, wr_lo_ref, br_ref, h_ref, hn_ref, lg_ref):
    h = x_ref[...] + jnp.dot(mixed_ref[...], wo_ref[...], preferred_element_type=F32)
    h_ref[...] = h
    hn = h * lax.rsqrt(jnp.mean(h * h, axis=-1, keepdims=True) + NORM_EPS) * nw_ref[...]
    hi = hn.astype(BF16)
    hn_ref[...] = hi
    lo = (hn - hi.astype(F32)).astype(BF16)
    lg = (jnp.dot(hi, wr_hi_ref[...], preferred_element_type=F32)
          + jnp.dot(lo, wr_hi_ref[...], preferred_element_type=F32)
          + jnp.dot(hi, wr_lo_ref[...], preferred_element_type=F32))
    lg_ref[...] = lg + br_ref[...]


def _resid(x, mixed, w_o, ffn_norm, wr_hi, wr_lo, br):
    m = x.shape[0]
    full = lambda shape: pl.BlockSpec(shape, lambda i: (0, 0))
    return pl.pallas_call(
        _resid_kernel,
        out_shape=(jax.ShapeDtypeStruct((m, D_MODEL), F32),
                   jax.ShapeDtypeStruct((m, D_MODEL), BF16),
                   jax.ShapeDtypeStruct((m, RT_LANES), F32)),
        grid=(m // MIX_TM,),
        in_specs=[
            pl.BlockSpec((MIX_TM, D_MODEL), lambda i: (i, 0)),
            pl.BlockSpec((MIX_TM, D_MODEL), lambda i: (i, 0)),
            full((D_MODEL, D_MODEL)), full((1, D_MODEL)),
            full((D_MODEL, RT_LANES)), full((D_MODEL, RT_LANES)), full((1, RT_LANES)),
        ],
        out_specs=(
            pl.BlockSpec((MIX_TM, D_MODEL), lambda i: (i, 0)),
            pl.BlockSpec((MIX_TM, D_MODEL), lambda i: (i, 0)),
            pl.BlockSpec((MIX_TM, RT_LANES), lambda i: (i, 0)),
        ),
        compiler_params=_cparams(("arbitrary",)),
        name="resid_router",
    )(x, mixed, w_o, ffn_norm.reshape(1, D_MODEL), wr_hi, wr_lo, br)


MOE_RB = 1280
MOE_SUB = 128
MOE_TF = 256
MOE_NF = D_FF // MOE_TF


MOE_TM = 512
MOE_GRAN = 16
MOE_SLOTS = 2560
MOE_CHUNK = 256
assert MOE_SLOTS >= MOE_TM * TOP_K + N_EXPERTS * (MOE_GRAN - 1) and MOE_SLOTS % MOE_CHUNK == 0


def _route(logits):
    n_tok = logits.shape[0]
    nt = n_tok // MOE_TM
    top_val, top_idx = lax.top_k(logits, TOP_K)
    gate = jax.nn.softmax(top_val, axis=-1)
    na = MOE_TM * TOP_K
    e3 = top_idx.reshape(nt, na)
    hot = e3[..., None] == jnp.arange(N_EXPERTS)
    onehot = hot.astype(jnp.int32)
    lower = (lax.broadcasted_iota(jnp.int32, (na, na), 0) >= lax.broadcasted_iota(jnp.int32, (na, na), 1))
    incl = jnp.einsum('ij,tjk->tik', lower.astype(BF16), hot.astype(BF16),
                      preferred_element_type=F32).astype(jnp.int32)
    rank = jnp.sum((incl - onehot) * onehot, axis=2)
    seg = (incl[:, -1] + MOE_GRAN - 1) // MOE_GRAN * MOE_GRAN
    loc_off = jnp.cumsum(seg, axis=1) - seg
    rows_e = jnp.sum(seg, axis=0)
    padded_sub = (rows_e + MOE_SUB - 1) // MOE_SUB * MOE_SUB
    padded_rb = (rows_e + MOE_RB - 1) // MOE_RB * MOE_RB
    blk_end = jnp.cumsum(padded_rb) // MOE_RB
    row_start = jnp.cumsum(padded_rb) - padded_rb
    seg_start = row_start[None, :] + jnp.cumsum(seg, axis=0) - seg
    slot = jnp.sum(onehot * loc_off[:, None, :], axis=2) + rank
    max_rows = n_tok * TOP_K + nt * N_EXPERTS * (MOE_GRAN - 1)
    n_blk = max_rows // MOE_RB + N_EXPERTS
    blk_ids = jnp.arange(n_blk)
    blk_e = jnp.minimum(jnp.searchsorted(blk_end, blk_ids, side='right'), N_EXPERTS - 1).astype(jnp.int32)
    rows_left = padded_sub[blk_e] - (blk_ids * MOE_RB - row_start[blk_e])
    blk_nsub = jnp.clip(rows_left // MOE_SUB, 0, MOE_RB // MOE_SUB).astype(jnp.int32)
    n_used = blk_end[-1:].astype(jnp.int32)
    tail_row = jnp.where(rows_e > 0, row_start + padded_sub - MOE_SUB, -1).astype(jnp.int32)
    layout = dict(seg_start=seg_start.reshape(-1).astype(jnp.int32),
                  seg_gran=(seg // MOE_GRAN).reshape(-1).astype(jnp.int32),
                  loc_off=loc_off.reshape(-1).astype(jnp.int32),
                  tail_row=tail_row)
    slot = slot.reshape(nt, MOE_TM, TOP_K).astype(jnp.int32)
    return blk_e, blk_nsub, n_used, layout, slot, gate.reshape(nt, MOE_TM, TOP_K), n_blk


def _segment_copies(layout_refs, tile, src_of, dst_of, sem, start):
    seg_start_ref, seg_gran_ref, loc_off_ref = layout_refs

    for e in range(N_EXPERTS):
        idx = tile * N_EXPERTS + e
        loc = loc_off_ref[idx]
        glob = seg_start_ref[idx]

        def body(c, carry, loc=loc, glob=glob):
            copy = pltpu.make_async_copy(
                src_of(pl.multiple_of(loc + c * MOE_GRAN, MOE_GRAN), pl.multiple_of(glob + c * MOE_GRAN, MOE_GRAN)),
                dst_of(pl.multiple_of(loc + c * MOE_GRAN, MOE_GRAN), pl.multiple_of(glob + c * MOE_GRAN, MOE_GRAN)),
                sem)
            if start:
                copy.start()
            else:
                copy.wait()
            return carry

        lax.fori_loop(0, seg_gran_ref[idx], body, 0)


def _slot_onehot(slot_vecs, base, shape, axis):
    ids = base + lax.broadcasted_iota(jnp.int32, shape, axis)
    hit = jnp.zeros(shape, F32)
    for v in slot_vecs:
        hit = jnp.where(ids == v, 1.0, hit)
    return hit.astype(BF16)


def _dispatch_kernel(seg_start_ref, seg_gran_ref, loc_off_ref, tail_ref, slot_ref, gate_ref, xa_ref, xb_ref,
                     xs_ref, gs_ref, sorted_scr, gsorted_scr, zero_scr, gzero_scr, sem, buf_sem, *, tiles_a):
    i = pl.program_id(0)

    @pl.when(i == 0)
    def _():
        zero_scr[...] = jnp.zeros_like(zero_scr)
        gzero_scr[...] = jnp.zeros_like(gzero_scr)

        def tail_copies(e):
            rows = pl.ds(pl.multiple_of(jnp.maximum(tail_ref[e], 0), MOE_SUB), MOE_SUB)
            return (pltpu.make_async_copy(zero_scr, xs_ref.at[rows], sem),
                    pltpu.make_async_copy(gzero_scr, gs_ref.at[rows], sem))

        for e in range(N_EXPERTS):
            @pl.when(tail_ref[e] >= 0)
            def _():
                for copy in tail_copies(e):
                    copy.start()
        for e in range(N_EXPERTS):
            @pl.when(tail_ref[e] >= 0)
            def _():
                for copy in tail_copies(e):
                    copy.wait()

    buf = lax.rem(i, 2)
    x = jnp.where(i < tiles_a, xa_ref[...], xb_ref[...])
    slot_vecs = [slot_ref[k:k + 1, :] for k in range(TOP_K)]
    for c in range(MOE_SLOTS // MOE_CHUNK):
        chunk = slice(c * MOE_CHUNK, (c + 1) * MOE_CHUNK)
        ids = c * MOE_CHUNK + lax.broadcasted_iota(jnp.int32, (MOE_CHUNK, MOE_TM), 0)
        hit = jnp.zeros((MOE_CHUNK, MOE_TM), F32)
        gval = jnp.zeros((MOE_CHUNK, MOE_TM), F32)
        for k in range(TOP_K):
            match = ids == slot_vecs[k]
            hit = jnp.where(match, 1.0, hit)
            gval = jnp.where(match, gate_ref[k:k + 1, :], gval)
        sorted_scr[buf, chunk, :] = jnp.dot(hit.astype(BF16), x,
                                            preferred_element_type=F32).astype(BF16)
        gsorted_scr[buf, chunk, :] = jnp.broadcast_to(jnp.sum(gval, axis=1, keepdims=True), (MOE_CHUNK, HEAD))

    layout_refs = (seg_start_ref, seg_gran_ref, loc_off_ref)

    def copies(tile, b, start):
        streams = (
            (lambda loc, glob: sorted_scr.at[b, pl.ds(loc, MOE_GRAN)],
             lambda loc, glob: xs_ref.at[pl.ds(glob, MOE_GRAN)]),
            (lambda loc, glob: gsorted_scr.at[b, pl.ds(loc, MOE_GRAN)],
             lambda loc, glob: gs_ref.at[pl.ds(glob, MOE_GRAN)]),
        )
        for src_of, dst_of in streams:
            _segment_copies(layout_refs, tile, src_of, dst_of, buf_sem.at[b], start=start)

    copies(i, buf, True)

    @pl.when(i > 0)
    def _():
        copies(i - 1, 1 - buf, False)

    @pl.when(i == pl.num_programs(0) - 1)
    def _():
        copies(i, buf, False)


def _dispatch(layout, slot, gate, hn_a, hn_b, n_rows):
    tiles_a, tiles_b = hn_a.shape[0] // MOE_TM, hn_b.shape[0] // MOE_TM
    slot_t = jnp.swapaxes(slot, 1, 2)
    gate_t = jnp.swapaxes(gate, 1, 2)
    grid_spec = pltpu.PrefetchScalarGridSpec(
        num_scalar_prefetch=4,
        grid=(tiles_a + tiles_b,),
        in_specs=[
            pl.BlockSpec((None, TOP_K, MOE_TM), lambda i, *_: (i, 0, 0)),
            pl.BlockSpec((None, TOP_K, MOE_TM), lambda i, *_: (i, 0, 0)),
            pl.BlockSpec((MOE_TM, D_MODEL), lambda i, *_: (jnp.minimum(i, tiles_a - 1), 0)),
            pl.BlockSpec((MOE_TM, D_MODEL), lambda i, *_: (jnp.maximum(i - tiles_a, 0), 0)),
        ],
        out_specs=(pl.BlockSpec(memory_space=pl.ANY), pl.BlockSpec(memory_space=pl.ANY)),
        scratch_shapes=[
            pltpu.VMEM((2, MOE_SLOTS, D_MODEL), BF16),
            pltpu.VMEM((2, MOE_SLOTS, HEAD), F32),
            pltpu.VMEM((MOE_SUB, D_MODEL), BF16),
            pltpu.VMEM((MOE_SUB, HEAD), F32),
            pltpu.SemaphoreType.DMA,
            pltpu.SemaphoreType.DMA((2,)),
        ],
    )
    return pl.pallas_call(
        functools.partial(_dispatch_kernel, tiles_a=tiles_a),
        out_shape=(jax.ShapeDtypeStruct((n_rows, D_MODEL), BF16), jax.ShapeDtypeStruct((n_rows, HEAD), F32)),
        grid_spec=grid_spec,
        compiler_params=_cparams(("arbitrary",)),
        name="moe_dispatch",
    )(layout["seg_start"], layout["seg_gran"], layout["loc_off"], layout["tail_row"], slot_t, gate_t, hn_a, hn_b)


def _moe_kernel(blk_e_ref, blk_nsub_ref, n_used_ref, xs_ref, gate_ref, wg_ref, wl_ref, bg_ref, bl_ref,
                wd_ref, bd_ref, yb_ref, wu_s, wd_s, act_s, y_s, pend_ref, sem):
    i = pl.program_id(0)
    f = pl.program_id(1)
    last_step = jnp.logical_and(i == pl.num_programs(0) - 1, f == MOE_NF - 1)

    def y_copies(blk, sb):
        rows = pl.ds(pl.multiple_of(sb * MOE_SUB, MOE_SUB), MOE_SUB)
        dst = pl.ds(pl.multiple_of(blk * MOE_RB + sb * MOE_SUB, MOE_SUB), MOE_SUB)
        return [pltpu.make_async_copy(y_s.at[part, rows], yb_ref.at[part, dst], sem) for part in range(2)]

    def drain():
        blk = pend_ref[1]

        def body(sb, c):
            for copy in y_copies(blk, sb):
                copy.wait()
            return c

        lax.fori_loop(0, pend_ref[0], body, 0)
        pend_ref[0] = 0

    @pl.when(jnp.logical_and(i == 0, f == 0))
    def _():
        pend_ref[0] = 0
        pend_ref[1] = 0

    def over_granules(nsub, fn):
        def octet(sb, c):
            fn(sb * (8 * MOE_SUB), 4 * MOE_SUB)
            fn(sb * (8 * MOE_SUB) + 4 * MOE_SUB, 4 * MOE_SUB)
            return c

        lax.fori_loop(0, nsub // 8, octet, 0)
        rem8 = nsub % 8
        rem = nsub % 4
        done = nsub - rem

        @pl.when(rem8 >= 4)
        def _():
            fn((nsub - rem8) * MOE_SUB, 4 * MOE_SUB)

        @pl.when(rem >= 2)
        def _():
            fn(done * MOE_SUB, 2 * MOE_SUB)

        @pl.when(rem % 2 == 1)
        def _():
            fn((nsub - 1) * MOE_SUB, MOE_SUB)

    @pl.when(i < n_used_ref[0])
    def _():
        wu_s[:, :MOE_TF] = wg_ref[...].astype(BF16)
        wu_s[:, MOE_TF:] = wl_ref[...].astype(BF16)
        wd_s[pl.ds(pl.multiple_of(f * MOE_TF, MOE_TF), MOE_TF), :] = wd_ref[...].astype(BF16)
        b_up = jnp.concatenate([bg_ref[...], bl_ref[...]], axis=1)
        nsub = blk_nsub_ref[i]

        def up(start, size):
            rows = pl.ds(pl.multiple_of(start, MOE_SUB), size)
            hh = jnp.dot(xs_ref[rows, :], wu_s[...], preferred_element_type=F32) + b_up
            glu = jnp.minimum(hh[:, :MOE_TF], SWIGLU_LIMIT)
            lin = jnp.clip(hh[:, MOE_TF:], -SWIGLU_LIMIT, SWIGLU_LIMIT)
            act_s[f, rows, :] = (glu * _sigmoid(SWIGLU_ALPHA * glu) * (lin + 1.0)).astype(BF16)

        over_granules(nsub, up)

        @pl.when(f == MOE_NF - 1)
        def _():
            drain()

            def down(start, size):
                rows = pl.ds(pl.multiple_of(start, MOE_SUB), size)
                act = jnp.concatenate([act_s[k, rows, :] for k in range(MOE_NF)], axis=1)
                y = (jnp.dot(act, wd_s[...], preferred_element_type=F32) + bd_ref[...]) * gate_ref[rows, 0:1]
                hi = y.astype(BF16)
                y_s[0, rows, :] = hi
                y_s[1, rows, :] = (y - hi.astype(F32)).astype(BF16)

            over_granules(nsub, down)

            def send(sb, c):
                for copy in y_copies(i, sb):
                    copy.start()
                return c

            lax.fori_loop(0, nsub, send, 0)
            pend_ref[0] = nsub
            pend_ref[1] = i

    @pl.when(last_step)
    def _():
        drain()


def _moe_experts(blk_e, blk_nsub, n_used, xs, gate_rows, w_up, b_up, w_down, b_down):
    n_blk = xs.shape[0] // MOE_RB

    def live(i, n_used_ref):
        return jnp.minimum(i, n_used_ref[0] - 1)

    def ff(i, f, n_used_ref):
        return jnp.where(i < n_used_ref[0], f, MOE_NF - 1)

    def expert(i, be, nu):
        return be[live(i, nu)]

    grid_spec = pltpu.PrefetchScalarGridSpec(
        num_scalar_prefetch=3,
        grid=(n_blk, MOE_NF),
        in_specs=[
            pl.BlockSpec((MOE_RB, D_MODEL), lambda i, f, be, ns, nu: (live(i, nu), 0)),
            pl.BlockSpec((MOE_RB, HEAD), lambda i, f, be, ns, nu: (live(i, nu), 0)),
            pl.BlockSpec((None, D_MODEL, MOE_TF), lambda i, f, be, ns, nu: (expert(i, be, nu), 0, ff(i, f, nu))),
            pl.BlockSpec((None, D_MODEL, MOE_TF),
                         lambda i, f, be, ns, nu: (expert(i, be, nu), 0, ff(i, f, nu) + MOE_NF)),
            pl.BlockSpec((None, 1, MOE_TF), lambda i, f, be, ns, nu: (expert(i, be, nu), 0, ff(i, f, nu))),
            pl.BlockSpec((None, 1, MOE_TF),
                         lambda i, f, be, ns, nu: (expert(i, be, nu), 0, ff(i, f, nu) + MOE_NF)),
            pl.BlockSpec((None, MOE_TF, D_MODEL), lambda i, f, be, ns, nu: (expert(i, be, nu), ff(i, f, nu), 0)),
            pl.BlockSpec((None, 1, D_MODEL), lambda i, f, be, ns, nu: (expert(i, be, nu), 0, 0)),
        ],
        out_specs=pl.BlockSpec(memory_space=pl.ANY),
        scratch_shapes=[
            pltpu.VMEM((D_MODEL, 2 * MOE_TF), BF16),
            pltpu.VMEM((D_FF, D_MODEL), BF16),
            pltpu.VMEM((MOE_NF, MOE_RB, MOE_TF), BF16),
            pltpu.VMEM((2, MOE_RB, D_MODEL), BF16),
            pltpu.SMEM((2,), jnp.int32),
            pltpu.SemaphoreType.DMA,
        ],
    )
    b_up3 = b_up.reshape(N_EXPERTS, 1, 2 * D_FF)
    return pl.pallas_call(
        _moe_kernel,
        out_shape=jax.ShapeDtypeStruct((2,) + xs.shape, BF16),
        grid_spec=grid_spec,
        compiler_params=_cparams(("arbitrary", "arbitrary")),
        name="moe_experts",
    )(blk_e, blk_nsub, n_used, xs, gate_rows, w_up, w_up, b_up3, b_up3, w_down,
      b_down.reshape(N_EXPERTS, 1, D_MODEL))


def _combine_kernel(seg_start_ref, seg_gran_ref, loc_off_ref, slot_ref, h_ref, yb_ref, y_ref, rows_scr, sem, *,
                    tile0):
    i = pl.program_id(0)

    @pl.when(i == 0)
    def _():
        rows_scr[...] = jnp.zeros_like(rows_scr)

    tile = tile0 + i
    per_chunk = MOE_CHUNK // MOE_GRAN
    last = tile * N_EXPERTS + N_EXPERTS - 1
    n_gran = loc_off_ref[last] // MOE_GRAN + seg_gran_ref[last]
    for e in range(N_EXPERTS):
        idx = tile * N_EXPERTS + e
        loc0 = loc_off_ref[idx]
        glob0 = seg_start_ref[idx]

        def issue(c, carry, loc0=loc0, glob0=glob0):
            loc = pl.multiple_of(loc0 + c * MOE_GRAN, MOE_GRAN)
            glob = pl.multiple_of(glob0 + c * MOE_GRAN, MOE_GRAN)
            for part in range(2):
                pltpu.make_async_copy(yb_ref.at[part, pl.ds(glob, MOE_GRAN)],
                                      rows_scr.at[part, pl.ds(loc, MOE_GRAN)], sem.at[loc // MOE_CHUNK]).start()
            return carry

        lax.fori_loop(0, seg_gran_ref[idx], issue, 0)

    slot_vecs = [slot_ref[:, k:k + 1] for k in range(TOP_K)]
    n_chunks = MOE_SLOTS // MOE_CHUNK
    y_ref[...] = h_ref[...]
    for half in (range(0, n_chunks // 2), range(n_chunks // 2, n_chunks)):
        for c in half:
            def landed(g, carry, c=c):
                for part in range(2):
                    pltpu.make_async_copy(yb_ref.at[part, pl.ds(0, MOE_GRAN)],
                                          rows_scr.at[part, pl.ds(c * MOE_CHUNK, MOE_GRAN)], sem.at[c]).wait()
                return carry

            lax.fori_loop(0, jnp.clip(n_gran - c * per_chunk, 0, per_chunk), landed, 0)
        for c in half:
            pick = _slot_onehot(slot_vecs, c * MOE_CHUNK, (MOE_TM, MOE_CHUNK), 1)
            chunk = slice(c * MOE_CHUNK, (c + 1) * MOE_CHUNK)
            y_ref[...] += (jnp.dot(pick, rows_scr[0, chunk, :], preferred_element_type=F32)
                           + jnp.dot(pick, rows_scr[1, chunk, :], preferred_element_type=F32))


def _combine(layout, slot, h, yb, tile0):
    m = h.shape[0]
    grid_spec = pltpu.PrefetchScalarGridSpec(
        num_scalar_prefetch=3,
        grid=(m // MOE_TM,),
        in_specs=[
            pl.BlockSpec((None, MOE_TM, TOP_K), lambda i, *_: (i + tile0, 0, 0)),
            pl.BlockSpec((MOE_TM, D_MODEL), lambda i, *_: (i, 0)),
            pl.BlockSpec(memory_space=pl.ANY),
        ],
        out_specs=pl.BlockSpec((MOE_TM, D_MODEL), lambda i, *_: (i, 0)),
        scratch_shapes=[pltpu.VMEM((2, MOE_SLOTS, D_MODEL), BF16),
                        pltpu.SemaphoreType.DMA((MOE_SLOTS // MOE_CHUNK,))],
    )
    return pl.pallas_call(
        functools.partial(_combine_kernel, tile0=tile0),
        out_shape=jax.ShapeDtypeStruct((m, D_MODEL), F32),
        grid_spec=grid_spec,
        compiler_params=_cparams(("arbitrary",)),
        name="moe_combine",
    )(layout["seg_start"], layout["seg_gran"], layout["loc_off"], slot, h, yb)


def kernel(x_prompt, x_sample, cache_kv_w128, cache_kv_w512, cache_kv_w2048, state_hgrn, rel_bias, attn_norm,
           w_in, q_norm, k_norm, lb_raw, g_norm, w_pa, w_pb, w_o, ffn_norm, w_router, b_router, w_up, b_up,
           w_down, b_down):
    assert attn_norm.shape[0] == 1, "single-layer stack"
    bsz, seq, _ = x_prompt.shape
    nseq, t_new, _ = x_sample.shape
    n_p, n_s = bsz * seq, nseq * t_new
    lb = jax.nn.softmax(lb_raw.astype(F32), axis=0)[0]
    wpa, wpb, wo = w_pa[0].astype(BF16), w_pb[0].astype(BF16), w_o[0].astype(BF16)
    wr = jnp.pad(w_router[0].astype(F32), ((0, 0), (0, RT_LANES - N_EXPERTS)))
    wr_hi = wr.astype(BF16)
    wr_lo = (wr - wr_hi.astype(F32)).astype(BF16)
    br = jnp.pad(b_router[0].astype(F32), (0, RT_LANES - N_EXPERTS)).reshape(1, RT_LANES)
    caches = (cache_kv_w128[0], cache_kv_w512[0], cache_kv_w2048[0])

    xp = x_prompt.reshape(n_p, D_MODEL)
    xs = x_sample.reshape(n_s, D_MODEL)

    proj_p = _inproj(xp, attn_norm[0], w_in[0])
    qn_p, *kv_p, kvt_p = _qkv(proj_p, q_norm[0], k_norm[0])
    oa_p = _attn_prompt(qn_p, kv_p, jnp.stack([_prompt_bias(rel_bias, g) for g in range(3)]), bsz, seq)
    ob_p, st_p = _hgrn_prompt(proj_p, lb, g_norm[0], bsz, seq)
    mixed_p = _mix(oa_p, ob_p, proj_p, wpa, wpb)
    h_p, hn_p, lg_p = _resid(xp, mixed_p, wo, ffn_norm[0], wr_hi, wr_lo, br)

    proj_s = _inproj(xs, attn_norm[0], w_in[0])
    qn_s, *kv_s, _ = _qkv(proj_s, q_norm[0], k_norm[0])
    oa_s = _attn_sample(qn_s, kv_s, caches, rel_bias, t_new)
    ob_s, st_s = _hgrn_sample(proj_s, lb, g_norm[0], state_hgrn[0], t_new)
    mixed_s = _mix(oa_s, ob_s, proj_s, wpa, wpb)
    h_s, hn_s, lg_s = _resid(xs, mixed_s, wo, ffn_norm[0], wr_hi, wr_lo, br)

    logits = jnp.concatenate([lg_p, lg_s], axis=0)[:, :N_EXPERTS]
    assert n_p % MOE_TM == 0 and n_s % MOE_TM == 0, "each pass must fill whole MoE token tiles"
    blk_e, blk_nsub, n_used, layout, slot, gate, n_blk = _route(logits)
    xs_sorted, gate_rows = _dispatch(layout, slot, gate, hn_p, hn_s, n_blk * MOE_RB)
    yb = _moe_experts(blk_e, blk_nsub, n_used, xs_sorted, gate_rows, w_up[0], b_up[0], w_down[0], b_down[0])
    y_p = _combine(layout, slot, h_p, yb, 0)
    y_s = _combine(layout, slot, h_s, yb, n_p // MOE_TM)

    def prompt_rows(kv, win):
        keep = min(win, seq)
        rows = kv.reshape(bsz, seq, 2 * A_OUT)[:, seq - keep:]
        return rows.reshape(1, bsz, keep, 2, A_HPG, HEAD)

    kvp = [prompt_rows(kv_p[g], A_GROUPS[g][0]) for g in range(2)]
    wide = min(A_GROUPS[2][0], seq)
    kvp.append(kvt_p.reshape(bsz, seq, 2, A_HPG, HEAD)[None, :, seq - wide:])
    kvs = [kv_s[g].reshape(1, nseq, t_new, 2, A_HPG, HEAD) for g in range(3)]
    return (y_p.reshape(bsz, seq, D_MODEL), y_s.reshape(nseq, t_new, D_MODEL),
            kvp[0], kvp[1], kvp[2], st_p[None],
            kvs[0], kvs[1], kvs[2], st_s[None])
```

```python
import functools

import numpy as np
import jax
import jax.numpy as jnp
from jax import lax
from jax.experimental import pallas as pl
from jax.experimental.pallas import tpu as pltpu

F32 = jnp.float32
BF16 = jnp.bfloat16

D_MODEL = 2048
HEAD = 128
A_GROUPS = ((128, 1), (512, 4), (2048, 16))
A_HPG = 4
A_KEYS = 128
A_WIDTH = 3 * A_HPG * HEAD
A_OUT = A_HPG * HEAD
B_HEADS = 8
B_WIDTH = B_HEADS * HEAD
B_CHUNK = 32
REL_BUCKETS = 32
REL_MAX_DIST = 2048
N_EXPERTS = 32
TOP_K = 4
D_FF = 2048
SWIGLU_ALPHA = 1.702
SWIGLU_LIMIT = 7.0
NORM_EPS = 1e-6
NEG_INF = -1e30
ATTN_SCALE = HEAD ** -0.5
IN_WIDTH = 3 * A_WIDTH + 4 * B_WIDTH + 2 * D_MODEL
OFF_Q, OFF_K, OFF_V = 0, A_WIDTH, 2 * A_WIDTH
OFF_QB = 3 * A_WIDTH
OFF_FB, OFF_IB, OFF_OB = OFF_QB + B_WIDTH, OFF_QB + 2 * B_WIDTH, OFF_QB + 3 * B_WIDTH
OFF_GA = OFF_QB + 4 * B_WIDTH
OFF_GB = OFF_GA + D_MODEL

V7X_VMEM_LIMIT = 56 * 1024 * 1024
SUBLANES = 8

NT_DIMS = (((1,), (1,)), ((), ()))


def _sigmoid(x):
    return 1.0 / (1.0 + jnp.exp(-x))


def _cparams(sem):
    return pltpu.CompilerParams(dimension_semantics=sem, vmem_limit_bytes=V7X_VMEM_LIMIT)


NORM_TM = 512
PROJ_TM = 1024
PROJ_TN = 1280


def _rmsnorm_kernel(x_ref, nw_ref, o_ref):
    x = x_ref[...]
    ms = jnp.mean(x * x, axis=-1, keepdims=True)
    o_ref[...] = (x * lax.rsqrt(ms + NORM_EPS) * nw_ref[...]).astype(BF16)


def _inproj_kernel(xn_ref, w_ref, o_ref, wbf_ref):
    @pl.when(pl.program_id(1) == 0)
    def _():
        wbf_ref[...] = w_ref[...].astype(BF16)

    o_ref[...] = jnp.dot(xn_ref[...], wbf_ref[...], preferred_element_type=F32)


def _inproj(x, norm_w, w_in):
    m = x.shape[0]
    xn = pl.pallas_call(
        _rmsnorm_kernel,
        out_shape=jax.ShapeDtypeStruct((m, D_MODEL), BF16),
        grid=(m // NORM_TM,),
        in_specs=[pl.BlockSpec((NORM_TM, D_MODEL), lambda i: (i, 0)), pl.BlockSpec((1, D_MODEL), lambda i: (0, 0))],
        out_specs=pl.BlockSpec((NORM_TM, D_MODEL), lambda i: (i, 0)),
        compiler_params=_cparams(("arbitrary",)),
        name="attn_rmsnorm",
    )(x, norm_w.reshape(1, D_MODEL))
    tm = min(PROJ_TM, m)
    return pl.pallas_call(
        _inproj_kernel,
        out_shape=jax.ShapeDtypeStruct((m, IN_WIDTH), F32),
        grid=(IN_WIDTH // PROJ_TN, m // tm),
        in_specs=[
            pl.BlockSpec((tm, D_MODEL), lambda j, i: (i, 0)),
            pl.BlockSpec((D_MODEL, PROJ_TN), lambda j, i: (0, j)),
        ],
        out_specs=pl.BlockSpec((tm, PROJ_TN), lambda j, i: (i, j)),
        scratch_shapes=[pltpu.VMEM((D_MODEL, PROJ_TN), BF16)],
        compiler_params=_cparams(("arbitrary", "arbitrary")),
        name="inproj",
    )(xn, w_in)


QKV_TM = 512


def _qkv_kernel(q_ref, k_ref, v_ref, qw_ref, kw_ref, qn_ref, kv0_ref, kv1_ref, kv2_ref, kvt_ref):
    kv_refs = (kv0_ref, kv1_ref, kv2_ref)
    qw = qw_ref[...]
    kw = kw_ref[...]
    for h in range(3 * A_HPG):
        sl = slice(h * HEAD, (h + 1) * HEAD)
        q = q_ref[:, sl]
        qn_ref[:, sl] = q * lax.rsqrt(jnp.mean(q * q, axis=-1, keepdims=True) + NORM_EPS) * qw
        k = k_ref[:, sl]
        kn = k * lax.rsqrt(jnp.mean(k * k, axis=-1, keepdims=True) + NORM_EPS) * kw
        g, hh = divmod(h, A_HPG)
        kv_refs[g][:, hh * HEAD:(hh + 1) * HEAD] = kn
        kv_refs[g][:, A_OUT + hh * HEAD:A_OUT + (hh + 1) * HEAD] = v_ref[:, sl]
        if g == 2:
            kvt_ref[:, hh, :] = kn
            kvt_ref[:, A_HPG + hh, :] = v_ref[:, sl]


def _qkv(proj, q_norm, k_norm):
    m = proj.shape[0]
    kv_shape = jax.ShapeDtypeStruct((m, 2 * A_OUT), F32)
    return pl.pallas_call(
        _qkv_kernel,
        out_shape=(jax.ShapeDtypeStruct((m, A_WIDTH), F32), kv_shape, kv_shape, kv_shape,
                   jax.ShapeDtypeStruct((m, 2 * A_HPG, HEAD), F32)),
        grid=(m // QKV_TM,),
        in_specs=[
            pl.BlockSpec((QKV_TM, A_WIDTH), lambda i: (i, OFF_Q // A_WIDTH)),
            pl.BlockSpec((QKV_TM, A_WIDTH), lambda i: (i, OFF_K // A_WIDTH)),
            pl.BlockSpec((QKV_TM, A_WIDTH), lambda i: (i, OFF_V // A_WIDTH)),
            pl.BlockSpec((1, HEAD), lambda i: (0, 0)),
            pl.BlockSpec((1, HEAD), lambda i: (0, 0)),
        ],
        out_specs=(
            pl.BlockSpec((QKV_TM, A_WIDTH), lambda i: (i, 0)),
            pl.BlockSpec((QKV_TM, 2 * A_OUT), lambda i: (i, 0)),
            pl.BlockSpec((QKV_TM, 2 * A_OUT), lambda i: (i, 0)),
            pl.BlockSpec((QKV_TM, 2 * A_OUT), lambda i: (i, 0)),
            pl.BlockSpec((QKV_TM, 2 * A_HPG, HEAD), lambda i: (i, 0, 0)),
        ),
        compiler_params=_cparams(("arbitrary",)),
        name="qkv_norm",
    )(proj, proj, proj, q_norm.reshape(1, HEAD), k_norm.reshape(1, HEAD))


def _t5_bucket(dist):
    dist = np.asarray(dist, np.int64)
    max_exact = REL_BUCKETS // 2
    ratio = np.log(np.maximum(dist, 1) / max_exact) / np.log(REL_MAX_DIST / max_exact)
    large = np.minimum(max_exact + (ratio * (REL_BUCKETS - max_exact)).astype(np.int64), REL_BUCKETS - 1)
    return np.where(dist < max_exact, dist, large).astype(np.int32)


def _bias_lookup(tab, bucket, valid):
    bucket = jnp.asarray(np.where(valid, bucket, -1).astype(np.int32))[None]
    out = jnp.full((tab.shape[1],) + bucket.shape[1:], NEG_INF, F32)
    for b in range(REL_BUCKETS):
        out = jnp.where(bucket == b, tab[b].reshape((-1,) + (1,) * (bucket.ndim - 1)), out)
    return out


def _prompt_bias(rel_bias, g):
    _, dil = A_GROUPS[g]
    n = A_KEYS
    j = n + np.arange(n)[:, None] - np.arange(2 * n)[None, :]
    valid = (j >= 0) & (j < n)
    tab = rel_bias[:, g * A_HPG:(g + 1) * A_HPG].astype(F32)
    return _bias_lookup(tab, _t5_bucket(np.clip(j, 0, n - 1) * dil), valid)


ATT_MIX_ROWS = 256


def _attn_prompt_kernel(q0_ref, q1_ref, q2_ref, k0_ref, v0_ref, k1_ref, v1_ref, k2_ref, v2_ref, bias_ref,
                        o_ref, og_scr, lg_scr, *, seq):
    n = A_KEYS
    h = pl.program_id(1)
    q_refs = (q0_ref, q1_ref, q2_ref)
    k_refs = (k0_ref, k1_ref, k2_ref)
    v_refs = (v0_ref, v1_ref, v2_ref)
    for g, (win, dil) in enumerate(A_GROUPS):
        bias = bias_ref[g, h]
        for r in range(dil):
            for i in range(seq // win):
                def rows(blk, r=r, dil=dil):
                    if dil == 1:
                        return pl.ds(blk * n, n)
                    return pl.ds(r + blk * n * dil, n, stride=dil)

                q = q_refs[g][rows(i), :].astype(BF16)
                if i == 0:
                    kk = k_refs[g][rows(i), :].astype(BF16)
                    vv = v_refs[g][rows(i), :].astype(BF16)
                    b_tile = bias[:, n:]
                else:
                    kk = jnp.concatenate([k_refs[g][rows(i - 1), :], k_refs[g][rows(i), :]], axis=0).astype(BF16)
                    vv = jnp.concatenate([v_refs[g][rows(i - 1), :], v_refs[g][rows(i), :]], axis=0).astype(BF16)
                    b_tile = bias
                s = lax.dot_general(q, kk, NT_DIMS, preferred_element_type=F32) * ATTN_SCALE + b_tile
                mx = jnp.max(s, axis=-1, keepdims=True)
                p = jnp.exp(s - mx)
                den = jnp.sum(p, axis=-1, keepdims=True)
                og_scr[g, rows(i), :] = jnp.dot(p.astype(BF16), vv, preferred_element_type=F32) / den
                lg_scr[g, rows(i), :] = jnp.broadcast_to(mx + jnp.log(den), (n, HEAD))
    for c in range(seq // ATT_MIX_ROWS):
        sl = slice(c * ATT_MIX_ROWS, (c + 1) * ATT_MIX_ROWS)
        la, lb_, lc = lg_scr[0, sl, :], lg_scr[1, sl, :], lg_scr[2, sl, :]
        top = jnp.maximum(jnp.maximum(la, lb_), lc)
        ea, eb, ec = jnp.exp(la - top), jnp.exp(lb_ - top), jnp.exp(lc - top)
        o_ref[sl, :] = (ea * og_scr[0, sl, :] + eb * og_scr[1, sl, :] + ec * og_scr[2, sl, :]) / (ea + eb + ec)


def _attn_prompt(qn, kvs, bias, bsz, seq):
    m = qn.shape[0]
    col = lambda fn: pl.BlockSpec((seq, HEAD), lambda b, h: (b, fn(h)))
    in_specs = [col(lambda h, g=g: g * A_HPG + h) for g in range(3)]
    for _ in range(3):
        in_specs += [col(lambda h: h), col(lambda h: A_HPG + h)]
    in_specs.append(pl.BlockSpec(bias.shape, lambda b, h: (0, 0, 0, 0)))
    return pl.pallas_call(
        functools.partial(_attn_prompt_kernel, seq=seq),
        out_shape=jax.ShapeDtypeStruct((m, A_OUT), F32),
        grid=(bsz, A_HPG),
        in_specs=in_specs,
        out_specs=col(lambda h: h),
        scratch_shapes=[pltpu.VMEM((3, seq, HEAD), F32), pltpu.VMEM((3, seq, HEAD), F32)],
        compiler_params=_cparams(("arbitrary", "arbitrary")),
        name="attn_prompt",
    )(qn, qn, qn, kvs[0], kvs[0], kvs[1], kvs[1], kvs[2], kvs[2], bias)


def _hgrn_gates(q, z, lb):
    qs = q * _sigmoid(q)
    logf = jnp.log(lb + (1.0 - lb) * _sigmoid(z))
    kb = (1.0 - lb) * _sigmoid(-z)
    return qs, kb, logf


def _hgrn_out(o, gate, gn):
    on = o * lax.rsqrt(jnp.mean(o * o, axis=-1, keepdims=True) + NORM_EPS) * gn
    return (on * (gate * _sigmoid(gate))).astype(BF16)


def _split3(x):
    hi = x.astype(BF16)
    r1 = x - hi.astype(F32)
    mid = r1.astype(BF16)
    lo = (r1 - mid.astype(F32)).astype(BF16)
    return hi, mid, lo


HG_TC = 256
HG_NC = HG_TC // B_CHUNK
HG_HPS = 4


def _hgrn_prompt_kernel(q_ref, f_ref, i_ref, g_ref, lb_ref, gn_ref, ob_ref, st_ref, s_scr):
    tb = pl.program_id(2)

    @pl.when(tb == 0)
    def _():
        s_scr[...] = jnp.zeros_like(s_scr)

    row = lax.broadcasted_iota(jnp.int32, (HG_TC, HG_TC), 0)
    col = lax.broadcasted_iota(jnp.int32, (HG_TC, HG_TC), 1)
    same_chunk = (row // B_CHUNK) == (col // B_CHUNK)
    causal = jnp.logical_and(same_chunk, col <= row)
    cum_mat = jnp.where(causal, 1.0, 0.0).astype(BF16)
    tcol = lax.broadcasted_iota(jnp.int32, (HEAD, HG_TC), 1) // B_CHUNK
    trow = lax.broadcasted_iota(jnp.int32, (HG_TC, HEAD), 0) // B_CHUNK
    gn = gn_ref[...]
    for h in range(HG_HPS):
        sl = slice(h * HEAD, (h + 1) * HEAD)
        qs, kb, logf = _hgrn_gates(q_ref[:, sl], f_ref[:, sl], lb_ref[:, sl])
        b3 = jnp.dot(cum_mat, jnp.concatenate(_split3(logf), axis=1), preferred_element_type=F32)
        b = b3[:, :HEAD] + b3[:, HEAD:2 * HEAD] + b3[:, 2 * HEAD:]
        bl_rows = [b[c * B_CHUNK + B_CHUNK - 1:(c + 1) * B_CHUNK, :] for c in range(HG_NC)]
        bl = jnp.concatenate([jnp.broadcast_to(r, (B_CHUNK, HEAD)) for r in bl_rows], axis=0)
        q_dec_f = qs * jnp.exp(b)
        q_dec = q_dec_f.astype(BF16)
        k_inv = (kb * jnp.exp(-b)).astype(BF16)
        k_end_t = jnp.transpose(kb * jnp.exp(bl - b))
        v = i_ref[:, sl].astype(BF16)
        att = lax.dot_general(q_dec, k_inv, NT_DIMS, preferred_element_type=F32)
        att = jnp.where(causal, att, 0.0).astype(BF16)
        o_intra = jnp.dot(att, v, preferred_element_type=F32)
        decay_t = jnp.exp(jnp.transpose(bl))
        ke_stack = jnp.concatenate([jnp.where(tcol == c, k_end_t, 0.0) for c in range(HG_NC)], axis=0)
        ds_all = jnp.dot(ke_stack.astype(BF16), v, preferred_element_type=F32)
        s = s_scr[h]
        starts = []
        for c in range(HG_NC):
            starts.append(s)
            s = decay_t[:, c * B_CHUNK:c * B_CHUNK + 1] * s + ds_all[c * HEAD:(c + 1) * HEAD, :]
        s_scr[h] = s
        q_bd = jnp.concatenate([jnp.where(trow == c, q_dec_f, 0.0) for c in range(HG_NC)], axis=1).astype(BF16)
        o_inter = jnp.dot(q_bd, jnp.concatenate(starts, axis=0).astype(BF16), preferred_element_type=F32)
        ob_ref[:, sl] = _hgrn_out(o_intra + o_inter, g_ref[:, sl], gn)

    @pl.when(tb == pl.num_programs(2) - 1)
    def _():
        st_ref[...] = s_scr[...]


def _hgrn_prompt(proj, lb, g_norm, bsz, seq):
    m = proj.shape[0]
    hw = HG_HPS * HEAD
    nt = seq // HG_TC
    rowblk = lambda b, hf, t: b * nt + t
    spec = lambda off: pl.BlockSpec((HG_TC, hw), lambda b, hf, t: (rowblk(b, hf, t), off // hw + hf))
    return pl.pallas_call(
        _hgrn_prompt_kernel,
        out_shape=(jax.ShapeDtypeStruct((m, B_WIDTH), BF16),
                   jax.ShapeDtypeStruct((bsz, B_HEADS, HEAD, HEAD), F32)),
        grid=(bsz, B_HEADS // HG_HPS, nt),
        in_specs=[
            spec(OFF_QB), spec(OFF_FB), spec(OFF_IB), spec(OFF_OB),
            pl.BlockSpec((1, hw), lambda b, hf, t: (0, hf)),
            pl.BlockSpec((1, HEAD), lambda b, hf, t: (0, 0)),
        ],
        out_specs=(
            pl.BlockSpec((HG_TC, hw), lambda b, hf, t: (rowblk(b, hf, t), hf)),
            pl.BlockSpec((None, HG_HPS, HEAD, HEAD), lambda b, hf, t: (b, hf, 0, 0)),
        ),
        scratch_shapes=[pltpu.VMEM((HG_HPS, HEAD, HEAD), F32)],
        compiler_params=_cparams(("arbitrary", "arbitrary", "arbitrary")),
        name="hgrn_prompt",
    )(proj, proj, proj, proj, lb.reshape(1, B_WIDTH), g_norm.reshape(1, HEAD))


HS_SEQ = 32


def _hgrn_sample_kernel(q_ref, f_ref, i_ref, g_ref, lb_ref, gn_ref, s0_ref, ob_ref, st_ref, *, t_new):
    rows = HS_SEQ * t_new
    qs, kb, logf = _hgrn_gates(q_ref[...], f_ref[...], lb_ref[...])
    t_of = lax.broadcasted_iota(jnp.int32, (rows, HEAD), 0) % t_new
    b = logf
    sh = 1
    while sh < t_new:
        b = b + jnp.where(t_of >= sh, pltpu.roll(b, sh, axis=0), 0.0)
        sh *= 2
    bl = b
    for back in range(1, t_new):
        bl = jnp.where(t_of == t_new - 1 - back, pltpu.roll(b, rows - back, axis=0), bl)
    q_dec = (qs * jnp.exp(b)).astype(BF16)
    k_inv = (kb * jnp.exp(-b)).astype(BF16)
    k_end_t = jnp.transpose(kb * jnp.exp(bl - b))
    decay_t = jnp.exp(jnp.transpose(bl))
    v = i_ref[...].astype(BF16)
    row = lax.broadcasted_iota(jnp.int32, (rows, rows), 0)
    col = lax.broadcasted_iota(jnp.int32, (rows, rows), 1)
    causal = jnp.logical_and(row // t_new == col // t_new, col <= row)
    att = lax.dot_general(q_dec, k_inv, NT_DIMS, preferred_element_type=F32)
    att = jnp.where(causal, att, 0.0).astype(BF16)
    o_intra = jnp.dot(att, v, preferred_element_type=F32)
    seq_of_col = lax.broadcasted_iota(jnp.int32, (HEAD, rows), 1) // t_new
    seq_of_row = lax.broadcasted_iota(jnp.int32, (rows, HEAD), 0) // t_new
    o_inter = jnp.zeros((rows, HEAD), F32)
    for s_i in range(HS_SEQ):
        s = s0_ref[s_i, 0]
        part = jnp.dot(q_dec, s.astype(BF16), preferred_element_type=F32)
        o_inter = jnp.where(seq_of_row == s_i, part, o_inter)
        ke = jnp.where(seq_of_col == s_i, k_end_t, 0.0).astype(BF16)
        ds = jnp.dot(ke, v, preferred_element_type=F32)
        st_ref[s_i, 0] = decay_t[:, s_i * t_new:s_i * t_new + 1] * s + ds
    ob_ref[...] = _hgrn_out(o_intra + o_inter, g_ref[...], gn_ref[...])


def _hgrn_sample(proj, lb, g_norm, s0, t_new):
    m = proj.shape[0]
    nseq = m // t_new
    rows = HS_SEQ * t_new
    spec = lambda off: pl.BlockSpec((rows, HEAD), lambda i, h: (i, off // HEAD + h))
    st_spec = pl.BlockSpec((HS_SEQ, 1, HEAD, HEAD), lambda i, h: (i, h, 0, 0))
    return pl.pallas_call(
        functools.partial(_hgrn_sample_kernel, t_new=t_new),
        out_shape=(jax.ShapeDtypeStruct((m, B_WIDTH), BF16),
                   jax.ShapeDtypeStruct((nseq, B_HEADS, HEAD, HEAD), F32)),
        grid=(nseq // HS_SEQ, B_HEADS),
        in_specs=[
            spec(OFF_QB), spec(OFF_FB), spec(OFF_IB), spec(OFF_OB),
            pl.BlockSpec((1, HEAD), lambda i, h: (0, h)),
            pl.BlockSpec((1, HEAD), lambda i, h: (0, 0)),
            st_spec,
        ],
        out_specs=(pl.BlockSpec((rows, HEAD), lambda i, h: (i, h)), st_spec),
        compiler_params=_cparams(("arbitrary", "arbitrary")),
        name="hgrn_sample",
    )(proj, proj, proj, proj, lb.reshape(1, B_WIDTH), g_norm.reshape(1, HEAD), s0)


SA_SEQ = 2
KV_ROWS = 2 * A_HPG


def _sample_bias(rel_bias, t_new):
    def tile(tab, dist, valid):
        bias = _bias_lookup(tab, _t5_bucket(np.clip(dist, 0, REL_MAX_DIST)), valid)
        bias = jnp.moveaxis(bias, 0, -1)
        bias = jnp.concatenate([bias, jnp.zeros(bias.shape[:-1] + (KV_ROWS - A_HPG,), F32)], axis=-1)
        return jnp.broadcast_to(bias[..., None], bias.shape + (HEAD,))

    t = np.arange(t_new)[:, None]
    r = np.arange(A_KEYS + t_new)[None, :]
    is_new = r >= A_KEYS
    dist = np.where(is_new, t - (r - A_KEYS), A_KEYS + t - r)
    valid = (dist >= 0) & (dist < A_KEYS)
    b0 = tile(rel_bias[:, 0:A_HPG].astype(F32), dist, valid)
    b12 = []
    for g in (1, 2):
        _, dil = A_GROUPS[g]
        m = np.arange(A_KEYS + 1)
        dist = dil * (A_KEYS - m)
        b12.append(tile(rel_bias[:, g * A_HPG:(g + 1) * A_HPG].astype(F32), dist, m >= 1))
    return b0, jnp.stack(b12)


def _attn_sample_kernel(q_ref, n0_ref, n1_ref, n2_ref, c0_ref, c1_ref, c2_ref, b0_ref, b12_ref,
                        o_ref, qt_scr, ns_scr, *, t_new):
    nq = SA_SEQ * t_new
    new_refs = (n0_ref, n1_ref, n2_ref)
    cache_refs = (c0_ref, c1_ref, c2_ref)
    qt_scr[...] = jnp.zeros_like(qt_scr)
    for g in range(3):
        for j in range(nq):
            for h in range(A_HPG):
                qt_scr[g, j, h:h + 1, :] = (
                    q_ref[j:j + 1, g * A_OUT + h * HEAD:g * A_OUT + (h + 1) * HEAD] * ATTN_SCALE)
                ns_scr[g, j, h:h + 1, :] = new_refs[g][j:j + 1, h * HEAD:(h + 1) * HEAD]
                ns_scr[g, j, A_HPG + h:A_HPG + h + 1, :] = (
                    new_refs[g][j:j + 1, A_OUT + h * HEAD:A_OUT + (h + 1) * HEAD])

    row_id = lax.broadcasted_iota(jnp.int32, (nq, HEAD), 0)
    ones = jnp.ones((HEAD, HEAD), BF16)

    def per_query(j, out):
        s = j // t_new
        t = j - s * t_new
        row0 = s * A_KEYS
        stats = []
        for g in range(3):
            c_ref = cache_refs[g]
            qt = qt_scr[g, j]
            if g == 0:
                sub0 = 0
                bias_c = b0_ref[t, 0:A_KEYS]
                new_tiles = [(b0_ref[t, A_KEYS + tp], ns_scr[0, s * t_new + tp]) for tp in range(t_new)]
            else:
                sub0 = pl.multiple_of(t * KV_ROWS, KV_ROWS)
                bias_c = b12_ref[g - 1, 0:A_KEYS]
                new_tiles = [(b12_ref[g - 1, A_KEYS], ns_scr[g, j])]

            def mirror(l):
                return l + pltpu.roll(l, A_HPG, axis=l.ndim - 2)

            tiles = c_ref[pl.ds(row0, A_KEYS), pl.ds(sub0, KV_ROWS), :]
            prod = (tiles * qt[None]).reshape(A_KEYS * KV_ROWS, HEAD).astype(BF16)
            sums = jnp.dot(prod, ones, preferred_element_type=F32).reshape(A_KEYS, KV_ROWS, HEAD)
            l_c = mirror(sums + bias_c)
            new_l = [mirror(jnp.sum(tile_kv * qt, axis=-1, keepdims=True) + bias) for bias, tile_kv in new_tiles]
            mx = jnp.max(l_c, axis=0)
            for l in new_l:
                mx = jnp.maximum(mx, l)
            p_c = jnp.exp(l_c - mx[None])
            den = jnp.sum(p_c, axis=0)
            acc = jnp.sum(p_c * tiles, axis=0)
            for l, (_, tile_kv) in zip(new_l, new_tiles):
                p = jnp.exp(l - mx)
                den = den + p
                acc = acc + p * tile_kv
            stats.append((mx + jnp.log(den), den, acc))
        top = jnp.maximum(jnp.maximum(stats[0][0], stats[1][0]), stats[2][0])
        e = [jnp.exp(st[0] - top) for st in stats]
        tot = e[0] + e[1] + e[2]
        mix = jnp.zeros((KV_ROWS, HEAD), F32)
        for g in range(3):
            mix = mix + e[g] / (tot * stats[g][1]) * stats[g][2]
        pieces = [jnp.broadcast_to(mix[A_HPG + h:A_HPG + h + 1, :], (nq, HEAD)) for h in range(A_HPG)]
        return tuple(jnp.where(row_id == j, pieces[h], out[h]) for h in range(A_HPG))

    init = tuple(jnp.zeros((nq, HEAD), F32) for _ in range(A_HPG))
    out = lax.fori_loop(0, nq, per_query, init)
    for h in range(A_HPG):
        o_ref[:, h * HEAD:(h + 1) * HEAD] = out[h]


def _attn_sample(qn, kv_new, caches, rel_bias, t_new):
    m = qn.shape[0]
    nseq = m // t_new
    nq = SA_SEQ * t_new
    assert nq % SUBLANES == 0, "the new-token rows of one grid step must fill whole sublane tiles"
    kvw = 2 * A_OUT
    views, specs = [], []
    for g, (win, dil) in enumerate(A_GROUPS):
        c = caches[g]
        assert c.shape[1] == win, "cache must hold exactly one window of positions"
        assert g == 0 or dil >= t_new, "each new token must sit in its own residue class"
        views.append(c.reshape(nseq * A_KEYS, dil * KV_ROWS, HEAD))
        specs.append(pl.BlockSpec((SA_SEQ * A_KEYS, min(dil, t_new) * KV_ROWS, HEAD), lambda i: (i, 0, 0)))
    b0, b12 = _sample_bias(rel_bias, t_new)
    new_spec = pl.BlockSpec((nq, kvw), lambda i: (i, 0))
    return pl.pallas_call(
        functools.partial(_attn_sample_kernel, t_new=t_new),
        out_shape=jax.ShapeDtypeStruct((m, A_OUT), F32),
        grid=(nseq // SA_SEQ,),
        in_specs=[
            pl.BlockSpec((nq, A_WIDTH), lambda i: (i, 0)),
            new_spec, new_spec, new_spec,
            specs[0], specs[1], specs[2],
            pl.BlockSpec(b0.shape, lambda i: (0, 0, 0, 0)),
            pl.BlockSpec(b12.shape, lambda i: (0, 0, 0, 0)),
        ],
        out_specs=pl.BlockSpec((nq, A_OUT), lambda i: (i, 0)),
        scratch_shapes=[
            pltpu.VMEM((3, nq, KV_ROWS, HEAD), F32),
            pltpu.VMEM((3, nq, KV_ROWS, HEAD), F32),
        ],
        compiler_params=_cparams(("arbitrary",)),
        name="attn_sample",
    )(qn, kv_new[0], kv_new[1], kv_new[2], views[0], views[1], views[2], b0, b12)


MIX_TM = 512
MIX_TN = 512


MIX_NJ = D_MODEL // MIX_TN


def _mix_kernel(oa_ref, ob_ref, *refs):
    ga_refs, gb_refs = refs[:MIX_NJ], refs[MIX_NJ:2 * MIX_NJ]
    wpa_ref, wpb_ref, out_ref = refs[2 * MIX_NJ:]
    oa = oa_ref[...].astype(BF16)
    ob = ob_ref[...]
    for j in range(MIX_NJ):
        cols = slice(j * MIX_TN, (j + 1) * MIX_TN)
        pa = jnp.dot(oa, wpa_ref[:, cols], preferred_element_type=F32)
        pb = jnp.dot(ob, wpb_ref[:, cols], preferred_element_type=F32)
        out_ref[:, cols] = (_sigmoid(ga_refs[j][...]) * pa + _sigmoid(gb_refs[j][...]) * pb).astype(BF16)


def _mix(oa, ob, proj, w_pa, w_pb):
    m = proj.shape[0]
    gate_specs = [pl.BlockSpec((MIX_TM, MIX_TN), lambda i, off=off, j=j: (i, off // MIX_TN + j))
                  for off in (OFF_GA, OFF_GB) for j in range(MIX_NJ)]
    return pl.pallas_call(
        _mix_kernel,
        out_shape=jax.ShapeDtypeStruct((m, D_MODEL), BF16),
        grid=(m // MIX_TM,),
        in_specs=[
            pl.BlockSpec((MIX_TM, A_OUT), lambda i: (i, 0)),
            pl.BlockSpec((MIX_TM, B_WIDTH), lambda i: (i, 0)),
            *gate_specs,
            pl.BlockSpec((A_OUT, D_MODEL), lambda i: (0, 0)),
            pl.BlockSpec((B_WIDTH, D_MODEL), lambda i: (0, 0)),
        ],
        out_specs=pl.BlockSpec((MIX_TM, D_MODEL), lambda i: (i, 0)),
        compiler_params=_cparams(("arbitrary",)),
        name="mix",
    )(oa, ob, *([proj] * (2 * MIX_NJ)), w_pa, w_pb)


RT_LANES = 128


def _resid_kernel(x_ref, mixed_ref, wo_ref, nw_ref, wr_hi_ref, wr_lo_ref, br_ref, h_ref, hn_ref, lg_ref):
    h = x_ref[...] + jnp.dot(mixed_ref[...], wo_ref[...], preferred_element_type=F32)
    h_ref[...] = h
    hn = h * lax.rsqrt(jnp.mean(h * h, axis=-1, keepdims=True) + NORM_EPS) * nw_ref[...]
    hi = hn.astype(BF16)
    hn_ref[...] = hi
    lo = (hn - hi.astype(F32)).astype(BF16)
    lg = (jnp.dot(hi, wr_hi_ref[...], preferred_element_type=F32)
          + jnp.dot(lo, wr_hi_ref[...], preferred_element_type=F32)
          + jnp.dot(hi, wr_lo_ref[...], preferred_element_type=F32))
    lg_ref[...] = lg + br_ref[...]


def _resid(x, mixed, w_o, ffn_norm, wr_hi, wr_lo, br):
    m = x.shape[0]
    full = lambda shape: pl.BlockSpec(shape, lambda i: (0, 0))
    return pl.pallas_call(
        _resid_kernel,
        out_shape=(jax.ShapeDtypeStruct((m, D_MODEL), F32),
                   jax.ShapeDtypeStruct((m, D_MODEL), BF16),
                   jax.ShapeDtypeStruct((m, RT_LANES), F32)),
        grid=(m // MIX_TM,),
        in_specs=[
            pl.BlockSpec((MIX_TM, D_MODEL), lambda i: (i, 0)),
            pl.BlockSpec((MIX_TM, D_MODEL), lambda i: (i, 0)),
            full((D_MODEL, D_MODEL)), full((1, D_MODEL)),
            full((D_MODEL, RT_LANES)), full((D_MODEL, RT_LANES)), full((1, RT_LANES)),
        ],
        out_specs=(
            pl.BlockSpec((MIX_TM, D_MODEL), lambda i: (i, 0)),
            pl.BlockSpec((MIX_TM, D_MODEL), lambda i: (i, 0)),
            pl.BlockSpec((MIX_TM, RT_LANES), lambda i: (i, 0)),
        ),
        compiler_params=_cparams(("arbitrary",)),
        name="resid_router",
    )(x, mixed, w_o, ffn_norm.reshape(1, D_MODEL), wr_hi, wr_lo, br)


MOE_RB = 1280
MOE_SUB = 128
MOE_TF = 256
MOE_NF = D_FF // MOE_TF


MOE_TM = 512
MOE_GRAN = 16
MOE_SLOTS = 2560
MOE_CHUNK = 256
assert MOE_SLOTS >= MOE_TM * TOP_K + N_EXPERTS * (MOE_GRAN - 1) and MOE_SLOTS % MOE_CHUNK == 0


def _route(logits):
    n_tok = logits.shape[0]
    nt = n_tok // MOE_TM
    top_val, top_idx = lax.top_k(logits, TOP_K)
    gate = jax.nn.softmax(top_val, axis=-1)
    na = MOE_TM * TOP_K
    e3 = top_idx.reshape(nt, na)
    hot = e3[..., None] == jnp.arange(N_EXPERTS)
    onehot = hot.astype(jnp.int32)
    lower = (lax.broadcasted_iota(jnp.int32, (na, na), 0) >= lax.broadcasted_iota(jnp.int32, (na, na), 1))
    incl = jnp.einsum('ij,tjk->tik', lower.astype(BF16), hot.astype(BF16),
                      preferred_element_type=F32).astype(jnp.int32)
    rank = jnp.sum((incl - onehot) * onehot, axis=2)
    seg = (incl[:, -1] + MOE_GRAN - 1) // MOE_GRAN * MOE_GRAN
    loc_off = jnp.cumsum(seg, axis=1) - seg
    rows_e = jnp.sum(seg, axis=0)
    padded_sub = (rows_e + MOE_SUB - 1) // MOE_SUB * MOE_SUB
    padded_rb = (rows_e + MOE_RB - 1) // MOE_RB * MOE_RB
    blk_end = jnp.cumsum(padded_rb) // MOE_RB
    row_start = jnp.cumsum(padded_rb) - padded_rb
    seg_start = row_start[None, :] + jnp.cumsum(seg, axis=0) - seg
    slot = jnp.sum(onehot * loc_off[:, None, :], axis=2) + rank
    max_rows = n_tok * TOP_K + nt * N_EXPERTS * (MOE_GRAN - 1)
    n_blk = max_rows // MOE_RB + N_EXPERTS
    blk_ids = jnp.arange(n_blk)
    blk_e = jnp.minimum(jnp.searchsorted(blk_end, blk_ids, side='right'), N_EXPERTS - 1).astype(jnp.int32)
    rows_left = padded_sub[blk_e] - (blk_ids * MOE_RB - row_start[blk_e])
    blk_nsub = jnp.clip(rows_left // MOE_SUB, 0, MOE_RB // MOE_SUB).astype(jnp.int32)
    n_used = blk_end[-1:].astype(jnp.int32)
    tail_row = jnp.where(rows_e > 0, row_start + padded_sub - MOE_SUB, -1).astype(jnp.int32)
    layout = dict(seg_start=seg_start.reshape(-1).astype(jnp.int32),
                  seg_gran=(seg // MOE_GRAN).reshape(-1).astype(jnp.int32),
                  loc_off=loc_off.reshape(-1).astype(jnp.int32),
                  tail_row=tail_row)
    slot = slot.reshape(nt, MOE_TM, TOP_K).astype(jnp.int32)
    return blk_e, blk_nsub, n_used, layout, slot, gate.reshape(nt, MOE_TM, TOP_K), n_blk


def _segment_copies(layout_refs, tile, src_of, dst_of, sem, start):
    seg_start_ref, seg_gran_ref, loc_off_ref = layout_refs

    for e in range(N_EXPERTS):
        idx = tile * N_EXPERTS + e
        loc = loc_off_ref[idx]
        glob = seg_start_ref[idx]

        def body(c, carry, loc=loc, glob=glob):
            copy = pltpu.make_async_copy(
                src_of(pl.multiple_of(loc + c * MOE_GRAN, MOE_GRAN), pl.multiple_of(glob + c * MOE_GRAN, MOE_GRAN)),
                dst_of(pl.multiple_of(loc + c * MOE_GRAN, MOE_GRAN), pl.multiple_of(glob + c * MOE_GRAN, MOE_GRAN)),
                sem)
            if start:
                copy.start()
            else:
                copy.wait()
            return carry

        lax.fori_loop(0, seg_gran_ref[idx], body, 0)


def _slot_onehot(slot_vecs, base, shape, axis):
    ids = base + lax.broadcasted_iota(jnp.int32, shape, axis)
    hit = jnp.zeros(shape, F32)
    for v in slot_vecs:
        hit = jnp.where(ids == v, 1.0, hit)
    return hit.astype(BF16)


def _dispatch_kernel(seg_start_ref, seg_gran_ref, loc_off_ref, tail_ref, slot_ref, gate_ref, xa_ref, xb_ref,
                     xs_ref, gs_ref, sorted_scr, gsorted_scr, zero_scr, gzero_scr, sem, buf_sem, *, tiles_a):
    i = pl.program_id(0)

    @pl.when(i == 0)
    def _():
        zero_scr[...] = jnp.zeros_like(zero_scr)
        gzero_scr[...] = jnp.zeros_like(gzero_scr)

        def tail_copies(e):
            rows = pl.ds(pl.multiple_of(jnp.maximum(tail_ref[e], 0), MOE_SUB), MOE_SUB)
            return (pltpu.make_async_copy(zero_scr, xs_ref.at[rows], sem),
                    pltpu.make_async_copy(gzero_scr, gs_ref.at[rows], sem))

        for e in range(N_EXPERTS):
            @pl.when(tail_ref[e] >= 0)
            def _():
                for copy in tail_copies(e):
                    copy.start()
        for e in range(N_EXPERTS):
            @pl.when(tail_ref[e] >= 0)
            def _():
                for copy in tail_copies(e):
                    copy.wait()

    buf = lax.rem(i, 2)
    x = jnp.where(i < tiles_a, xa_ref[...], xb_ref[...])
    slot_vecs = [slot_ref[k:k + 1, :] for k in range(TOP_K)]
    for c in range(MOE_SLOTS // MOE_CHUNK):
        chunk = slice(c * MOE_CHUNK, (c + 1) * MOE_CHUNK)
        ids = c * MOE_CHUNK + lax.broadcasted_iota(jnp.int32, (MOE_CHUNK, MOE_TM), 0)
        hit = jnp.zeros((MOE_CHUNK, MOE_TM), F32)
        gval = jnp.zeros((MOE_CHUNK, MOE_TM), F32)
        for k in range(TOP_K):
            match = ids == slot_vecs[k]
            hit = jnp.where(match, 1.0, hit)
            gval = jnp.where(match, gate_ref[k:k + 1, :], gval)
        sorted_scr[buf, chunk, :] = jnp.dot(hit.astype(BF16), x,
                                            preferred_element_type=F32).astype(BF16)
        gsorted_scr[buf, chunk, :] = jnp.broadcast_to(jnp.sum(gval, axis=1, keepdims=True), (MOE_CHUNK, HEAD))

    layout_refs = (seg_start_ref, seg_gran_ref, loc_off_ref)

    def copies(tile, b, start):
        streams = (
            (lambda loc, glob: sorted_scr.at[b, pl.ds(loc, MOE_GRAN)],
             lambda loc, glob: xs_ref.at[pl.ds(glob, MOE_GRAN)]),
            (lambda loc, glob: gsorted_scr.at[b, pl.ds(loc, MOE_GRAN)],
             lambda loc, glob: gs_ref.at[pl.ds(glob, MOE_GRAN)]),
        )
        for src_of, dst_of in streams:
            _segment_copies(layout_refs, tile, src_of, dst_of, buf_sem.at[b], start=start)

    copies(i, buf, True)

    @pl.when(i > 0)
    def _():
        copies(i - 1, 1 - buf, False)

    @pl.when(i == pl.num_programs(0) - 1)
    def _():
        copies(i, buf, False)


def _dispatch(layout, slot, gate, hn_a, hn_b, n_rows):
    tiles_a, tiles_b = hn_a.shape[0] // MOE_TM, hn_b.shape[0] // MOE_TM
    slot_t = jnp.swapaxes(slot, 1, 2)
    gate_t = jnp.swapaxes(gate, 1, 2)
    grid_spec = pltpu.PrefetchScalarGridSpec(
        num_scalar_prefetch=4,
        grid=(tiles_a + tiles_b,),
        in_specs=[
            pl.BlockSpec((None, TOP_K, MOE_TM), lambda i, *_: (i, 0, 0)),
            pl.BlockSpec((None, TOP_K, MOE_TM), lambda i, *_: (i, 0, 0)),
            pl.BlockSpec((MOE_TM, D_MODEL), lambda i, *_: (jnp.minimum(i, tiles_a - 1), 0)),
            pl.BlockSpec((MOE_TM, D_MODEL), lambda i, *_: (jnp.maximum(i - tiles_a, 0), 0)),
        ],
        out_specs=(pl.BlockSpec(memory_space=pl.ANY), pl.BlockSpec(memory_space=pl.ANY)),
        scratch_shapes=[
            pltpu.VMEM((2, MOE_SLOTS, D_MODEL), BF16),
            pltpu.VMEM((2, MOE_SLOTS, HEAD), F32),
            pltpu.VMEM((MOE_SUB, D_MODEL), BF16),
            pltpu.VMEM((MOE_SUB, HEAD), F32),
            pltpu.SemaphoreType.DMA,
            pltpu.SemaphoreType.DMA((2,)),
        ],
    )
    return pl.pallas_call(
        functools.partial(_dispatch_kernel, tiles_a=tiles_a),
        out_shape=(jax.ShapeDtypeStruct((n_rows, D_MODEL), BF16), jax.ShapeDtypeStruct((n_rows, HEAD), F32)),
        grid_spec=grid_spec,
        compiler_params=_cparams(("arbitrary",)),
        name="moe_dispatch",
    )(layout["seg_start"], layout["seg_gran"], layout["loc_off"], layout["tail_row"], slot_t, gate_t, hn_a, hn_b)


def _moe_kernel(blk_e_ref, blk_nsub_ref, n_used_ref, xs_ref, gate_ref, wg_ref, wl_ref, bg_ref, bl_ref,
                wd_ref, bd_ref, yb_ref, wu_s, wd_s, act_s, y_s, pend_ref, sem):
    i = pl.program_id(0)
    f = pl.program_id(1)
    last_step = jnp.logical_and(i == pl.num_programs(0) - 1, f == MOE_NF - 1)

    def y_copies(blk, sb):
        rows = pl.ds(pl.multiple_of(sb * MOE_SUB, MOE_SUB), MOE_SUB)
        dst = pl.ds(pl.multiple_of(blk * MOE_RB + sb * MOE_SUB, MOE_SUB), MOE_SUB)
        return [pltpu.make_async_copy(y_s.at[part, rows], yb_ref.at[part, dst], sem) for part in range(2)]

    def drain():
        blk = pend_ref[1]

        def body(sb, c):
            for copy in y_copies(blk, sb):
                copy.wait()
            return c

        lax.fori_loop(0, pend_ref[0], body, 0)
        pend_ref[0] = 0

    @pl.when(jnp.logical_and(i == 0, f == 0))
    def _():
        pend_ref[0] = 0
        pend_ref[1] = 0

    def over_granules(nsub, fn):
        def octet(sb, c):
            fn(sb * (8 * MOE_SUB), 4 * MOE_SUB)
            fn(sb * (8 * MOE_SUB) + 4 * MOE_SUB, 4 * MOE_SUB)
            return c

        lax.fori_loop(0, nsub // 8, octet, 0)
        rem8 = nsub % 8
        rem = nsub % 4
        done = nsub - rem

        @pl.when(rem8 >= 4)
        def _():
            fn((nsub - rem8) * MOE_SUB, 4 * MOE_SUB)

        @pl.when(rem >= 2)
        def _():
            fn(done * MOE_SUB, 2 * MOE_SUB)

        @pl.when(rem % 2 == 1)
        def _():
            fn((nsub - 1) * MOE_SUB, MOE_SUB)

    @pl.when(i < n_used_ref[0])
    def _():
        wu_s[:, :MOE_TF] = wg_ref[...].astype(BF16)
        wu_s[:, MOE_TF:] = wl_ref[...].astype(BF16)
        wd_s[pl.ds(pl.multiple_of(f * MOE_TF, MOE_TF), MOE_TF), :] = wd_ref[...].astype(BF16)
        b_up = jnp.concatenate([bg_ref[...], bl_ref[...]], axis=1)
        nsub = blk_nsub_ref[i]

        def up(start, size):
            rows = pl.ds(pl.multiple_of(start, MOE_SUB), size)
            hh = jnp.dot(xs_ref[rows, :], wu_s[...], preferred_element_type=F32) + b_up
            glu = jnp.minimum(hh[:, :MOE_TF], SWIGLU_LIMIT)
            lin = jnp.clip(hh[:, MOE_TF:], -SWIGLU_LIMIT, SWIGLU_LIMIT)
            act_s[f, rows, :] = (glu * _sigmoid(SWIGLU_ALPHA * glu) * (lin + 1.0)).astype(BF16)

        over_granules(nsub, up)

        @pl.when(f == MOE_NF - 1)
        def _():
            drain()

            def down(start, size):
                rows = pl.ds(pl.multiple_of(start, MOE_SUB), size)
                act = jnp.concatenate([act_s[k, rows, :] for k in range(MOE_NF)], axis=1)
                y = (jnp.dot(act, wd_s[...], preferred_element_type=F32) + bd_ref[...]) * gate_ref[rows, 0:1]
                hi = y.astype(BF16)
                y_s[0, rows, :] = hi
                y_s[1, rows, :] = (y - hi.astype(F32)).astype(BF16)

            over_granules(nsub, down)

            def send(sb, c):
                for copy in y_copies(i, sb):
                    copy.start()
                return c

            lax.fori_loop(0, nsub, send, 0)
            pend_ref[0] = nsub
            pend_ref[1] = i

    @pl.when(last_step)
    def _():
        drain()


def _moe_experts(blk_e, blk_nsub, n_used, xs, gate_rows, w_up, b_up, w_down, b_down):
    n_blk = xs.shape[0] // MOE_RB

    def live(i, n_used_ref):
        return jnp.minimum(i, n_used_ref[0] - 1)

    def ff(i, f, n_used_ref):
        return jnp.where(i < n_used_ref[0], f, MOE_NF - 1)

    def expert(i, be, nu):
        return be[live(i, nu)]

    grid_spec = pltpu.PrefetchScalarGridSpec(
        num_scalar_prefetch=3,
        grid=(n_blk, MOE_NF),
        in_specs=[
            pl.BlockSpec((MOE_RB, D_MODEL), lambda i, f, be, ns, nu: (live(i, nu), 0)),
            pl.BlockSpec((MOE_RB, HEAD), lambda i, f, be, ns, nu: (live(i, nu), 0)),
            pl.BlockSpec((None, D_MODEL, MOE_TF), lambda i, f, be, ns, nu: (expert(i, be, nu), 0, ff(i, f, nu))),
            pl.BlockSpec((None, D_MODEL, MOE_TF),
                         lambda i, f, be, ns, nu: (expert(i, be, nu), 0, ff(i, f, nu) + MOE_NF)),
            pl.BlockSpec((None, 1, MOE_TF), lambda i, f, be, ns, nu: (expert(i, be, nu), 0, ff(i, f, nu))),
            pl.BlockSpec((None, 1, MOE_TF),
                         lambda i, f, be, ns, nu: (expert(i, be, nu), 0, ff(i, f, nu) + MOE_NF)),
            pl.BlockSpec((None, MOE_TF, D_MODEL), lambda i, f, be, ns, nu: (expert(i, be, nu), ff(i, f, nu), 0)),
            pl.BlockSpec((None, 1, D_MODEL), lambda i, f, be, ns, nu: (expert(i, be, nu), 0, 0)),
        ],
        out_specs=pl.BlockSpec(memory_space=pl.ANY),
        scratch_shapes=[
            pltpu.VMEM((D_MODEL, 2 * MOE_TF), BF16),
            pltpu.VMEM((D_FF, D_MODEL), BF16),
            pltpu.VMEM((MOE_NF, MOE_RB, MOE_TF), BF16),
            pltpu.VMEM((2, MOE_RB, D_MODEL), BF16),
            pltpu.SMEM((2,), jnp.int32),
            pltpu.SemaphoreType.DMA,
        ],
    )
    b_up3 = b_up.reshape(N_EXPERTS, 1, 2 * D_FF)
    return pl.pallas_call(
        _moe_kernel,
        out_shape=jax.ShapeDtypeStruct((2,) + xs.shape, BF16),
        grid_spec=grid_spec,
        compiler_params=_cparams(("arbitrary", "arbitrary")),
        name="moe_experts",
    )(blk_e, blk_nsub, n_used, xs, gate_rows, w_up, w_up, b_up3, b_up3, w_down,
      b_down.reshape(N_EXPERTS, 1, D_MODEL))


def _combine_kernel(seg_start_ref, seg_gran_ref, loc_off_ref, slot_ref, h_ref, yb_ref, y_ref, rows_scr, sem, *,
                    tile0):
    i = pl.program_id(0)

    @pl.when(i == 0)
    def _():
        rows_scr[...] = jnp.zeros_like(rows_scr)

    tile = tile0 + i
    per_chunk = MOE_CHUNK // MOE_GRAN
    last = tile * N_EXPERTS + N_EXPERTS - 1
    n_gran = loc_off_ref[last] // MOE_GRAN + seg_gran_ref[last]
    for e in range(N_EXPERTS):
        idx = tile * N_EXPERTS + e
        loc0 = loc_off_ref[idx]
        glob0 = seg_start_ref[idx]

        def issue(c, carry, loc0=loc0, glob0=glob0):
            loc = pl.multiple_of(loc0 + c * MOE_GRAN, MOE_GRAN)
            glob = pl.multiple_of(glob0 + c * MOE_GRAN, MOE_GRAN)
            for part in range(2):
                pltpu.make_async_copy(yb_ref.at[part, pl.ds(glob, MOE_GRAN)],
                                      rows_scr.at[part, pl.ds(loc, MOE_GRAN)], sem.at[loc // MOE_CHUNK]).start()
            return carry

        lax.fori_loop(0, seg_gran_ref[idx], issue, 0)

    slot_vecs = [slot_ref[:, k:k + 1] for k in range(TOP_K)]
    n_chunks = MOE_SLOTS // MOE_CHUNK
    y_ref[...] = h_ref[...]
    for half in (range(0, n_chunks // 2), range(n_chunks // 2, n_chunks)):
        for c in half:
            def landed(g, carry, c=c):
                for part in range(2):
                    pltpu.make_async_copy(yb_ref.at[part, pl.ds(0, MOE_GRAN)],
                                          rows_scr.at[part, pl.ds(c * MOE_CHUNK, MOE_GRAN)], sem.at[c]).wait()
                return carry

            lax.fori_loop(0, jnp.clip(n_gran - c * per_chunk, 0, per_chunk), landed, 0)
        for c in half:
            pick = _slot_onehot(slot_vecs, c * MOE_CHUNK, (MOE_TM, MOE_CHUNK), 1)
            chunk = slice(c * MOE_CHUNK, (c + 1) * MOE_CHUNK)
            y_ref[...] += (jnp.dot(pick, rows_scr[0, chunk, :], preferred_element_type=F32)
                           + jnp.dot(pick, rows_scr[1, chunk, :], preferred_element_type=F32))


def _combine(layout, slot, h, yb, tile0):
    m = h.shape[0]
    grid_spec = pltpu.PrefetchScalarGridSpec(
        num_scalar_prefetch=3,
        grid=(m // MOE_TM,),
        in_specs=[
            pl.BlockSpec((None, MOE_TM, TOP_K), lambda i, *_: (i + tile0, 0, 0)),
            pl.BlockSpec((MOE_TM, D_MODEL), lambda i, *_: (i, 0)),
            pl.BlockSpec(memory_space=pl.ANY),
        ],
        out_specs=pl.BlockSpec((MOE_TM, D_MODEL), lambda i, *_: (i, 0)),
        scratch_shapes=[pltpu.VMEM((2, MOE_SLOTS, D_MODEL), BF16),
                        pltpu.SemaphoreType.DMA((MOE_SLOTS // MOE_CHUNK,))],
    )
    return pl.pallas_call(
        functools.partial(_combine_kernel, tile0=tile0),
        out_shape=jax.ShapeDtypeStruct((m, D_MODEL), F32),
        grid_spec=grid_spec,
        compiler_params=_cparams(("arbitrary",)),
        name="moe_combine",
    )(layout["seg_start"], layout["seg_gran"], layout["loc_off"], slot, h, yb)


def kernel(x_prompt, x_sample, cache_kv_w128, cache_kv_w512, cache_kv_w2048, state_hgrn, rel_bias, attn_norm,
           w_in, q_norm, k_norm, lb_raw, g_norm, w_pa, w_pb, w_o, ffn_norm, w_router, b_router, w_up, b_up,
           w_down, b_down):
    assert attn_norm.shape[0] == 1, "single-layer stack"
    bsz, seq, _ = x_prompt.shape
    nseq, t_new, _ = x_sample.shape
    n_p, n_s = bsz * seq, nseq * t_new
    lb = jax.nn.softmax(lb_raw.astype(F32), axis=0)[0]
    wpa, wpb, wo = w_pa[0].astype(BF16), w_pb[0].astype(BF16), w_o[0].astype(BF16)
    wr = jnp.pad(w_router[0].astype(F32), ((0, 0), (0, RT_LANES - N_EXPERTS)))
    wr_hi = wr.astype(BF16)
    wr_lo = (wr - wr_hi.astype(F32)).astype(BF16)
    br = jnp.pad(b_router[0].astype(F32), (0, RT_LANES - N_EXPERTS)).reshape(1, RT_LANES)
    caches = (cache_kv_w128[0], cache_kv_w512[0], cache_kv_w2048[0])

    xp = x_prompt.reshape(n_p, D_MODEL)
    xs = x_sample.reshape(n_s, D_MODEL)

    proj_p = _inproj(xp, attn_norm[0], w_in[0])
    qn_p, *kv_p, kvt_p = _qkv(proj_p, q_norm[0], k_norm[0])
    oa_p = _attn_prompt(qn_p, kv_p, jnp.stack([_prompt_bias(rel_bias, g) for g in range(3)]), bsz, seq)
    ob_p, st_p = _hgrn_prompt(proj_p, lb, g_norm[0], bsz, seq)
    mixed_p = _mix(oa_p, ob_p, proj_p, wpa, wpb)
    h_p, hn_p, lg_p = _resid(xp, mixed_p, wo, ffn_norm[0], wr_hi, wr_lo, br)

    proj_s = _inproj(xs, attn_norm[0], w_in[0])
    qn_s, *kv_s, _ = _qkv(proj_s, q_norm[0], k_norm[0])
    oa_s = _attn_sample(qn_s, kv_s, caches, rel_bias, t_new)
    ob_s, st_s = _hgrn_sample(proj_s, lb, g_norm[0], state_hgrn[0], t_new)
    mixed_s = _mix(oa_s, ob_s, proj_s, wpa, wpb)
    h_s, hn_s, lg_s = _resid(xs, mixed_s, wo, ffn_norm[0], wr_hi, wr_lo, br)

    logits = jnp.concatenate([lg_p, lg_s], axis=0)[:, :N_EXPERTS]
    assert n_p % MOE_TM == 0 and n_s % MOE_TM == 0, "each pass must fill whole MoE token tiles"
    blk_e, blk_nsub, n_used, layout, slot, gate, n_blk = _route(logits)
    xs_sorted, gate_rows = _dispatch(layout, slot, gate, hn_p, hn_s, n_blk * MOE_RB)
    yb = _moe_experts(blk_e, blk_nsub, n_used, xs_sorted, gate_rows, w_up[0], b_up[0], w_down[0], b_down[0])
    y_p = _combine(layout, slot, h_p, yb, 0)
    y_s = _combine(layout, slot, h_s, yb, n_p // MOE_TM)

    def prompt_rows(kv, win):
        keep = min(win, seq)
        rows = kv.reshape(bsz, seq, 2 * A_OUT)[:, seq - keep:]
        return rows.reshape(1, bsz, keep, 2, A_HPG, HEAD)

    kvp = [prompt_rows(kv_p[g], A_GROUPS[g][0]) for g in range(2)]
    wide = min(A_GROUPS[2][0], seq)
    kvp.append(kvt_p.reshape(bsz, seq, 2, A_HPG, HEAD)[None, :, seq - wide:])
    kvs = [kv_s[g].reshape(1, nseq, t_new, 2, A_HPG, HEAD) for g in range(3)]
    return (y_p.reshape(bsz, seq, D_MODEL), y_s.reshape(nseq, t_new, D_MODEL),
            kvp[0], kvp[1], kvp[2], st_p[None],
            kvs[0], kvs[1], kvs[2], st_s[None])
```

```python
import functools

import numpy as np
import jax
import jax.numpy as jnp
from jax import lax
from jax.experimental import pallas as pl
from jax.experimental.pallas import tpu as pltpu

F32 = jnp.float32
BF16 = jnp.bfloat16

D_MODEL = 2048
HEAD = 128
A_GROUPS = ((128, 1), (512, 4), (2048, 16))
A_HPG = 4
A_KEYS = 128
A_WIDTH = 3 * A_HPG * HEAD
A_OUT = A_HPG * HEAD
B_HEADS = 8
B_WIDTH = B_HEADS * HEAD
B_CHUNK = 32
REL_BUCKETS = 32
REL_MAX_DIST = 2048
N_EXPERTS = 32
TOP_K = 4
D_FF = 2048
SWIGLU_ALPHA = 1.702
SWIGLU_LIMIT = 7.0
NORM_EPS = 1e-6
NEG_INF = -1e30
ATTN_SCALE = HEAD ** -0.5
IN_WIDTH = 3 * A_WIDTH + 4 * B_WIDTH + 2 * D_MODEL
OFF_Q, OFF_K, OFF_V = 0, A_WIDTH, 2 * A_WIDTH
OFF_QB = 3 * A_WIDTH
OFF_FB, OFF_IB, OFF_OB = OFF_QB + B_WIDTH, OFF_QB + 2 * B_WIDTH, OFF_QB + 3 * B_WIDTH
OFF_GA = OFF_QB + 4 * B_WIDTH
OFF_GB = OFF_GA + D_MODEL

V7X_VMEM_LIMIT = 56 * 1024 * 1024
SUBLANES = 8

NT_DIMS = (((1,), (1,)), ((), ()))


def _sigmoid(x):
    return 1.0 / (1.0 + jnp.exp(-x))


def _cparams(sem):
    return pltpu.CompilerParams(dimension_semantics=sem, vmem_limit_bytes=V7X_VMEM_LIMIT)


NORM_TM = 512
PROJ_TM = 1024
PROJ_TN = 1280


def _rmsnorm_kernel(x_ref, nw_ref, o_ref):
    x = x_ref[...]
    ms = jnp.mean(x * x, axis=-1, keepdims=True)
    o_ref[...] = (x * lax.rsqrt(ms + NORM_EPS) * nw_ref[...]).astype(BF16)


def _inproj_kernel(xn_ref, w_ref, o_ref, wbf_ref):
    @pl.when(pl.program_id(1) == 0)
    def _():
        wbf_ref[...] = w_ref[...].astype(BF16)

    o_ref[...] = jnp.dot(xn_ref[...], wbf_ref[...], preferred_element_type=F32)


def _inproj(x, norm_w, w_in):
    m = x.shape[0]
    xn = pl.pallas_call(
        _rmsnorm_kernel,
        out_shape=jax.ShapeDtypeStruct((m, D_MODEL), BF16),
        grid=(m // NORM_TM,),
        in_specs=[pl.BlockSpec((NORM_TM, D_MODEL), lambda i: (i, 0)), pl.BlockSpec((1, D_MODEL), lambda i: (0, 0))],
        out_specs=pl.BlockSpec((NORM_TM, D_MODEL), lambda i: (i, 0)),
        compiler_params=_cparams(("arbitrary",)),
        name="attn_rmsnorm",
    )(x, norm_w.reshape(1, D_MODEL))
    tm = min(PROJ_TM, m)
    return pl.pallas_call(
        _inproj_kernel,
        out_shape=jax.ShapeDtypeStruct((m, IN_WIDTH), F32),
        grid=(IN_WIDTH // PROJ_TN, m // tm),
        in_specs=[
            pl.BlockSpec((tm, D_MODEL), lambda j, i: (i, 0)),
            pl.BlockSpec((D_MODEL, PROJ_TN), lambda j, i: (0, j)),
        ],
        out_specs=pl.BlockSpec((tm, PROJ_TN), lambda j, i: (i, j)),
        scratch_shapes=[pltpu.VMEM((D_MODEL, PROJ_TN), BF16)],
        compiler_params=_cparams(("arbitrary", "arbitrary")),
        name="inproj",
    )(xn, w_in)


QKV_TM = 512


def _qkv_kernel(q_ref, k_ref, v_ref, qw_ref, kw_ref, qn_ref, kv0_ref, kv1_ref, kv2_ref, kvt_ref):
    kv_refs = (kv0_ref, kv1_ref, kv2_ref)
    qw = qw_ref[...]
    kw = kw_ref[...]
    for h in range(3 * A_HPG):
        sl = slice(h * HEAD, (h + 1) * HEAD)
        q = q_ref[:, sl]
        qn_ref[:, sl] = q * lax.rsqrt(jnp.mean(q * q, axis=-1, keepdims=True) + NORM_EPS) * qw
        k = k_ref[:, sl]
        kn = k * lax.rsqrt(jnp.mean(k * k, axis=-1, keepdims=True) + NORM_EPS) * kw
        g, hh = divmod(h, A_HPG)
        kv_refs[g][:, hh * HEAD:(hh + 1) * HEAD] = kn
        kv_refs[g][:, A_OUT + hh * HEAD:A_OUT + (hh + 1) * HEAD] = v_ref[:, sl]
        if g == 2:
            kvt_ref[:, hh, :] = kn
            kvt_ref[:, A_HPG + hh, :] = v_ref[:, sl]


def _qkv(proj, q_norm, k_norm):
    m = proj.shape[0]
    kv_shape = jax.ShapeDtypeStruct((m, 2 * A_OUT), F32)
    return pl.pallas_call(
        _qkv_kernel,
        out_shape=(jax.ShapeDtypeStruct((m, A_WIDTH), F32), kv_shape, kv_shape, kv_shape,
                   jax.ShapeDtypeStruct((m, 2 * A_HPG, HEAD), F32)),
        grid=(m // QKV_TM,),
        in_specs=[
            pl.BlockSpec((QKV_TM, A_WIDTH), lambda i: (i, OFF_Q // A_WIDTH)),
            pl.BlockSpec((QKV_TM, A_WIDTH), lambda i: (i, OFF_K // A_WIDTH)),
            pl.BlockSpec((QKV_TM, A_WIDTH), lambda i: (i, OFF_V // A_WIDTH)),
            pl.BlockSpec((1, HEAD), lambda i: (0, 0)),
            pl.BlockSpec((1, HEAD), lambda i: (0, 0)),
        ],
        out_specs=(
            pl.BlockSpec((QKV_TM, A_WIDTH), lambda i: (i, 0)),
            pl.BlockSpec((QKV_TM, 2 * A_OUT), lambda i: (i, 0)),
            pl.BlockSpec((QKV_TM, 2 * A_OUT), lambda i: (i, 0)),
            pl.BlockSpec((QKV_TM, 2 * A_OUT), lambda i: (i, 0)),
            pl.BlockSpec((QKV_TM, 2 * A_HPG, HEAD), lambda i: (i, 0, 0)),
        ),
        compiler_params=_cparams(("arbitrary",)),
        name="qkv_norm",
    )(proj, proj, proj, q_norm.reshape(1, HEAD), k_norm.reshape(1, HEAD))


def _t5_bucket(dist):
    dist = np.asarray(dist, np.int64)
    max_exact = REL_BUCKETS // 2
    ratio = np.log(np.maximum(dist, 1) / max_exact) / np.log(REL_MAX_DIST / max_exact)
    large = np.minimum(max_exact + (ratio * (REL_BUCKETS - max_exact)).astype(np.int64), REL_BUCKETS - 1)
    return np.where(dist < max_exact, dist, large).astype(np.int32)


def _bias_lookup(tab, bucket, valid):
    bucket = jnp.asarray(np.where(valid, bucket, -1).astype(np.int32))[None]
    out = jnp.full((tab.shape[1],) + bucket.shape[1:], NEG_INF, F32)
    for b in range(REL_BUCKETS):
        out = jnp.where(bucket == b, tab[b].reshape((-1,) + (1,) * (bucket.ndim - 1)), out)
    return out


def _prompt_bias(rel_bias, g):
    _, dil = A_GROUPS[g]
    n = A_KEYS
    j = n + np.arange(n)[:, None] - np.arange(2 * n)[None, :]
    valid = (j >= 0) & (j < n)
    tab = rel_bias[:, g * A_HPG:(g + 1) * A_HPG].astype(F32)
    return _bias_lookup(tab, _t5_bucket(np.clip(j, 0, n - 1) * dil), valid)


ATT_MIX_ROWS = 256


def _attn_prompt_kernel(q0_ref, q1_ref, q2_ref, k0_ref, v0_ref, k1_ref, v1_ref, k2_ref, v2_ref, bias_ref,
                        o_ref, og_scr, lg_scr, *, seq):
    n = A_KEYS
    h = pl.program_id(1)
    q_refs = (q0_ref, q1_ref, q2_ref)
    k_refs = (k0_ref, k1_ref, k2_ref)
    v_refs = (v0_ref, v1_ref, v2_ref)
    for g, (win, dil) in enumerate(A_GROUPS):
        bias = bias_ref[g, h]
        for r in range(dil):
            for i in range(seq // win):
                def rows(blk, r=r, dil=dil):
                    if dil == 1:
                        return pl.ds(blk * n, n)
                    return pl.ds(r + blk * n * dil, n, stride=dil)

                q = q_refs[g][rows(i), :].astype(BF16)
                if i == 0:
                    kk = k_refs[g][rows(i), :].astype(BF16)
                    vv = v_refs[g][rows(i), :].astype(BF16)
                    b_tile = bias[:, n:]
                else:
                    kk = jnp.concatenate([k_refs[g][rows(i - 1), :], k_refs[g][rows(i), :]], axis=0).astype(BF16)
                    vv = jnp.concatenate([v_refs[g][rows(i - 1), :], v_refs[g][rows(i), :]], axis=0).astype(BF16)
                    b_tile = bias
                s = lax.dot_general(q, kk, NT_DIMS, preferred_element_type=F32) * ATTN_SCALE + b_tile
                mx = jnp.max(s, axis=-1, keepdims=True)
                p = jnp.exp(s - mx)
                den = jnp.sum(p, axis=-1, keepdims=True)
                og_scr[g, rows(i), :] = jnp.dot(p.astype(BF16), vv, preferred_element_type=F32) / den
                lg_scr[g, rows(i), :] = jnp.broadcast_to(mx + jnp.log(den), (n, HEAD))
    for c in range(seq // ATT_MIX_ROWS):
        sl = slice(c * ATT_MIX_ROWS, (c + 1) * ATT_MIX_ROWS)
        la, lb_, lc = lg_scr[0, sl, :], lg_scr[1, sl, :], lg_scr[2, sl, :]
        top = jnp.maximum(jnp.maximum(la, lb_), lc)
        ea, eb, ec = jnp.exp(la - top), jnp.exp(lb_ - top), jnp.exp(lc - top)
        o_ref[sl, :] = (ea * og_scr[0, sl, :] + eb * og_scr[1, sl, :] + ec * og_scr[2, sl, :]) / (ea + eb + ec)


def _attn_prompt(qn, kvs, bias, bsz, seq):
    m = qn.shape[0]
    col = lambda fn: pl.BlockSpec((seq, HEAD), lambda b, h: (b, fn(h)))
    in_specs = [col(lambda h, g=g: g * A_HPG + h) for g in range(3)]
    for _ in range(3):
        in_specs += [col(lambda h: h), col(lambda h: A_HPG + h)]
    in_specs.append(pl.BlockSpec(bias.shape, lambda b, h: (0, 0, 0, 0)))
    return pl.pallas_call(
        functools.partial(_attn_prompt_kernel, seq=seq),
        out_shape=jax.ShapeDtypeStruct((m, A_OUT), F32),
        grid=(bsz, A_HPG),
        in_specs=in_specs,
        out_specs=col(lambda h: h),
        scratch_shapes=[pltpu.VMEM((3, seq, HEAD), F32), pltpu.VMEM((3, seq, HEAD), F32)],
        compiler_params=_cparams(("arbitrary", "arbitrary")),
        name="attn_prompt",
    )(qn, qn, qn, kvs[0], kvs[0], kvs[1], kvs[1], kvs[2], kvs[2], bias)


def _hgrn_gates(q, z, lb):
    qs = q * _sigmoid(q)
    logf = jnp.log(lb + (1.0 - lb) * _sigmoid(z))
    kb = (1.0 - lb) * _sigmoid(-z)
    return qs, kb, logf


def _hgrn_out(o, gate, gn):
    on = o * lax.rsqrt(jnp.mean(o * o, axis=-1, keepdims=True) + NORM_EPS) * gn
    return (on * (gate * _sigmoid(gate))).astype(BF16)


def _split3(x):
    hi = x.astype(BF16)
    r1 = x - hi.astype(F32)
    mid = r1.astype(BF16)
    lo = (r1 - mid.astype(F32)).astype(BF16)
    return hi, mid, lo


HG_TC = 256
HG_NC = HG_TC // B_CHUNK
HG_HPS = 4


def _hgrn_prompt_kernel(q_ref, f_ref, i_ref, g_ref, lb_ref, gn_ref, ob_ref, st_ref, s_scr):
    tb = pl.program_id(2)

    @pl.when(tb == 0)
    def _():
        s_scr[...] = jnp.zeros_like(s_scr)

    row = lax.broadcasted_iota(jnp.int32, (HG_TC, HG_TC), 0)
    col = lax.broadcasted_iota(jnp.int32, (HG_TC, HG_TC), 1)
    same_chunk = (row // B_CHUNK) == (col // B_CHUNK)
    causal = jnp.logical_and(same_chunk, col <= row)
    cum_mat = jnp.where(causal, 1.0, 0.0).astype(BF16)
    tcol = lax.broadcasted_iota(jnp.int32, (HEAD, HG_TC), 1) // B_CHUNK
    trow = lax.broadcasted_iota(jnp.int32, (HG_TC, HEAD), 0) // B_CHUNK
    gn = gn_ref[...]
    for h in range(HG_HPS):
        sl = slice(h * HEAD, (h + 1) * HEAD)
        qs, kb, logf = _hgrn_gates(q_ref[:, sl], f_ref[:, sl], lb_ref[:, sl])
        b3 = jnp.dot(cum_mat, jnp.concatenate(_split3(logf), axis=1), preferred_element_type=F32)
        b = b3[:, :HEAD] + b3[:, HEAD:2 * HEAD] + b3[:, 2 * HEAD:]
        bl_rows = [b[c * B_CHUNK + B_CHUNK - 1:(c + 1) * B_CHUNK, :] for c in range(HG_NC)]
        bl = jnp.concatenate([jnp.broadcast_to(r, (B_CHUNK, HEAD)) for r in bl_rows], axis=0)
        q_dec_f = qs * jnp.exp(b)
        q_dec = q_dec_f.astype(BF16)
        k_inv = (kb * jnp.exp(-b)).astype(BF16)
        k_end_t = jnp.transpose(kb * jnp.exp(bl - b))
        v = i_ref[:, sl].astype(BF16)
        att = lax.dot_general(q_dec, k_inv, NT_DIMS, preferred_element_type=F32)
        att = jnp.where(causal, att, 0.0).astype(BF16)
        o_intra = jnp.dot(att, v, preferred_element_type=F32)
        decay_t = jnp.exp(jnp.transpose(bl))
        ke_stack = jnp.concatenate([jnp.where(tcol == c, k_end_t, 0.0) for c in range(HG_NC)], axis=0)
        ds_all = jnp.dot(ke_stack.astype(BF16), v, preferred_element_type=F32)
        s = s_scr[h]
        starts = []
        for c in range(HG_NC):
            starts.append(s)
            s = decay_t[:, c * B_CHUNK:c * B_CHUNK + 1] * s + ds_all[c * HEAD:(c + 1) * HEAD, :]
        s_scr[h] = s
        q_bd = jnp.concatenate([jnp.where(trow == c, q_dec_f, 0.0) for c in range(HG_NC)], axis=1).astype(BF16)
        o_inter = jnp.dot(q_bd, jnp.concatenate(starts, axis=0).astype(BF16), preferred_element_type=F32)
        ob_ref[:, sl] = _hgrn_out(o_intra + o_inter, g_ref[:, sl], gn)

    @pl.when(tb == pl.num_programs(2) - 1)
    def _():
        st_ref[...] = s_scr[...]


def _hgrn_prompt(proj, lb, g_norm, bsz, seq):
    m = proj.shape[0]
    hw = HG_HPS * HEAD
    nt = seq // HG_TC
    rowblk = lambda b, hf, t: b * nt + t
    spec = lambda off: pl.BlockSpec((HG_TC, hw), lambda b, hf, t: (rowblk(b, hf, t), off // hw + hf))
    return pl.pallas_call(
        _hgrn_prompt_kernel,
        out_shape=(jax.ShapeDtypeStruct((m, B_WIDTH), BF16),
                   jax.ShapeDtypeStruct((bsz, B_HEADS, HEAD, HEAD), F32)),
        grid=(bsz, B_HEADS // HG_HPS, nt),
        in_specs=[
            spec(OFF_QB), spec(OFF_FB), spec(OFF_IB), spec(OFF_OB),
            pl.BlockSpec((1, hw), lambda b, hf, t: (0, hf)),
            pl.BlockSpec((1, HEAD), lambda b, hf, t: (0, 0)),
        ],
        out_specs=(
            pl.BlockSpec((HG_TC, hw), lambda b, hf, t: (rowblk(b, hf, t), hf)),
            pl.BlockSpec((None, HG_HPS, HEAD, HEAD), lambda b, hf, t: (b, hf, 0, 0)),
        ),
        scratch_shapes=[pltpu.VMEM((HG_HPS, HEAD, HEAD), F32)],
        compiler_params=_cparams(("arbitrary", "arbitrary", "arbitrary")),
        name="hgrn_prompt",
    )(proj, proj, proj, proj, lb.reshape(1, B_WIDTH), g_norm.reshape(1, HEAD))


HS_SEQ = 32


def _hgrn_sample_kernel(q_ref, f_ref, i_ref, g_ref, lb_ref, gn_ref, s0_ref, ob_ref, st_ref, *, t_new):
    rows = HS_SEQ * t_new
    qs, kb, logf = _hgrn_gates(q_ref[...], f_ref[...], lb_ref[...])
    t_of = lax.broadcasted_iota(jnp.int32, (rows, HEAD), 0) % t_new
    b = logf
    sh = 1
    while sh < t_new:
        b = b + jnp.where(t_of >= sh, pltpu.roll(b, sh, axis=0), 0.0)
        sh *= 2
    bl = b
    for back in range(1, t_new):
        bl = jnp.where(t_of == t_new - 1 - back, pltpu.roll(b, rows - back, axis=0), bl)
    q_dec = (qs * jnp.exp(b)).astype(BF16)
    k_inv = (kb * jnp.exp(-b)).astype(BF16)
    k_end_t = jnp.transpose(kb * jnp.exp(bl - b))
    decay_t = jnp.exp(jnp.transpose(bl))
    v = i_ref[...].astype(BF16)
    row = lax.broadcasted_iota(jnp.int32, (rows, rows), 0)
    col = lax.broadcasted_iota(jnp.int32, (rows, rows), 1)
    causal = jnp.logical_and(row // t_new == col // t_new, col <= row)
    att = lax.dot_general(q_dec, k_inv, NT_DIMS, preferred_element_type=F32)
    att = jnp.where(causal, att, 0.0).astype(BF16)
    o_intra = jnp.dot(att, v, preferred_element_type=F32)
    seq_of_col = lax.broadcasted_iota(jnp.int32, (HEAD, rows), 1) // t_new
    seq_of_row = lax.broadcasted_iota(jnp.int32, (rows, HEAD), 0) // t_new
    o_inter = jnp.zeros((rows, HEAD), F32)
    for s_i in range(HS_SEQ):
        s = s0_ref[s_i, 0]
        part = jnp.dot(q_dec, s.astype(BF16), preferred_element_type=F32)
        o_inter = jnp.where(seq_of_row == s_i, part, o_inter)
        ke = jnp.where(seq_of_col == s_i, k_end_t, 0.0).astype(BF16)
        ds = jnp.dot(ke, v, preferred_element_type=F32)
        st_ref[s_i, 0] = decay_t[:, s_i * t_new:s_i * t_new + 1] * s + ds
    ob_ref[...] = _hgrn_out(o_intra + o_inter, g_ref[...], gn_ref[...])


def _hgrn_sample(proj, lb, g_norm, s0, t_new):
    m = proj.shape[0]
    nseq = m // t_new
    rows = HS_SEQ * t_new
    spec = lambda off: pl.BlockSpec((rows, HEAD), lambda i, h: (i, off // HEAD + h))
    st_spec = pl.BlockSpec((HS_SEQ, 1, HEAD, HEAD), lambda i, h: (i, h, 0, 0))
    return pl.pallas_call(
        functools.partial(_hgrn_sample_kernel, t_new=t_new),
        out_shape=(jax.ShapeDtypeStruct((m, B_WIDTH), BF16),
                   jax.ShapeDtypeStruct((nseq, B_HEADS, HEAD, HEAD), F32)),
        grid=(nseq // HS_SEQ, B_HEADS),
        in_specs=[
            spec(OFF_QB), spec(OFF_FB), spec(OFF_IB), spec(OFF_OB),
            pl.BlockSpec((1, HEAD), lambda i, h: (0, h)),
            pl.BlockSpec((1, HEAD), lambda i, h: (0, 0)),
            st_spec,
        ],
        out_specs=(pl.BlockSpec((rows, HEAD), lambda i, h: (i, h)), st_spec),
        compiler_params=_cparams(("arbitrary", "arbitrary")),
        name="hgrn_sample",
    )(proj, proj, proj, proj, lb.reshape(1, B_WIDTH), g_norm.reshape(1, HEAD), s0)


SA_SEQ = 2
KV_ROWS = 2 * A_HPG


def _sample_bias(rel_bias, t_new):
    def tile(tab, dist, valid):
        bias = _bias_lookup(tab, _t5_bucket(np.clip(dist, 0, REL_MAX_DIST)), valid)
        bias = jnp.moveaxis(bias, 0, -1)
        bias = jnp.concatenate([bias, jnp.zeros(bias.shape[:-1] + (KV_ROWS - A_HPG,), F32)], axis=-1)
        return jnp.broadcast_to(bias[..., None], bias.shape + (HEAD,))

    t = np.arange(t_new)[:, None]
    r = np.arange(A_KEYS + t_new)[None, :]
    is_new = r >= A_KEYS
    dist = np.where(is_new, t - (r - A_KEYS), A_KEYS + t - r)
    valid = (dist >= 0) & (dist < A_KEYS)
    b0 = tile(rel_bias[:, 0:A_HPG].astype(F32), dist, valid)
    b12 = []
    for g in (1, 2):
        _, dil = A_GROUPS[g]
        m = np.arange(A_KEYS + 1)
        dist = dil * (A_KEYS - m)
        b12.append(tile(rel_bias[:, g * A_HPG:(g + 1) * A_HPG].astype(F32), dist, m >= 1))
    return b0, jnp.stack(b12)


def _attn_sample_kernel(q_ref, n0_ref, n1_ref, n2_ref, c0_ref, c1_ref, c2_ref, b0_ref, b12_ref,
                        o_ref, qt_scr, ns_scr, *, t_new):
    nq = SA_SEQ * t_new
    new_refs = (n0_ref, n1_ref, n2_ref)
    cache_refs = (c0_ref, c1_ref, c2_ref)
    qt_scr[...] = jnp.zeros_like(qt_scr)
    for g in range(3):
        for j in range(nq):
            for h in range(A_HPG):
                qt_scr[g, j, h:h + 1, :] = (
                    q_ref[j:j + 1, g * A_OUT + h * HEAD:g * A_OUT + (h + 1) * HEAD] * ATTN_SCALE)
                ns_scr[g, j, h:h + 1, :] = new_refs[g][j:j + 1, h * HEAD:(h + 1) * HEAD]
                ns_scr[g, j, A_HPG + h:A_HPG + h + 1, :] = (
                    new_refs[g][j:j + 1, A_OUT + h * HEAD:A_OUT + (h + 1) * HEAD])

    row_id = lax.broadcasted_iota(jnp.int32, (nq, HEAD), 0)
    ones = jnp.ones((HEAD, HEAD), BF16)

    def per_query(j, out):
        s = j // t_new
        t = j - s * t_new
        row0 = s * A_KEYS
        stats = []
        for g in range(3):
            c_ref = cache_refs[g]
            qt = qt_scr[g, j]
            if g == 0:
                sub0 = 0
                bias_c = b0_ref[t, 0:A_KEYS]
                new_tiles = [(b0_ref[t, A_KEYS + tp], ns_scr[0, s * t_new + tp]) for tp in range(t_new)]
            else:
                sub0 = pl.multiple_of(t * KV_ROWS, KV_ROWS)
                bias_c = b12_ref[g - 1, 0:A_KEYS]
                new_tiles = [(b12_ref[g - 1, A_KEYS], ns_scr[g, j])]

            def mirror(l):
                return l + pltpu.roll(l, A_HPG, axis=l.ndim - 2)

            tiles = c_ref[pl.ds(row0, A_KEYS), pl.ds(sub0, KV_ROWS), :]
            prod = (tiles * qt[None]).reshape(A_KEYS * KV_ROWS, HEAD).astype(BF16)
            sums = jnp.dot(prod, ones, preferred_element_type=F32).reshape(A_KEYS, KV_ROWS, HEAD)
            l_c = mirror(sums + bias_c)
            new_l = [mirror(jnp.sum(tile_kv * qt, axis=-1, keepdims=True) + bias) for bias, tile_kv in new_tiles]
            mx = jnp.max(l_c, axis=0)
            for l in new_l:
                mx = jnp.maximum(mx, l)
            p_c = jnp.exp(l_c - mx[None])
            den = jnp.sum(p_c, axis=0)
            acc = jnp.sum(p_c * tiles, axis=0)
            for l, (_, tile_kv) in zip(new_l, new_tiles):
                p = jnp.exp(l - mx)
                den = den + p
                acc = acc + p * tile_kv
            stats.append((mx + jnp.log(den), den, acc))
        top = jnp.maximum(jnp.maximum(stats[0][0], stats[1][0]), stats[2][0])
        e = [jnp.exp(st[0] - top) for st in stats]
        tot = e[0] + e[1] + e[2]
        mix = jnp.zeros((KV_ROWS, HEAD), F32)
        for g in range(3):
            mix = mix + e[g] / (tot * stats[g][1]) * stats[g][2]
        pieces = [jnp.broadcast_to(mix[A_HPG + h:A_HPG + h + 1, :], (nq, HEAD)) for h in range(A_HPG)]
        return tuple(jnp.where(row_id == j, pieces[h], out[h]) for h in range(A_HPG))

    init = tuple(jnp.zeros((nq, HEAD), F32) for _ in range(A_HPG))
    out = lax.fori_loop(0, nq, per_query, init)
    for h in range(A_HPG):
        o_ref[:, h * HEAD:(h + 1) * HEAD] = out[h]


def _attn_sample(qn, kv_new, caches, rel_bias, t_new):
    m = qn.shape[0]
    nseq = m // t_new
    nq = SA_SEQ * t_new
    assert nq % SUBLANES == 0, "the new-token rows of one grid step must fill whole sublane tiles"
    kvw = 2 * A_OUT
    views, specs = [], []
    for g, (win, dil) in enumerate(A_GROUPS):
        c = caches[g]
        assert c.shape[1] == win, "cache must hold exactly one window of positions"
        assert g == 0 or dil >= t_new, "each new token must sit in its own residue class"
        views.append(c.reshape(nseq * A_KEYS, dil * KV_ROWS, HEAD))
        specs.append(pl.BlockSpec((SA_SEQ * A_KEYS, min(dil, t_new) * KV_ROWS, HEAD), lambda i: (i, 0, 0)))
    b0, b12 = _sample_bias(rel_bias, t_new)
    new_spec = pl.BlockSpec((nq, kvw), lambda i: (i, 0))
    return pl.pallas_call(
        functools.partial(_attn_sample_kernel, t_new=t_new),
        out_shape=jax.ShapeDtypeStruct((m, A_OUT), F32),
        grid=(nseq // SA_SEQ,),
        in_specs=[
            pl.BlockSpec((nq, A_WIDTH), lambda i: (i, 0)),
            new_spec, new_spec, new_spec,
            specs[0], specs[1], specs[2],
            pl.BlockSpec(b0.shape, lambda i: (0, 0, 0, 0)),
            pl.BlockSpec(b12.shape, lambda i: (0, 0, 0, 0)),
        ],
        out_specs=pl.BlockSpec((nq, A_OUT), lambda i: (i, 0)),
        scratch_shapes=[
            pltpu.VMEM((3, nq, KV_ROWS, HEAD), F32),
            pltpu.VMEM((3, nq, KV_ROWS, HEAD), F32),
        ],
        compiler_params=_cparams(("arbitrary",)),
        name="attn_sample",
    )(qn, kv_new[0], kv_new[1], kv_new[2], views[0], views[1], views[2], b0, b12)


MIX_TM = 512
MIX_TN = 512


MIX_NJ = D_MODEL // MIX_TN


def _mix_kernel(oa_ref, ob_ref, *refs):
    ga_refs, gb_refs = refs[:MIX_NJ], refs[MIX_NJ:2 * MIX_NJ]
    wpa_ref, wpb_ref, out_ref = refs[2 * MIX_NJ:]
    oa = oa_ref[...].astype(BF16)
    ob = ob_ref[...]
    for j in range(MIX_NJ):
        cols = slice(j * MIX_TN, (j + 1) * MIX_TN)
        pa = jnp.dot(oa, wpa_ref[:, cols], preferred_element_type=F32)
        pb = jnp.dot(ob, wpb_ref[:, cols], preferred_element_type=F32)
        out_ref[:, cols] = (_sigmoid(ga_refs[j][...]) * pa + _sigmoid(gb_refs[j][...]) * pb).astype(BF16)


def _mix(oa, ob, proj, w_pa, w_pb):
    m = proj.shape[0]
    gate_specs = [pl.BlockSpec((MIX_TM, MIX_TN), lambda i, off=off, j=j: (i, off // MIX_TN + j))
                  for off in (OFF_GA, OFF_GB) for j in range(MIX_NJ)]
    return pl.pallas_call(
        _mix_kernel,
        out_shape=jax.ShapeDtypeStruct((m, D_MODEL), BF16),
        grid=(m // MIX_TM,),
        in_specs=[
            pl.BlockSpec((MIX_TM, A_OUT), lambda i: (i, 0)),
            pl.BlockSpec((MIX_TM, B_WIDTH), lambda i: (i, 0)),
            *gate_specs,
            pl.BlockSpec((A_OUT, D_MODEL), lambda i: (0, 0)),
            pl.BlockSpec((B_WIDTH, D_MODEL), lambda i: (0, 0)),
        ],
        out_specs=pl.BlockSpec((MIX_TM, D_MODEL), lambda i: (i, 0)),
        compiler_params=_cparams(("arbitrary",)),
        name="mix",
    )(oa, ob, *([proj] * (2 * MIX_NJ)), w_pa, w_pb)


RT_LANES = 128


def _resid_kernel(x_ref, mixed_ref, wo_ref, nw_ref, wr_hi_ref, wr_lo_ref, br_ref, h_ref, hn_ref, lg_ref):
    h = x_ref[...] + jnp.dot(mixed_ref[...], wo_ref[...], preferred_element_type=F32)
    h_ref[...] = h
    hn = h * lax.rsqrt(jnp.mean(h * h, axis=-1, keepdims=True) + NORM_EPS) * nw_ref[...]
    hi = hn.astype(BF16)
    hn_ref[...] = hi
    lo = (hn - hi.astype(F32)).astype(BF16)
    lg = (jnp.dot(hi, wr_hi_ref[...], preferred_element_type=F32)
          + jnp.dot(lo, wr_hi_ref[...], preferred_element_type=F32)
          + jnp.dot(hi, wr_lo_ref[...], preferred_element_type=F32))
    lg_ref[...] = lg + br_ref[...]


def _resid(x, mixed, w_o, ffn_norm, wr_hi, wr_lo, br):
    m = x.shape[0]
    full = lambda shape: pl.BlockSpec(shape, lambda i: (0, 0))
    return pl.pallas_call(
        _resid_kernel,
        out_shape=(jax.ShapeDtypeStruct((m, D_MODEL), F32),
                   jax.ShapeDtypeStruct((m, D_MODEL), BF16),
                   jax.ShapeDtypeStruct((m, RT_LANES), F32)),
        grid=(m // MIX_TM,),
        in_specs=[
            pl.BlockSpec((MIX_TM, D_MODEL), lambda i: (i, 0)),
            pl.BlockSpec((MIX_TM, D_MODEL), lambda i: (i, 0)),
            full((D_MODEL, D_MODEL)), full((1, D_MODEL)),
            full((D_MODEL, RT_LANES)), full((D_MODEL, RT_LANES)), full((1, RT_LANES)),
        ],
        out_specs=(
            pl.BlockSpec((MIX_TM, D_MODEL), lambda i: (i, 0)),
            pl.BlockSpec((MIX_TM, D_MODEL), lambda i: (i, 0)),
            pl.BlockSpec((MIX_TM, RT_LANES), lambda i: (i, 0)),
        ),
        compiler_params=_cparams(("arbitrary",)),
        name="resid_router",
    )(x, mixed, w_o, ffn_norm.reshape(1, D_MODEL), wr_hi, wr_lo, br)


MOE_RB = 1280
MOE_SUB = 128
MOE_TF = 256
MOE_NF = D_FF // MOE_TF


MOE_TM = 512
MOE_GRAN = 16
MOE_SLOTS = 2560
MOE_CHUNK = 256
assert MOE_SLOTS >= MOE_TM * TOP_K + N_EXPERTS * (MOE_GRAN - 1) and MOE_SLOTS % MOE_CHUNK == 0


def _route(logits):
    n_tok = logits.shape[0]
    nt = n_tok // MOE_TM
    top_val, top_idx = lax.top_k(logits, TOP_K)
    gate = jax.nn.softmax(top_val, axis=-1)
    na = MOE_TM * TOP_K
    e3 = top_idx.reshape(nt, na)
    hot = e3[..., None] == jnp.arange(N_EXPERTS)
    onehot = hot.astype(jnp.int32)
    lower = (lax.broadcasted_iota(jnp.int32, (na, na), 0) >= lax.broadcasted_iota(jnp.int32, (na, na), 1))
    incl = jnp.einsum('ij,tjk->tik', lower.astype(BF16), hot.astype(BF16),
                      preferred_element_type=F32).astype(jnp.int32)
    rank = jnp.sum((incl - onehot) * onehot, axis=2)
    seg = (incl[:, -1] + MOE_GRAN - 1) // MOE_GRAN * MOE_GRAN
    loc_off = jnp.cumsum(seg, axis=1) - seg
    rows_e = jnp.sum(seg, axis=0)
    padded_sub = (rows_e + MOE_SUB - 1) // MOE_SUB * MOE_SUB
    padded_rb = (rows_e + MOE_RB - 1) // MOE_RB * MOE_RB
    blk_end = jnp.cumsum(padded_rb) // MOE_RB
    row_start = jnp.cumsum(padded_rb) - padded_rb
    seg_start = row_start[None, :] + jnp.cumsum(seg, axis=0) - seg
    slot = jnp.sum(onehot * loc_off[:, None, :], axis=2) + rank
    max_rows = n_tok * TOP_K + nt * N_EXPERTS * (MOE_GRAN - 1)
    n_blk = max_rows // MOE_RB + N_EXPERTS
    blk_ids = jnp.arange(n_blk)
    blk_e = jnp.minimum(jnp.searchsorted(blk_end, blk_ids, side='right'), N_EXPERTS - 1).astype(jnp.int32)
    rows_left = padded_sub[blk_e] - (blk_ids * MOE_RB - row_start[blk_e])
    blk_nsub = jnp.clip(rows_left // MOE_SUB, 0, MOE_RB // MOE_SUB).astype(jnp.int32)
    n_used = blk_end[-1:].astype(jnp.int32)
    tail_row = jnp.where(rows_e > 0, row_start + padded_sub - MOE_SUB, -1).astype(jnp.int32)
    layout = dict(seg_start=seg_start.reshape(-1).astype(jnp.int32),
                  seg_gran=(seg // MOE_GRAN).reshape(-1).astype(jnp.int32),
                  loc_off=loc_off.reshape(-1).astype(jnp.int32),
                  tail_row=tail_row)
    slot = slot.reshape(nt, MOE_TM, TOP_K).astype(jnp.int32)
    return blk_e, blk_nsub, n_used, layout, slot, gate.reshape(nt, MOE_TM, TOP_K), n_blk


def _segment_copies(layout_refs, tile, src_of, dst_of, sem, start):
    seg_start_ref, seg_gran_ref, loc_off_ref = layout_refs

    for e in range(N_EXPERTS):
        idx = tile * N_EXPERTS + e
        loc = loc_off_ref[idx]
        glob = seg_start_ref[idx]

        def body(c, carry, loc=loc, glob=glob):
            copy = pltpu.make_async_copy(
                src_of(pl.multiple_of(loc + c * MOE_GRAN, MOE_GRAN), pl.multiple_of(glob + c * MOE_GRAN, MOE_GRAN)),
                dst_of(pl.multiple_of(loc + c * MOE_GRAN, MOE_GRAN), pl.multiple_of(glob + c * MOE_GRAN, MOE_GRAN)),
                sem)
            if start:
                copy.start()
            else:
                copy.wait()
            return carry

        lax.fori_loop(0, seg_gran_ref[idx], body, 0)


def _slot_onehot(slot_vecs, base, shape, axis):
    ids = base + lax.broadcasted_iota(jnp.int32, shape, axis)
    hit = jnp.zeros(shape, F32)
    for v in slot_vecs:
        hit = jnp.where(ids == v, 1.0, hit)
    return hit.astype(BF16)


def _dispatch_kernel(seg_start_ref, seg_gran_ref, loc_off_ref, tail_ref, slot_ref, gate_ref, xa_ref, xb_ref,
                     xs_ref, gs_ref, sorted_scr, gsorted_scr, zero_scr, gzero_scr, sem, buf_sem, *, tiles_a):
    i = pl.program_id(0)

    @pl.when(i == 0)
    def _():
        zero_scr[...] = jnp.zeros_like(zero_scr)
        gzero_scr[...] = jnp.zeros_like(gzero_scr)

        def tail_copies(e):
            rows = pl.ds(pl.multiple_of(jnp.maximum(tail_ref[e], 0), MOE_SUB), MOE_SUB)
            return (pltpu.make_async_copy(zero_scr, xs_ref.at[rows], sem),
                    pltpu.make_async_copy(gzero_scr, gs_ref.at[rows], sem))

        for e in range(N_EXPERTS):
            @pl.when(tail_ref[e] >= 0)
            def _():
                for copy in tail_copies(e):
                    copy.start()
        for e in range(N_EXPERTS):
            @pl.when(tail_ref[e] >= 0)
            def _():
                for copy in tail_copies(e):
                    copy.wait()

    buf = lax.rem(i, 2)
    x = jnp.where(i < tiles_a, xa_ref[...], xb_ref[...])
    slot_vecs = [slot_ref[k:k + 1, :] for k in range(TOP_K)]
    for c in range(MOE_SLOTS // MOE_CHUNK):
        chunk = slice(c * MOE_CHUNK, (c + 1) * MOE_CHUNK)
        ids = c * MOE_CHUNK + lax.broadcasted_iota(jnp.int32, (MOE_CHUNK, MOE_TM), 0)
        hit = jnp.zeros((MOE_CHUNK, MOE_TM), F32)
        gval = jnp.zeros((MOE_CHUNK, MOE_TM), F32)
        for k in range(TOP_K):
            match = ids == slot_vecs[k]
            hit = jnp.where(match, 1.0, hit)
            gval = jnp.where(match, gate_ref[k:k + 1, :], gval)
        sorted_scr[buf, chunk, :] = jnp.dot(hit.astype(BF16), x,
                                            preferred_element_type=F32).astype(BF16)
        gsorted_scr[buf, chunk, :] = jnp.broadcast_to(jnp.sum(gval, axis=1, keepdims=True), (MOE_CHUNK, HEAD))

    layout_refs = (seg_start_ref, seg_gran_ref, loc_off_ref)

    def copies(tile, b, start):
        streams = (
            (lambda loc, glob: sorted_scr.at[b, pl.ds(loc, MOE_GRAN)],
             lambda loc, glob: xs_ref.at[pl.ds(glob, MOE_GRAN)]),
            (lambda loc, glob: gsorted_scr.at[b, pl.ds(loc, MOE_GRAN)],
             lambda loc, glob: gs_ref.at[pl.ds(glob, MOE_GRAN)]),
        )
        for src_of, dst_of in streams:
            _segment_copies(layout_refs, tile, src_of, dst_of, buf_sem.at[b], start=start)

    copies(i, buf, True)

    @pl.when(i > 0)
    def _():
        copies(i - 1, 1 - buf, False)

    @pl.when(i == pl.num_programs(0) - 1)
    def _():
        copies(i, buf, False)


def _dispatch(layout, slot, gate, hn_a, hn_b, n_rows):
    tiles_a, tiles_b = hn_a.shape[0] // MOE_TM, hn_b.shape[0] // MOE_TM
    slot_t = jnp.swapaxes(slot, 1, 2)
    gate_t = jnp.swapaxes(gate, 1, 2)
    grid_spec = pltpu.PrefetchScalarGridSpec(
        num_scalar_prefetch=4,
        grid=(tiles_a + tiles_b,),
        in_specs=[
            pl.BlockSpec((None, TOP_K, MOE_TM), lambda i, *_: (i, 0, 0)),
            pl.BlockSpec((None, TOP_K, MOE_TM), lambda i, *_: (i, 0, 0)),
            pl.BlockSpec((MOE_TM, D_MODEL), lambda i, *_: (jnp.minimum(i, tiles_a - 1), 0)),
            pl.BlockSpec((MOE_TM, D_MODEL), lambda i, *_: (jnp.maximum(i - tiles_a, 0), 0)),
        ],
        out_specs=(pl.BlockSpec(memory_space=pl.ANY), pl.BlockSpec(memory_space=pl.ANY)),
        scratch_shapes=[
            pltpu.VMEM((2, MOE_SLOTS, D_MODEL), BF16),
            pltpu.VMEM((2, MOE_SLOTS, HEAD), F32),
            pltpu.VMEM((MOE_SUB, D_MODEL), BF16),
            pltpu.VMEM((MOE_SUB, HEAD), F32),
            pltpu.SemaphoreType.DMA,
            pltpu.SemaphoreType.DMA((2,)),
        ],
    )
    return pl.pallas_call(
        functools.partial(_dispatch_kernel, tiles_a=tiles_a),
        out_shape=(jax.ShapeDtypeStruct((n_rows, D_MODEL), BF16), jax.ShapeDtypeStruct((n_rows, HEAD), F32)),
        grid_spec=grid_spec,
        compiler_params=_cparams(("arbitrary",)),
        name="moe_dispatch",
    )(layout["seg_start"], layout["seg_gran"], layout["loc_off"], layout["tail_row"], slot_t, gate_t, hn_a, hn_b)


def _moe_kernel(blk_e_ref, blk_nsub_ref, n_used_ref, xs_ref, gate_ref, wg_ref, wl_ref, bg_ref, bl_ref,
                wd_ref, bd_ref, yb_ref, wu_s, wd_s, act_s, y_s, pend_ref, sem):
    i = pl.program_id(0)
    f = pl.program_id(1)
    last_step = jnp.logical_and(i == pl.num_programs(0) - 1, f == MOE_NF - 1)

    def y_copies(blk, sb):
        rows = pl.ds(pl.multiple_of(sb * MOE_SUB, MOE_SUB), MOE_SUB)
        dst = pl.ds(pl.multiple_of(blk * MOE_RB + sb * MOE_SUB, MOE_SUB), MOE_SUB)
        return [pltpu.make_async_copy(y_s.at[part, rows], yb_ref.at[part, dst], sem) for part in range(2)]

    def drain():
        blk = pend_ref[1]

        def body(sb, c):
            for copy in y_copies(blk, sb):
                copy.wait()
            return c

        lax.fori_loop(0, pend_ref[0], body, 0)
        pend_ref[0] = 0

    @pl.when(jnp.logical_and(i == 0, f == 0))
    def _():
        pend_ref[0] = 0
        pend_ref[1] = 0

    def over_granules(nsub, fn):
        def octet(sb, c):
            fn(sb * (8 * MOE_SUB), 4 * MOE_SUB)
            fn(sb * (8 * MOE_SUB) + 4 * MOE_SUB, 4 * MOE_SUB)
            return c

        lax.fori_loop(0, nsub // 8, octet, 0)
        rem8 = nsub % 8
        rem = nsub % 4
        done = nsub - rem

        @pl.when(rem8 >= 4)
        def _():
            fn((nsub - rem8) * MOE_SUB, 4 * MOE_SUB)

        @pl.when(rem >= 2)
        def _():
            fn(done * MOE_SUB, 2 * MOE_SUB)

        @pl.when(rem % 2 == 1)
        def _():
            fn((nsub - 1) * MOE_SUB, MOE_SUB)

    @pl.when(i < n_used_ref[0])
    def _():
        wu_s[:, :MOE_TF] = wg_ref[...].astype(BF16)
        wu_s[:, MOE_TF:] = wl_ref[...].astype(BF16)
        wd_s[pl.ds(pl.multiple_of(f * MOE_TF, MOE_TF), MOE_TF), :] = wd_ref[...].astype(BF16)
        b_up = jnp.concatenate([bg_ref[...], bl_ref[...]], axis=1)
        nsub = blk_nsub_ref[i]

        def up(start, size):
            rows = pl.ds(pl.multiple_of(start, MOE_SUB), size)
            hh = jnp.dot(xs_ref[rows, :], wu_s[...], preferred_element_type=F32) + b_up
            glu = jnp.minimum(hh[:, :MOE_TF], SWIGLU_LIMIT)
            lin = jnp.clip(hh[:, MOE_TF:], -SWIGLU_LIMIT, SWIGLU_LIMIT)
            act_s[f, rows, :] = (glu * _sigmoid(SWIGLU_ALPHA * glu) * (lin + 1.0)).astype(BF16)

        over_granules(nsub, up)

        @pl.when(f == MOE_NF - 1)
        def _():
            drain()

            def down(start, size):
                rows = pl.ds(pl.multiple_of(start, MOE_SUB), size)
                act = jnp.concatenate([act_s[k, rows, :] for k in range(MOE_NF)], axis=1)
                y = (jnp.dot(act, wd_s[...], preferred_element_type=F32) + bd_ref[...]) * gate_ref[rows, 0:1]
                hi = y.astype(BF16)
                y_s[0, rows, :] = hi
                y_s[1, rows, :] = (y - hi.astype(F32)).astype(BF16)

            over_granules(nsub, down)

            def send(sb, c):
                for copy in y_copies(i, sb):
                    copy.start()
                return c

            lax.fori_loop(0, nsub, send, 0)
            pend_ref[0] = nsub
            pend_ref[1] = i

    @pl.when(last_step)
    def _():
        drain()


def _moe_experts(blk_e, blk_nsub, n_used, xs, gate_rows, w_up, b_up, w_down, b_down):
    n_blk = xs.shape[0] // MOE_RB

    def live(i, n_used_ref):
        return jnp.minimum(i, n_used_ref[0] - 1)

    def ff(i, f, n_used_ref):
        return jnp.where(i < n_used_ref[0], f, MOE_NF - 1)

    def expert(i, be, nu):
        return be[live(i, nu)]

    grid_spec = pltpu.PrefetchScalarGridSpec(
        num_scalar_prefetch=3,
        grid=(n_blk, MOE_NF),
        in_specs=[
            pl.BlockSpec((MOE_RB, D_MODEL), lambda i, f, be, ns, nu: (live(i, nu), 0)),
            pl.BlockSpec((MOE_RB, HEAD), lambda i, f, be, ns, nu: (live(i, nu), 0)),
            pl.BlockSpec((None, D_MODEL, MOE_TF), lambda i, f, be, ns, nu: (expert(i, be, nu), 0, ff(i, f, nu))),
            pl.BlockSpec((None, D_MODEL, MOE_TF),
                         lambda i, f, be, ns, nu: (expert(i, be, nu), 0, ff(i, f, nu) + MOE_NF)),
            pl.BlockSpec((None, 1, MOE_TF), lambda i, f, be, ns, nu: (expert(i, be, nu), 0, ff(i, f, nu))),
            pl.BlockSpec((None, 1, MOE_TF),
                         lambda i, f, be, ns, nu: (expert(i, be, nu), 0, ff(i, f, nu) + MOE_NF)),
            pl.BlockSpec((None, MOE_TF, D_MODEL), lambda i, f, be, ns, nu: (expert(i, be, nu), ff(i, f, nu), 0)),
            pl.BlockSpec((None, 1, D_MODEL), lambda i, f, be, ns, nu: (expert(i, be, nu), 0, 0)),
        ],
        out_specs=pl.BlockSpec(memory_space=pl.ANY),
        scratch_shapes=[
            pltpu.VMEM((D_MODEL, 2 * MOE_TF), BF16),
            pltpu.VMEM((D_FF, D_MODEL), BF16),
            pltpu.VMEM((MOE_NF, MOE_RB, MOE_TF), BF16),
            pltpu.VMEM((2, MOE_RB, D_MODEL), BF16),
            pltpu.SMEM((2,), jnp.int32),
            pltpu.SemaphoreType.DMA,
        ],
    )
    b_up3 = b_up.reshape(N_EXPERTS, 1, 2 * D_FF)
    return pl.pallas_call(
        _moe_kernel,
        out_shape=jax.ShapeDtypeStruct((2,) + xs.shape, BF16),
        grid_spec=grid_spec,
        compiler_params=_cparams(("arbitrary", "arbitrary")),
        name="moe_experts",
    )(blk_e, blk_nsub, n_used, xs, gate_rows, w_up, w_up, b_up3, b_up3, w_down,
      b_down.reshape(N_EXPERTS, 1, D_MODEL))


def _combine_kernel(seg_start_ref, seg_gran_ref, loc_off_ref, slot_ref, h_ref, yb_ref, y_ref, rows_scr, sem, *,
                    tile0):
    i = pl.program_id(0)

    @pl.when(i == 0)
    def _():
        rows_scr[...] = jnp.zeros_like(rows_scr)

    tile = tile0 + i
    per_chunk = MOE_CHUNK // MOE_GRAN
    last = tile * N_EXPERTS + N_EXPERTS - 1
    n_gran = loc_off_ref[last] // MOE_GRAN + seg_gran_ref[last]
    for e in range(N_EXPERTS):
        idx = tile * N_EXPERTS + e
        loc0 = loc_off_ref[idx]
        glob0 = seg_start_ref[idx]

        def issue(c, carry, loc0=loc0, glob0=glob0):
            loc = pl.multiple_of(loc0 + c * MOE_GRAN, MOE_GRAN)
            glob = pl.multiple_of(glob0 + c * MOE_GRAN, MOE_GRAN)
            for part in range(2):
                pltpu.make_async_copy(yb_ref.at[part, pl.ds(glob, MOE_GRAN)],
                                      rows_scr.at[part, pl.ds(loc, MOE_GRAN)], sem.at[loc // MOE_CHUNK]).start()
            return carry

        lax.fori_loop(0, seg_gran_ref[idx], issue, 0)

    slot_vecs = [slot_ref[:, k:k + 1] for k in range(TOP_K)]
    n_chunks = MOE_SLOTS // MOE_CHUNK
    y_ref[...] = h_ref[...]
    for half in [range(c0, c0 + 2) for c0 in range(0, n_chunks, 2)]:
        for c in half:
            def landed(g, carry, c=c):
                for part in range(2):
                    pltpu.make_async_copy(yb_ref.at[part, pl.ds(0, MOE_GRAN)],
                                          rows_scr.at[part, pl.ds(c * MOE_CHUNK, MOE_GRAN)], sem.at[c]).wait()
                return carry

            lax.fori_loop(0, jnp.clip(n_gran - c * per_chunk, 0, per_chunk), landed, 0)
        for c in half:
            pick = _slot_onehot(slot_vecs, c * MOE_CHUNK, (MOE_TM, MOE_CHUNK), 1)
            chunk = slice(c * MOE_CHUNK, (c + 1) * MOE_CHUNK)
            y_ref[...] += (jnp.dot(pick, rows_scr[0, chunk, :], preferred_element_type=F32)
                           + jnp.dot(pick, rows_scr[1, chunk, :], preferred_element_type=F32))


def _combine(layout, slot, h, yb, tile0):
    m = h.shape[0]
    grid_spec = pltpu.PrefetchScalarGridSpec(
        num_scalar_prefetch=3,
        grid=(m // MOE_TM,),
        in_specs=[
            pl.BlockSpec((None, MOE_TM, TOP_K), lambda i, *_: (i + tile0, 0, 0)),
            pl.BlockSpec((MOE_TM, D_MODEL), lambda i, *_: (i, 0)),
            pl.BlockSpec(memory_space=pl.ANY),
        ],
        out_specs=pl.BlockSpec((MOE_TM, D_MODEL), lambda i, *_: (i, 0)),
        scratch_shapes=[pltpu.VMEM((2, MOE_SLOTS, D_MODEL), BF16),
                        pltpu.SemaphoreType.DMA((MOE_SLOTS // MOE_CHUNK,))],
    )
    return pl.pallas_call(
        functools.partial(_combine_kernel, tile0=tile0),
        out_shape=jax.ShapeDtypeStruct((m, D_MODEL), F32),
        grid_spec=grid_spec,
        compiler_params=_cparams(("arbitrary",)),
        name="moe_combine",
    )(layout["seg_start"], layout["seg_gran"], layout["loc_off"], slot, h, yb)


def kernel(x_prompt, x_sample, cache_kv_w128, cache_kv_w512, cache_kv_w2048, state_hgrn, rel_bias, attn_norm,
           w_in, q_norm, k_norm, lb_raw, g_norm, w_pa, w_pb, w_o, ffn_norm, w_router, b_router, w_up, b_up,
           w_down, b_down):
    assert attn_norm.shape[0] == 1, "single-layer stack"
    bsz, seq, _ = x_prompt.shape
    nseq, t_new, _ = x_sample.shape
    n_p, n_s = bsz * seq, nseq * t_new
    lb = jax.nn.softmax(lb_raw.astype(F32), axis=0)[0]
    wpa, wpb, wo = w_pa[0].astype(BF16), w_pb[0].astype(BF16), w_o[0].astype(BF16)
    wr = jnp.pad(w_router[0].astype(F32), ((0, 0), (0, RT_LANES - N_EXPERTS)))
    wr_hi = wr.astype(BF16)
    wr_lo = (wr - wr_hi.astype(F32)).astype(BF16)
    br = jnp.pad(b_router[0].astype(F32), (0, RT_LANES - N_EXPERTS)).reshape(1, RT_LANES)
    caches = (cache_kv_w128[0], cache_kv_w512[0], cache_kv_w2048[0])

    xp = x_prompt.reshape(n_p, D_MODEL)
    xs = x_sample.reshape(n_s, D_MODEL)

    proj_p = _inproj(xp, attn_norm[0], w_in[0])
    qn_p, *kv_p, kvt_p = _qkv(proj_p, q_norm[0], k_norm[0])
    oa_p = _attn_prompt(qn_p, kv_p, jnp.stack([_prompt_bias(rel_bias, g) for g in range(3)]), bsz, seq)
    ob_p, st_p = _hgrn_prompt(proj_p, lb, g_norm[0], bsz, seq)
    mixed_p = _mix(oa_p, ob_p, proj_p, wpa, wpb)
    h_p, hn_p, lg_p = _resid(xp, mixed_p, wo, ffn_norm[0], wr_hi, wr_lo, br)

    proj_s = _inproj(xs, attn_norm[0], w_in[0])
    qn_s, *kv_s, _ = _qkv(proj_s, q_norm[0], k_norm[0])
    oa_s = _attn_sample(qn_s, kv_s, caches, rel_bias, t_new)
    ob_s, st_s = _hgrn_sample(proj_s, lb, g_norm[0], state_hgrn[0], t_new)
    mixed_s = _mix(oa_s, ob_s, proj_s, wpa, wpb)
    h_s, hn_s, lg_s = _resid(xs, mixed_s, wo, ffn_norm[0], wr_hi, wr_lo, br)

    logits = jnp.concatenate([lg_p, lg_s], axis=0)[:, :N_EXPERTS]
    assert n_p % MOE_TM == 0 and n_s % MOE_TM == 0, "each pass must fill whole MoE token tiles"
    blk_e, blk_nsub, n_used, layout, slot, gate, n_blk = _route(logits)
    xs_sorted, gate_rows = _dispatch(layout, slot, gate, hn_p, hn_s, n_blk * MOE_RB)
    yb = _moe_experts(blk_e, blk_nsub, n_used, xs_sorted, gate_rows, w_up[0], b_up[0], w_down[0], b_down[0])
    y_p = _combine(layout, slot, h_p, yb, 0)
    y_s = _combine(layout, slot, h_s, yb, n_p // MOE_TM)

    def prompt_rows(kv, win):
        keep = min(win, seq)
        rows = kv.reshape(bsz, seq, 2 * A_OUT)[:, seq - keep:]
        return rows.reshape(1, bsz, keep, 2, A_HPG, HEAD)

    kvp = [prompt_rows(kv_p[g], A_GROUPS[g][0]) for g in range(2)]
    wide = min(A_GROUPS[2][0], seq)
    kvp.append(kvt_p.reshape(bsz, seq, 2, A_HPG, HEAD)[None, :, seq - wide:])
    kvs = [kv_s[g].reshape(1, nseq, t_new, 2, A_HPG, HEAD) for g in range(3)]
    return (y_p.reshape(bsz, seq, D_MODEL), y_s.reshape(nseq, t_new, D_MODEL),
            kvp[0], kvp[1], kvp[2], st_p[None],
            kvs[0], kvs[1], kvs[2], st_s[None])
```
